```python
import jax, jax.numpy as jnp
from jax import lax
import numpy as np

D_MODEL = 1024
BATCH = 2
SEQ = 16384
DEPTH = 2

GRID_W = 64
CTX_LEN = 256
EPS = 1e-6
NEG_INF = -1e30

W_A = 256
W_B = 256
W_C = 256
W_D = 256
D_MIX = W_A + W_B + W_C + W_D
N_GROUPS = 4
GROUP_W = D_MIX // N_GROUPS
CONV_A = 3
CONV_B = 31
CHUNK = 128
H_C = 4
DH_C = W_C // H_C
H_D = 4
DH_D = W_D // H_D
WIN_R = 8
WIN_C = 16
OFF_A = 0
OFF_B = OFF_A + 3 * W_A
OFF_C = OFF_B + 2 * W_B
OFF_D = OFF_C + 2 * W_C
OFF_KV = OFF_D + W_D
IN_COLS = OFF_D + 3 * W_D
D_FF = 2816
N_EXPERTS = 8
TOP_K = 2
D_FF_EXPERT = 3584
MOE_BLOCK = 128
N_DENSE = (DEPTH + 1) // 2
N_MOE = DEPTH // 2

kernel_name = "hybrid_headgroup_dit_moe"


def rms_norm(x, g):
    xf = x.astype(jnp.float32)
    y = xf * lax.rsqrt(jnp.mean(xf * xf, axis=-1, keepdims=True) + EPS)
    return (y * g.astype(jnp.float32)).astype(x.dtype)


def layer_norm(x, g, b):
    xf = x.astype(jnp.float32)
    mu = jnp.mean(xf, axis=-1, keepdims=True)
    var = jnp.mean(jnp.square(xf - mu), axis=-1, keepdims=True)
    y = (xf - mu) * lax.rsqrt(var + EPS) * g.astype(jnp.float32) + b.astype(jnp.float32)
    return y.astype(x.dtype)


def modulate(h, shift, scale):
    return h * (1 + scale) + shift


def depthwise_conv(x, w):
    k, ch = w.shape
    return lax.conv_general_dilated(
        x, w.astype(x.dtype)[:, None, :], window_strides=(1,),
        padding=[((k - 1) // 2, k // 2)],
        dimension_numbers=("NWC", "WIO", "NWC"), feature_group_count=ch)


def short_conv(p, w):
    bg, cg, xa = p[..., :W_A], p[..., W_A:2 * W_A], p[..., 2 * W_A:]
    return bg * depthwise_conv(cg * xa, w)


def conformer_conv(p, w, bias, ln_g, ln_b):
    y = p[..., :W_B] * jax.nn.sigmoid(p[..., W_B:])
    y = depthwise_conv(y, w) + bias
    return jax.nn.silu(layer_norm(y, ln_g, ln_b))


def spatial_gating(p, ln_g, ln_b, w_s, b_s):
    b, s, _ = p.shape
    z = jax.nn.gelu(p)
    u, v = z[..., :W_C], z[..., W_C:]
    v = layer_norm(v, ln_g, ln_b).reshape(b, s // CHUNK, CHUNK, H_C, DH_C)
    v = jnp.einsum("hpq,bnqhd->bnphd", w_s, v) + b_s.T[:, :, None]
    return u * v.reshape(b, s, W_C)


def heads(t):
    return t.reshape(*t.shape[:-1], H_D, DH_D)


def neighbourhood_attention(q, k, v, k_ctx, v_ctx, rpb):
    b, n = q.shape[:2]
    rows = n // GRID_W
    kr = min(WIN_R, rows)
    kc = min(WIN_C, GRID_W)
    qg = (q * (DH_D ** -0.5)).reshape(b, rows, GRID_W, H_D, DH_D)
    kg = k.reshape(b, rows, GRID_W, H_D, DH_D)
    vg = v.reshape(b, rows, GRID_W, H_D, DH_D)
    r_idx = jnp.arange(rows)
    key_rows = jnp.clip(r_idx - kr // 2, 0, rows - kr)[:, None] + jnp.arange(kr)[None, :]
    k_blk = kg[:, key_rows]
    v_blk = vg[:, key_rows]
    c_idx = jnp.arange(GRID_W)
    c_start = jnp.clip(c_idx - kc // 2, 0, GRID_W - kc)
    col_ok = (c_idx[None, :] >= c_start[:, None]) & (c_idx[None, :] < c_start[:, None] + kc)
    dr = key_rows - r_idx[:, None] + (WIN_R - 1)
    dc = jnp.clip(c_idx[None, :] - c_idx[:, None], -(WIN_C - 1), WIN_C - 1) + (WIN_C - 1)
    bias = rpb[:, dr[:, None, :, None], dc[None, :, None, :]]
    s_loc = jnp.einsum("brqhd,brkchd->bhrqkc", qg, k_blk).astype(jnp.float32)
    s_loc = jnp.where(col_ok[:, None, :], s_loc + bias.astype(jnp.float32), NEG_INF)
    s_ctx = jnp.einsum("brqhd,blhd->bhrql", qg, k_ctx).astype(jnp.float32)
    n_loc = kr * GRID_W
    logits = jnp.concatenate([s_loc.reshape(b, H_D, rows, GRID_W, n_loc), s_ctx], axis=-1)
    prob = jax.nn.softmax(logits, axis=-1).astype(v.dtype)
    p_loc = prob[..., :n_loc].reshape(b, H_D, rows, GRID_W, kr, GRID_W)
    p_ctx = prob[..., n_loc:]
    out = (jnp.einsum("bhrqkc,brkchd->brqhd", p_loc, v_blk)
           + jnp.einsum("bhrql,blhd->brqhd", p_ctx, v_ctx))
    return out.reshape(b, n, W_D)


def context_attention(q, k, v):
    b, l = q.shape[:2]
    s = jnp.einsum("blhd,bmhd->bhlm", q * (DH_D ** -0.5), k).astype(jnp.float32)
    prob = jax.nn.softmax(s, axis=-1).astype(v.dtype)
    return jnp.einsum("bhlm,bmhd->blhd", prob, v).reshape(b, l, W_D)


def merge_groups(ys, group_g, w_out):
    y = jnp.concatenate(ys, axis=-1)
    y = rms_norm(y.reshape(*y.shape[:-1], N_GROUPS, GROUP_W), group_g.reshape(N_GROUPS, GROUP_W))
    return y.reshape(*y.shape[:-2], D_MIX) @ w_out


def hybrid_mixer(h, hc, w_in, conv_a_w, conv_b_w, conv_b_b, conv_ln_g, conv_ln_b,
                 sgu_ln_g, sgu_ln_b, sgu_w, sgu_b, rpb, group_g, w_out, with_ctx_out):
    p = h @ w_in
    q = heads(p[..., OFF_D:OFF_KV])
    k = heads(p[..., OFF_KV:OFF_KV + W_D])
    v = heads(p[..., OFF_KV + W_D:IN_COLS])
    kv_c = hc @ w_in[:, OFF_KV:IN_COLS]
    k_c, v_c = heads(kv_c[..., :W_D]), heads(kv_c[..., W_D:])
    y = merge_groups([
        short_conv(p[..., OFF_A:OFF_B], conv_a_w),
        conformer_conv(p[..., OFF_B:OFF_C], conv_b_w, conv_b_b, conv_ln_g, conv_ln_b),
        spatial_gating(p[..., OFF_C:OFF_D], sgu_ln_g, sgu_ln_b, sgu_w, sgu_b),
        neighbourhood_attention(q, k, v, k_c, v_c, rpb),
    ], group_g, w_out)
    if not with_ctx_out:
        return y, None
    pc = hc @ w_in[:, :OFF_KV]
    yc = merge_groups([
        short_conv(pc[..., OFF_A:OFF_B], conv_a_w),
        conformer_conv(pc[..., OFF_B:OFF_C], conv_b_w, conv_b_b, conv_ln_g, conv_ln_b),
        spatial_gating(pc[..., OFF_C:OFF_D], sgu_ln_g, sgu_ln_b, sgu_w, sgu_b),
        context_attention(heads(pc[..., OFF_D:OFF_KV]), k_c, v_c),
    ], group_g, w_out)
    return y, yc


def swiglu(h, w1, w3, w2):
    return (jax.nn.silu(h @ w1) * (h @ w3)) @ w2


def moe_swiglu(h, router_w, router_b, w1, w3, w2):
    b, s, d = h.shape
    xt = h.reshape(-1, d)
    t = xt.shape[0]
    logits = (xt @ router_w).astype(jnp.float32) + router_b.astype(jnp.float32)
    top_val, top_idx = lax.top_k(logits, TOP_K)
    gates = jax.nn.softmax(top_val, axis=-1)
    n_assign = t * TOP_K
    e_flat = top_idx.reshape(-1).astype(jnp.int32)
    tok_flat = jnp.repeat(jnp.arange(t, dtype=jnp.int32), TOP_K)
    g_flat = gates.reshape(-1)
    order = jnp.argsort(e_flat)
    e_s, tok_s, g_s = e_flat[order], tok_flat[order], g_flat[order]
    counts = jnp.bincount(e_flat, length=N_EXPERTS).astype(jnp.int32)
    padded = (counts + MOE_BLOCK - 1) // MOE_BLOCK * MOE_BLOCK
    pad_end = jnp.cumsum(padded)
    pad_start = pad_end - padded
    start = jnp.cumsum(counts) - counts
    pos = pad_start[e_s] + jnp.arange(n_assign, dtype=jnp.int32) - start[e_s]
    total = -(-n_assign // MOE_BLOCK) * MOE_BLOCK + N_EXPERTS * MOE_BLOCK
    n_blk = total // MOE_BLOCK
    buf_tok = jnp.full((total,), t, jnp.int32).at[pos].set(tok_s)
    buf_gate = jnp.zeros((total,), jnp.float32).at[pos].set(g_s)
    blk_exp = jnp.minimum(jnp.searchsorted(pad_end, jnp.arange(n_blk, dtype=jnp.int32) * MOE_BLOCK,
                                           side="right"), N_EXPERTS - 1)
    x_pad = jnp.concatenate([xt, jnp.zeros((1, d), xt.dtype)], axis=0)
    xb = x_pad[buf_tok].reshape(n_blk, MOE_BLOCK, d)

    def expert_block(args):
        xblk, e = args
        return swiglu(xblk, w1[e], w3[e], w2[e])

    yb = lax.map(expert_block, (xb, blk_exp)).reshape(total, d)
    yb = yb * buf_gate[:, None].astype(yb.dtype)
    out = jnp.zeros((t + 1, d), yb.dtype).at[buf_tok].add(yb)[:t]
    return out.reshape(b, s, d)


def setup_inputs(seed: int = 0) -> dict:
    key = jax.random.key(seed)
    ks = jax.random.split(key, 32)
    L = DEPTH

    def nrm(k, shape, scale):
        return jax.random.normal(k, shape, jnp.float32) * scale

    return {
        "x": nrm(ks[0], (BATCH, SEQ, D_MODEL), 1.0),
        "c": nrm(ks[1], (BATCH, D_MODEL), 1.0),
        "ctx": nrm(ks[2], (BATCH, CTX_LEN, D_MODEL), 1.0),
        "c_ctx": nrm(ks[3], (D_MODEL,), 1.0),
        "w_mod": nrm(ks[4], (L, D_MODEL, 6 * D_MODEL), 0.5 * D_MODEL ** -0.5),
        "b_mod": nrm(ks[5], (L, 6 * D_MODEL), 0.01),
        "norm1_g": 1.0 + nrm(ks[6], (L, D_MODEL), 0.1),
        "norm2_g": 1.0 + nrm(ks[7], (L, D_MODEL), 0.1),
        "w_in": nrm(ks[8], (L, D_MODEL, IN_COLS), D_MODEL ** -0.5),
        "conv_a_w": nrm(ks[9], (L, CONV_A, W_A), CONV_A ** -0.5),
        "conv_b_w": nrm(ks[10], (L, CONV_B, W_B), CONV_B ** -0.5),
        "conv_b_b": nrm(ks[11], (L, W_B), 0.02),
        "conv_ln_g": 1.0 + nrm(ks[12], (L, W_B), 0.1),
        "conv_ln_b": nrm(ks[13], (L, W_B), 0.1),
        "sgu_ln_g": 1.0 + nrm(ks[14], (L, W_C), 0.1),
        "sgu_ln_b": nrm(ks[15], (L, W_C), 0.1),
        "sgu_w": nrm(ks[16], (L, H_C, CHUNK, CHUNK), CHUNK ** -0.5),
        "sgu_b": 1.0 + nrm(ks[17], (L, H_C, CHUNK), 0.1),
        "rpb": nrm(ks[18], (L, H_D, 2 * WIN_R - 1, 2 * WIN_C - 1), 0.1),
        "group_g": 1.0 + nrm(ks[19], (L, D_MIX), 0.1),
        "w_out": nrm(ks[20], (L, D_MIX, D_MODEL), D_MIX ** -0.5),
        "ffn_w1": nrm(ks[21], (N_DENSE, D_MODEL, D_FF), D_MODEL ** -0.5),
        "ffn_w3": nrm(ks[22], (N_DENSE, D_MODEL, D_FF), D_MODEL ** -0.5),
        "ffn_w2": nrm(ks[23], (N_DENSE, D_FF, D_MODEL), D_FF ** -0.5),
        "router_w": nrm(ks[24], (N_MOE, D_MODEL, N_EXPERTS), D_MODEL ** -0.5),
        "router_b": nrm(ks[25], (N_MOE, N_EXPERTS), 0.01),
        "moe_w1": nrm(ks[26], (N_MOE, N_EXPERTS, D_MODEL, D_FF_EXPERT), D_MODEL ** -0.5),
        "moe_w3": nrm(ks[27], (N_MOE, N_EXPERTS, D_MODEL, D_FF_EXPERT), D_MODEL ** -0.5),
        "moe_w2": nrm(ks[28], (N_MOE, N_EXPERTS, D_FF_EXPERT, D_MODEL), D_FF_EXPERT ** -0.5),
        "final_g": 1.0 + nrm(ks[29], (D_MODEL,), 0.1),
    }


def reference(x, c, ctx, c_ctx, w_mod, b_mod, norm1_g, norm2_g, w_in, conv_a_w, conv_b_w,
              conv_b_b, conv_ln_g, conv_ln_b, sgu_ln_g, sgu_ln_b, sgu_w, sgu_b, rpb, group_g,
              w_out, ffn_w1, ffn_w3, ffn_w2, router_w, router_b, moe_w1, moe_w3, moe_w2, final_g):
    xl, xc = x, ctx
    n_ctx = ctx.shape[1]
    sc_lat = jax.nn.silu(c)
    sc_ctx = jax.nn.silu(c_ctx)
    for l in range(DEPTH):
        last = l == DEPTH - 1
        mod = (sc_lat @ w_mod[l] + b_mod[l])[:, None, :]
        mod_c = sc_ctx @ w_mod[l] + b_mod[l]
        sh1, s1, g1, sh2, s2, g2 = jnp.split(mod, 6, axis=-1)
        csh1, cs1, cg1, csh2, cs2, cg2 = jnp.split(mod_c, 6, axis=-1)
        h = modulate(rms_norm(xl, norm1_g[l]), sh1, s1)
        hc = modulate(rms_norm(xc, norm1_g[l]), csh1, cs1)
        y, yc = hybrid_mixer(h, hc, w_in[l], conv_a_w[l], conv_b_w[l], conv_b_b[l], conv_ln_g[l],
                             conv_ln_b[l], sgu_ln_g[l], sgu_ln_b[l], sgu_w[l], sgu_b[l], rpb[l],
                             group_g[l], w_out[l], not last)
        xl = xl + g1 * y
        h = modulate(rms_norm(xl, norm2_g[l]), sh2, s2)
        if not last:
            xc = xc + cg1 * yc
            hc = modulate(rms_norm(xc, norm2_g[l]), csh2, cs2)
            h = jnp.concatenate([hc, h], axis=1)
        if l % 2 == 0:
            f = swiglu(h, ffn_w1[l // 2], ffn_w3[l // 2], ffn_w2[l // 2])
        else:
            f = moe_swiglu(h, router_w[l // 2], router_b[l // 2], moe_w1[l // 2],
                           moe_w3[l // 2], moe_w2[l // 2])
        if not last:
            xc = xc + cg2 * f[:, :n_ctx]
            f = f[:, n_ctx:]
        xl = xl + g2 * f
    return rms_norm(xl, final_g)
```

```python
import functools

import numpy as np
import jax
import jax.numpy as jnp
from jax import lax
from jax.experimental import pallas as pl
from jax.experimental.pallas import tpu as pltpu

F32 = jnp.float32
BF16 = jnp.bfloat16

D_MODEL = 1024
GRID_W = 64
EPS = 1e-6
NEG_INF = -1e30
GROUP_W = 256
N_HEADS = 4
HEAD_W = GROUP_W // N_HEADS
CONV_A = 3
CONV_B = 31
CHUNK = 128
WIN_R = 8
WIN_C = 16
OFF_B = 3 * GROUP_W
OFF_C = OFF_B + 2 * GROUP_W
OFF_D = OFF_C + 2 * GROUP_W
OFF_KV = OFF_D + GROUP_W
IN_COLS = OFF_KV + 2 * GROUP_W
N_EXPERTS = 8
LANES = 128
HALO = 16
Q_BLOCK_ROWS = 8
KEY_ROWS_BEFORE = 4
KEY_ROWS_AFTER = 4
VMEM_LIMIT = 56 * 1024 * 1024

M_SH1, M_SC1, M_G1, M_SH2, M_SC2, M_G2 = range(6)


def _params(*dims):
    return pltpu.CompilerParams(dimension_semantics=dims, vmem_limit_bytes=VMEM_LIMIT)


def _rms(x, g):
    return x * lax.rsqrt(jnp.mean(x * x, axis=-1, keepdims=True) + EPS) * g


def _layer_norm(x, g, b):
    mu = jnp.mean(x, axis=-1, keepdims=True)
    xc = x - mu
    var = jnp.mean(xc * xc, axis=-1, keepdims=True)
    return xc * lax.rsqrt(var + EPS) * g + b


def _mod_body(c_ref, w_ref, b_ref, o_ref):
    s = c_ref[...]
    s = s * jax.nn.sigmoid(s)
    o_ref[0] = jnp.dot(s, w_ref[0], preferred_element_type=F32,
                       precision=lax.Precision.HIGHEST) + b_ref[0]


def _modulation(c8, w_mod, b_mod):
    n_layers, d, six_d = w_mod.shape
    return pl.pallas_call(
        _mod_body,
        grid=(n_layers, six_d // d),
        in_specs=[pl.BlockSpec((8, d), lambda l, j: (0, 0)),
                  pl.BlockSpec((1, d, d), lambda l, j: (l, 0, j)),
                  pl.BlockSpec((1, 1, d), lambda l, j: (l, 0, j))],
        out_specs=pl.BlockSpec((1, 8, d), lambda l, j: (l, 0, j)),
        out_shape=jax.ShapeDtypeStruct((n_layers, 8, six_d), F32),
        compiler_params=_params("arbitrary", "arbitrary"),
        name="modulation",
    )(c8, w_mod, b_mod.reshape(n_layers, 1, six_d))


def _norm_proj_body(x_ref, g_ref, mod_ref, w_ref, p_ref, q_ref, kv_ref):
    m = mod_ref[0]
    h = _rms(x_ref[0], g_ref[...]) * (1.0 + m[M_SC1:M_SC1 + 1]) + m[M_SH1:M_SH1 + 1]
    hb = h.astype(BF16)
    p_ref[0] = jnp.dot(hb, w_ref[:, :OFF_D], preferred_element_type=F32)
    q = jnp.dot(hb, w_ref[:, OFF_D:OFF_KV], preferred_element_type=F32)
    q_ref[0] = (q * (HEAD_W ** -0.5)).astype(BF16)
    kv_ref[0] = jnp.dot(hb, w_ref[:, OFF_KV:], preferred_element_type=F32).astype(BF16)


def _norm_proj(x, g, mod, mod_row, w_in_bf16, tm):
    b, s, d = x.shape
    return pl.pallas_call(
        _norm_proj_body,
        grid=(b, s // tm),
        in_specs=[pl.BlockSpec((1, tm, d), lambda bi, i: (bi, i, 0)),
                  pl.BlockSpec((1, d), lambda bi, i: (0, 0)),
                  pl.BlockSpec((1, 6, d), lambda bi, i: (mod_row(bi), 0, 0)),
                  pl.BlockSpec((d, IN_COLS), lambda bi, i: (0, 0))],
        out_specs=[pl.BlockSpec((1, tm, OFF_D), lambda bi, i: (bi, i, 0)),
                   pl.BlockSpec((1, tm, GROUP_W), lambda bi, i: (bi, i, 0)),
                   pl.BlockSpec((1, tm, 2 * GROUP_W), lambda bi, i: (bi, i, 0))],
        out_shape=[jax.ShapeDtypeStruct((b, s, OFF_D), F32),
                   jax.ShapeDtypeStruct((b, s, GROUP_W), BF16),
                   jax.ShapeDtypeStruct((b, s, 2 * GROUP_W), BF16)],
        compiler_params=_params("parallel", "parallel"),
        name="norm_proj",
    )(x, g.reshape(1, d), mod, w_in_bf16)


def _conv_inputs(blk):
    za = blk[:, GROUP_W:2 * GROUP_W] * blk[:, 2 * GROUP_W:3 * GROUP_W]
    zb = blk[:, OFF_B:OFF_B + GROUP_W] * jax.nn.sigmoid(blk[:, OFF_B + GROUP_W:OFF_C])
    return za, zb


def _mixer_body(prev_ref, cur_ref, next_ref, wa_ref, wb_ref, bb_ref, blg_ref, blb_ref,
                slg_ref, slb_ref, sw_ref, sbias_ref, gg_ref, o_ref, za_s, zb_s, *, ts, rc):
    i = pl.program_id(1)
    n = pl.num_programs(1)
    has_prev = (i > 0).astype(F32)
    has_next = (i < n - 1).astype(F32)

    pa, pb = _conv_inputs(prev_ref[0])
    za_s[0:HALO] = pa * has_prev
    zb_s[0:HALO] = pb * has_prev
    ca, cb = _conv_inputs(cur_ref[0])
    za_s[HALO:HALO + ts] = ca
    zb_s[HALO:HALO + ts] = cb
    na, nb = _conv_inputs(next_ref[0])
    za_s[HALO + ts:] = na * has_next
    zb_s[HALO + ts:] = nb * has_next

    wa = wa_ref[...]
    wb = wb_ref[...]
    lane_head = lax.broadcasted_iota(jnp.int32, (1, GROUP_W), 1) // HEAD_W
    for r0 in range(0, ts, rc):
        acc = wa[0:1] * za_s[HALO + r0 - 1:HALO + r0 - 1 + rc]
        for j in range(1, CONV_A):
            acc = acc + wa[j:j + 1] * za_s[HALO + r0 - 1 + j:HALO + r0 - 1 + j + rc]
        ya = cur_ref[0, r0:r0 + rc, 0:GROUP_W] * acc
        o_ref[0, r0:r0 + rc, 0:GROUP_W] = _rms(ya, gg_ref[:, 0:GROUP_W]).astype(o_ref.dtype)

        base = HALO + r0 - CONV_B // 2
        acc = wb[0:1] * zb_s[base:base + rc]
        for j in range(1, CONV_B):
            acc = acc + wb[j:j + 1] * zb_s[base + j:base + j + rc]
        yb = _layer_norm(acc + bb_ref[...], blg_ref[...], blb_ref[...])
        yb = yb * jax.nn.sigmoid(yb)
        o_ref[0, r0:r0 + rc, GROUP_W:2 * GROUP_W] = _rms(yb, gg_ref[:, GROUP_W:2 * GROUP_W]).astype(o_ref.dtype)

    for r0 in range(0, ts, CHUNK):
        z = jax.nn.gelu(cur_ref[0, r0:r0 + CHUNK, OFF_C:OFF_D])
        u = z[:, :GROUP_W]
        v = _layer_norm(z[:, GROUP_W:], slg_ref[...], slb_ref[...]).astype(BF16)
        mixed = sbias_ref[...]
        for h in range(N_HEADS):
            vh = v * (lane_head == h).astype(BF16)
            mixed = mixed + jnp.dot(sw_ref[h], vh, preferred_element_type=F32)
        yc = u * mixed
        o_ref[0, r0:r0 + CHUNK, 2 * GROUP_W:3 * GROUP_W] = _rms(yc, gg_ref[:, 2 * GROUP_W:3 * GROUP_W]).astype(o_ref.dtype)


def _conv_mixer(p, wa, wb, bb, blg, blb, slg, slb, sw_bf16, sbias, gg, ts):
    b, s, _ = p.shape
    rc = min(64, ts)
    hb = ts // HALO
    n_halo = s // HALO
    row = lambda a: a.reshape(1, -1)
    const = lambda *shape: pl.BlockSpec(shape, lambda bi, i: (0,) * len(shape))
    return pl.pallas_call(
        functools.partial(_mixer_body, ts=ts, rc=rc),
        grid=(b, s // ts),
        in_specs=[pl.BlockSpec((1, HALO, OFF_D), lambda bi, i: (bi, jnp.maximum(i * hb - 1, 0), 0)),
                  pl.BlockSpec((1, ts, OFF_D), lambda bi, i: (bi, i, 0)),
                  pl.BlockSpec((1, HALO, OFF_D), lambda bi, i: (bi, jnp.minimum((i + 1) * hb, n_halo - 1), 0)),
                  const(CONV_A, GROUP_W), const(CONV_B, GROUP_W), const(1, GROUP_W),
                  const(1, GROUP_W), const(1, GROUP_W), const(1, GROUP_W), const(1, GROUP_W),
                  const(N_HEADS, CHUNK, CHUNK), const(CHUNK, GROUP_W), const(1, 3 * GROUP_W)],
        out_specs=pl.BlockSpec((1, ts, 3 * GROUP_W), lambda bi, i: (bi, i, 0)),
        out_shape=jax.ShapeDtypeStruct((b, s, 3 * GROUP_W), BF16),
        scratch_shapes=[pltpu.VMEM((ts + 2 * HALO, GROUP_W), F32),
                        pltpu.VMEM((ts + 2 * HALO, GROUP_W), F32)],
        compiler_params=_params("parallel", "parallel"),
        name="conv_mixer",
    )(p, p, p, wa, wb, row(bb), row(blg), row(blb), row(slg), row(slb), sw_bf16, sbias, row(gg))


def _head_masks():
    lane_head = lax.broadcasted_iota(jnp.int32, (1, GROUP_W), 1) // HEAD_W
    return [(lane_head == h).astype(BF16) for h in range(N_HEADS)]


_NT = (((1,), (1,)), ((), ()))


def _attend(q, masks, keys, values, biases, acc):
    for h in range(N_HEADS):
        qh = q * masks[h]
        scores = []
        for k, bias in zip(keys, biases):
            s = lax.dot_general(qh, k, _NT, preferred_element_type=F32)
            scores.append(s if bias is None else s + bias(h))
        m = scores[0].max(axis=-1, keepdims=True)
        for s in scores[1:]:
            m = jnp.maximum(m, s.max(axis=-1, keepdims=True))
        probs = [jnp.exp(s - m) for s in scores]
        denom = probs[0].sum(axis=-1, keepdims=True)
        for p in probs[1:]:
            denom = denom + p.sum(axis=-1, keepdims=True)
        o = jnp.dot(probs[0].astype(BF16), values[0](h), preferred_element_type=F32)
        for p, v in zip(probs[1:], values[1:]):
            o = o + jnp.dot(p.astype(BF16), v(h), preferred_element_type=F32)
        acc = acc + o / denom
    return acc


def _nbr_attn_body(q_ref, kp_ref, kc_ref, kn_ref, kx_ref, bias_ref, gg_ref, o_ref,
                   k_s, v_s, kx_s, vx_s, *, tq, n_sub):
    masks = _head_masks()
    n_half = kp_ref.shape[1]
    n_cur = kc_ref.shape[1]
    pieces = ((kp_ref, 0, n_half), (kc_ref, n_half, n_cur), (kn_ref, n_half + n_cur, n_half))
    for ref, off, n in pieces:
        k_s[off:off + n] = ref[0, :, 0:GROUP_W]
        for h in range(N_HEADS):
            v_s[h, off:off + n] = ref[0, :, GROUP_W:] * masks[h]
    kx_s[...] = kx_ref[0, :, 0:GROUP_W]
    for h in range(N_HEADS):
        vx_s[h] = kx_ref[0, :, GROUP_W:] * masks[h]

    def sub(sb, carry):
        r0 = pl.multiple_of(sb * tq, tq)
        q = q_ref[0, pl.ds(r0, tq), :]
        acc = _attend(
            q, masks,
            keys=[k_s[...], kx_s[...]],
            values=[lambda h: v_s[h], lambda h: vx_s[h]],
            biases=[lambda h: bias_ref[0, h, pl.ds(r0, tq), :], None],
            acc=jnp.zeros((tq, GROUP_W), F32))
        o_ref[0, pl.ds(r0, tq), :] = _rms(acc, gg_ref[...]).astype(o_ref.dtype)
        return carry

    lax.fori_loop(0, n_sub, sub, 0)


def _window_bias(rpb_l, rows):
    nb = rows // Q_BLOCK_ROWS
    n_key_rows = KEY_ROWS_BEFORE + Q_BLOCK_ROWS + KEY_ROWS_AFTER
    i = np.arange(Q_BLOCK_ROWS)[:, None, None, None]
    c = np.arange(GRID_W)[None, :, None, None]
    kr = np.arange(-KEY_ROWS_BEFORE, Q_BLOCK_ROWS + KEY_ROWS_AFTER)[None, None, :, None]
    kc = np.arange(GRID_W)[None, None, None, :]
    shape = (Q_BLOCK_ROWS, GRID_W, n_key_rows, GRID_W)
    tables = []
    for jv in (0, min(1, nb - 1), nb - 1):
        r = Q_BLOCK_ROWS * jv + i
        ks = np.clip(r - WIN_R // 2, 0, rows - WIN_R)
        krow = Q_BLOCK_ROWS * jv + kr
        c_start = np.clip(c - WIN_C // 2, 0, GRID_W - WIN_C)
        valid = ((krow >= ks) & (krow < ks + WIN_R) & (krow >= 0) & (krow < rows)
                 & (kc >= c_start) & (kc < c_start + WIN_C))
        dr = np.broadcast_to(np.clip(krow - r + (WIN_R - 1), 0, 2 * WIN_R - 2), shape)
        dc = np.broadcast_to(np.clip(kc - c, -(WIN_C - 1), WIN_C - 1) + (WIN_C - 1), shape)
        bias = rpb_l[:, dr, dc]
        bias = jnp.where(np.broadcast_to(valid, shape)[None], bias, NEG_INF)
        tables.append(bias.reshape(N_HEADS, Q_BLOCK_ROWS * GRID_W, n_key_rows * GRID_W))
    return jnp.stack(tables)


def _nbr_attention(q, kv, kv_ctx, bias, gg):
    b, s, _ = q.shape
    n_ctx = kv_ctx.shape[1]
    tb = Q_BLOCK_ROWS * GRID_W
    half = KEY_ROWS_BEFORE * GRID_W
    nb = s // tb
    assert nb >= 2 and KEY_ROWS_BEFORE == KEY_ROWS_AFTER and tb == 2 * half
    n_loc = tb + 2 * half
    tq = 128
    variant = lambda j: jnp.where(j == 0, 0, jnp.where(j == nb - 1, 2, 1))
    return pl.pallas_call(
        functools.partial(_nbr_attn_body, tq=tq, n_sub=tb // tq),
        grid=(b, nb),
        in_specs=[pl.BlockSpec((1, tb, GROUP_W), lambda bi, j: (bi, j, 0)),
                  pl.BlockSpec((1, half, 2 * GROUP_W), lambda bi, j: (bi, jnp.maximum(2 * j - 1, 0), 0)),
                  pl.BlockSpec((1, tb, 2 * GROUP_W), lambda bi, j: (bi, j, 0)),
                  pl.BlockSpec((1, half, 2 * GROUP_W), lambda bi, j: (bi, jnp.minimum(2 * j + 2, 2 * nb - 1), 0)),
                  pl.BlockSpec((1, n_ctx, 2 * GROUP_W), lambda bi, j: (bi, 0, 0)),
                  pl.BlockSpec((1, N_HEADS, tb, n_loc), lambda bi, j: (variant(j), 0, 0, 0)),
                  pl.BlockSpec((1, GROUP_W), lambda bi, j: (0, 0))],
        out_specs=pl.BlockSpec((1, tb, GROUP_W), lambda bi, j: (bi, j, 0)),
        out_shape=jax.ShapeDtypeStruct((b, s, GROUP_W), BF16),
        scratch_shapes=[pltpu.VMEM((n_loc, GROUP_W), BF16),
                        pltpu.VMEM((N_HEADS, n_loc, GROUP_W), BF16),
                        pltpu.VMEM((n_ctx, GROUP_W), BF16),
                        pltpu.VMEM((N_HEADS, n_ctx, GROUP_W), BF16)],
        compiler_params=_params("parallel", "arbitrary"),
        name="nbr_attention",
    )(q, kv, kv, kv, kv_ctx, bias, gg.reshape(1, GROUP_W))


def _ctx_attn_body(q_ref, kx_ref, gg_ref, o_ref, vx_s):
    masks = _head_masks()
    for h in range(N_HEADS):
        vx_s[h] = kx_ref[0, :, GROUP_W:] * masks[h]
    tq = q_ref.shape[1]
    acc = _attend(q_ref[0], masks, keys=[kx_ref[0, :, 0:GROUP_W]], values=[lambda h: vx_s[h]],
                  biases=[None], acc=jnp.zeros((tq, GROUP_W), F32))
    o_ref[0] = _rms(acc, gg_ref[...]).astype(o_ref.dtype)


def _ctx_attention(q, kv_ctx, gg):
    b, n_ctx, _ = q.shape
    return pl.pallas_call(
        _ctx_attn_body,
        grid=(b,),
        in_specs=[pl.BlockSpec((1, n_ctx, GROUP_W), lambda bi: (bi, 0, 0)),
                  pl.BlockSpec((1, n_ctx, 2 * GROUP_W), lambda bi: (bi, 0, 0)),
                  pl.BlockSpec((1, GROUP_W), lambda bi: (0, 0))],
        out_specs=pl.BlockSpec((1, n_ctx, GROUP_W), lambda bi: (bi, 0, 0)),
        out_shape=jax.ShapeDtypeStruct((b, n_ctx, GROUP_W), BF16),
        scratch_shapes=[pltpu.VMEM((N_HEADS, n_ctx, GROUP_W), BF16)],
        compiler_params=_params("parallel"),
        name="ctx_attention",
    )(q, kv_ctx, gg.reshape(1, GROUP_W))


def _merge_body(abc_ref, dn_ref, w_ref, x_ref, mod_ref, g_ref, *rest, with_router):
    if with_router:
        rw_ref, rb_ref, xo_ref, h_ref, info_ref = rest
    else:
        xo_ref, h_ref = rest
    y = (jnp.dot(abc_ref[0], w_ref[0:3 * GROUP_W], preferred_element_type=F32)
         + jnp.dot(dn_ref[0], w_ref[3 * GROUP_W:], preferred_element_type=F32))
    m = mod_ref[0]
    xn = x_ref[0] + m[M_G1:M_G1 + 1] * y
    xo_ref[0] = xn
    h = _rms(xn, g_ref[...]) * (1.0 + m[M_SC2:M_SC2 + 1]) + m[M_SH2:M_SH2 + 1]
    h_ref[0] = h.astype(h_ref.dtype)
    if with_router:
        logits = jnp.dot(h.astype(BF16), rw_ref[...], preferred_element_type=F32) + rb_ref[...]
        lane = lax.broadcasted_iota(jnp.int32, logits.shape, 1).astype(F32)
        m1 = logits.max(axis=-1, keepdims=True)
        i1 = jnp.where(logits == m1, lane, float(LANES)).min(axis=-1, keepdims=True)
        rest_logits = jnp.where(lane == i1, NEG_INF, logits)
        m2 = rest_logits.max(axis=-1, keepdims=True)
        i2 = jnp.where(rest_logits == m2, lane, float(LANES)).min(axis=-1, keepdims=True)
        e2 = jnp.exp(m2 - m1)
        den = 1.0 + e2
        info = jnp.where(lane == 0, i1, jnp.where(lane == 1, i2,
               jnp.where(lane == 2, 1.0 / den, jnp.where(lane == 3, e2 / den, 0.0))))
        info_ref[0] = info


def _merge(abc, dn, w_out_bf16, x, mod, mod_row, g2, tm, h_dtype, router=None):
    b, s, d = x.shape
    in_specs = [pl.BlockSpec((1, tm, 3 * GROUP_W), lambda bi, i: (bi, i, 0)),
                pl.BlockSpec((1, tm, GROUP_W), lambda bi, i: (bi, i, 0)),
                pl.BlockSpec((4 * GROUP_W, d), lambda bi, i: (0, 0)),
                pl.BlockSpec((1, tm, d), lambda bi, i: (bi, i, 0)),
                pl.BlockSpec((1, 6, d), lambda bi, i: (mod_row(bi), 0, 0)),
                pl.BlockSpec((1, d), lambda bi, i: (0, 0))]
    out_specs = [pl.BlockSpec((1, tm, d), lambda bi, i: (bi, i, 0)),
                 pl.BlockSpec((1, tm, d), lambda bi, i: (bi, i, 0))]
    out_shape = [jax.ShapeDtypeStruct((b, s, d), F32), jax.ShapeDtypeStruct((b, s, d), h_dtype)]
    args = [abc, dn, w_out_bf16, x, mod, g2.reshape(1, d)]
    if router is not None:
        in_specs += [pl.BlockSpec((d, LANES), lambda bi, i: (0, 0)),
                     pl.BlockSpec((1, LANES), lambda bi, i: (0, 0))]
        out_specs.append(pl.BlockSpec((1, tm, LANES), lambda bi, i: (bi, i, 0)))
        out_shape.append(jax.ShapeDtypeStruct((b, s, LANES), F32))
        args += list(router)
    return pl.pallas_call(
        functools.partial(_merge_body, with_router=router is not None),
        grid=(b, s // tm),
        in_specs=in_specs, out_specs=out_specs, out_shape=out_shape,
        compiler_params=_params("parallel", "parallel"),
        name="merge_router" if router is not None else "merge",
    )(*args)


def _ffn_body(h_ref, w1_ref, w3_ref, w2_ref, x_ref, mod_ref, o_ref, acc_ref):
    k = pl.program_id(1)
    h = h_ref[...]
    a = jnp.dot(h, w1_ref[...], preferred_element_type=F32)
    g = jnp.dot(h, w3_ref[...], preferred_element_type=F32)
    part = jnp.dot((a * jax.nn.sigmoid(a) * g).astype(BF16), w2_ref[...], preferred_element_type=F32)

    @pl.when(k == 0)
    def _():
        acc_ref[...] = part

    @pl.when(k > 0)
    def _():
        acc_ref[...] += part

    @pl.when(k == pl.num_programs(1) - 1)
    def _():
        o_ref[...] = x_ref[...] + mod_ref[0, M_G2:M_G2 + 1] * acc_ref[...]


def _dense_ffn(h, w1, w3, w2, x, mod, mod_row, tm, tf):
    t, d = h.shape
    f = w1.shape[1]
    return pl.pallas_call(
        _ffn_body,
        grid=(t // tm, f // tf),
        in_specs=[pl.BlockSpec((tm, d), lambda i, k: (i, 0)),
                  pl.BlockSpec((d, tf), lambda i, k: (0, k)),
                  pl.BlockSpec((d, tf), lambda i, k: (0, k)),
                  pl.BlockSpec((tf, d), lambda i, k: (k, 0)),
                  pl.BlockSpec((tm, d), lambda i, k: (i, 0)),
                  pl.BlockSpec((1, 6, d), lambda i, k: (mod_row(i), 0, 0))],
        out_specs=pl.BlockSpec((tm, d), lambda i, k: (i, 0)),
        out_shape=jax.ShapeDtypeStruct((t, d), F32),
        scratch_shapes=[pltpu.VMEM((tm, d), F32)],
        compiler_params=_params("parallel", "arbitrary"),
        name="dense_ffn",
    )(h, w1, w3, w2, x, mod)


def _route_plan(expert_idx, tm):
    n_assign = expert_idx.size
    e_flat = expert_idx.reshape(-1)
    onehot = (e_flat[:, None] == jnp.arange(N_EXPERTS, dtype=jnp.int32)[None, :]).astype(jnp.int32)
    csum = jnp.cumsum(onehot, axis=0)
    counts = csum[-1]
    rank = jnp.sum((csum - onehot) * onehot, axis=1)
    padded = (counts + tm - 1) // tm * tm
    pad_end = jnp.cumsum(padded)
    pad_start = pad_end - padded
    slot = pad_start[e_flat] + rank
    total = n_assign + N_EXPERTS * tm
    n_blk = total // tm
    slot_tok = jnp.zeros((total,), jnp.int32).at[slot].set(jnp.arange(n_assign, dtype=jnp.int32) // 2)
    n_active = pad_end[-1] // tm
    blk = jnp.arange(n_blk, dtype=jnp.int32)
    blk_exp = jnp.minimum(jnp.searchsorted(pad_end, blk * tm, side="right"), N_EXPERTS - 1).astype(jnp.int32)
    blk_exp = jnp.where(blk < n_active, blk_exp, blk_exp[jnp.maximum(n_active - 1, 0)])
    return slot.reshape(-1, 2), slot_tok.reshape(n_blk, tm), blk_exp, n_active.astype(jnp.int32).reshape(1)


def _moe_body(bexp_ref, nact_ref, tok_hbm, h_hbm, w1_ref, w3_ref, w2_ref, o_ref,
              tok_s, x_s, xb_s, acc_s, sem_idx, sem_row, *, tm):
    i = pl.program_id(0)
    k = pl.program_id(1)
    nk = pl.num_programs(1)
    active = i < nact_ref[0]

    def row_copy(r, tok):
        return pltpu.make_async_copy(h_hbm.at[pl.ds(tok, 1)], x_s.at[pl.ds(r, 1)], sem_row)

    @pl.when(active & (k == 0))
    def _():
        idx_copy = pltpu.make_async_copy(tok_hbm.at[i], tok_s, sem_idx)
        idx_copy.start()
        idx_copy.wait()

        def issue(r, c):
            row_copy(r, tok_s[r]).start()
            return c

        lax.fori_loop(0, tm, issue, 0)

        def drain(r, c):
            row_copy(r, 0).wait()
            return c

        lax.fori_loop(0, tm, drain, 0)
        xb_s[...] = x_s[...].astype(BF16)

    @pl.when(active)
    def _():
        xb = xb_s[...]
        a = jnp.dot(xb, w1_ref[0], preferred_element_type=F32)
        g = jnp.dot(xb, w3_ref[0], preferred_element_type=F32)
        part = jnp.dot((a * jax.nn.sigmoid(a) * g).astype(BF16), w2_ref[0], preferred_element_type=F32)

        @pl.when(k == 0)
        def _():
            acc_s[...] = part

        @pl.when(k > 0)
        def _():
            acc_s[...] += part

    @pl.when(k == nk - 1)
    def _():
        o_ref[...] = jnp.where(active, acc_s[...], 0.0)


def _moe_experts(h, slot_tok, blk_exp, n_active, w1, w3, w2, tm, tf):
    t, d = h.shape
    n_blk = slot_tok.shape[0]
    f = w1.shape[2]
    nk = f // tf

    def wmap(i, k, bexp, nact):
        return bexp[i], jnp.where(i < nact[0], k, nk - 1)

    grid_spec = pltpu.PrefetchScalarGridSpec(
        num_scalar_prefetch=2,
        grid=(n_blk, nk),
        in_specs=[pl.BlockSpec(memory_space=pl.ANY),
                  pl.BlockSpec(memory_space=pl.ANY),
                  pl.BlockSpec((1, d, tf), lambda i, k, bexp, nact: (wmap(i, k, bexp, nact)[0], 0, wmap(i, k, bexp, nact)[1])),
                  pl.BlockSpec((1, d, tf), lambda i, k, bexp, nact: (wmap(i, k, bexp, nact)[0], 0, wmap(i, k, bexp, nact)[1])),
                  pl.BlockSpec((1, tf, d), lambda i, k, bexp, nact: (wmap(i, k, bexp, nact)[0], wmap(i, k, bexp, nact)[1], 0))],
        out_specs=pl.BlockSpec((tm, d), lambda i, k, bexp, nact: (i, 0)),
        scratch_shapes=[pltpu.SMEM((tm,), jnp.int32),
                        pltpu.VMEM((tm, d), F32),
                        pltpu.VMEM((tm, d), BF16),
                        pltpu.VMEM((tm, d), F32),
                        pltpu.SemaphoreType.DMA,
                        pltpu.SemaphoreType.DMA])
    return pl.pallas_call(
        functools.partial(_moe_body, tm=tm),
        grid_spec=grid_spec,
        out_shape=jax.ShapeDtypeStruct((n_blk * tm, d), F32),
        compiler_params=_params("arbitrary", "arbitrary"),
        name="moe_experts",
    )(blk_exp, n_active, slot_tok, h, w1, w3, w2)


def _combine_body(slot_hbm, y_hbm, info_ref, x_ref, mod_ref, g_ref, o_ref,
                  slot_s, y0_s, y1_s, sem_idx, sem_row, *, tt):
    i = pl.program_id(0)
    idx_copy = pltpu.make_async_copy(slot_hbm.at[i], slot_s, sem_idx)
    idx_copy.start()
    idx_copy.wait()

    def row_copy(dst, r, slot):
        return pltpu.make_async_copy(y_hbm.at[pl.ds(slot, 1)], dst.at[pl.ds(r, 1)], sem_row)

    def issue(r, c):
        row_copy(y0_s, r, slot_s[2 * r]).start()
        row_copy(y1_s, r, slot_s[2 * r + 1]).start()
        return c

    lax.fori_loop(0, tt, issue, 0)

    def drain(r, c):
        row_copy(y0_s, r, 0).wait()
        row_copy(y1_s, r, 0).wait()
        return c

    lax.fori_loop(0, tt, drain, 0)
    info = info_ref[...]
    f = info[:, 2:3] * y0_s[...] + info[:, 3:4] * y1_s[...]
    xn = x_ref[...] + mod_ref[0, M_G2:M_G2 + 1] * f
    o_ref[...] = _rms(xn, g_ref[...])


def _combine_final(slots, y, info, x, mod, mod_row, final_g, tt):
    t, d = x.shape
    return pl.pallas_call(
        functools.partial(_combine_body, tt=tt),
        grid=(t // tt,),
        in_specs=[pl.BlockSpec(memory_space=pl.ANY),
                  pl.BlockSpec(memory_space=pl.ANY),
                  pl.BlockSpec((tt, LANES), lambda i: (i, 0)),
                  pl.BlockSpec((tt, d), lambda i: (i, 0)),
                  pl.BlockSpec((1, 6, d), lambda i: (mod_row(i), 0, 0)),
                  pl.BlockSpec((1, d), lambda i: (0, 0))],
        out_specs=pl.BlockSpec((tt, d), lambda i: (i, 0)),
        out_shape=jax.ShapeDtypeStruct((t, d), F32),
        scratch_shapes=[pltpu.SMEM((2 * tt,), jnp.int32),
                        pltpu.VMEM((tt, d), F32),
                        pltpu.VMEM((tt, d), F32),
                        pltpu.SemaphoreType.DMA,
                        pltpu.SemaphoreType.DMA],
        compiler_params=_params("arbitrary"),
        name="moe_combine_final",
    )(slots.reshape(t // tt, 2 * tt), y, info, x, mod, final_g.reshape(1, d))


def kernel(x, c, ctx, c_ctx, w_mod, b_mod, norm1_g, norm2_g, w_in, conv_a_w, conv_b_w, conv_b_b, conv_ln_g, conv_ln_b, sgu_ln_g, sgu_ln_b, sgu_w, sgu_b, rpb, group_g, w_out, ffn_w1, ffn_w3, ffn_w2, router_w, router_b, moe_w1, moe_w3, moe_w2, final_g):
    bsz, seq, d = x.shape
    n_ctx = ctx.shape[1]
    depth = w_mod.shape[0]
    rows = seq // GRID_W
    assert d == D_MODEL and seq % (Q_BLOCK_ROWS * GRID_W) == 0 and n_ctx % CHUNK == 0 and bsz + 1 <= 8
    assert depth == 2, "layer 0 dense with context, layer 1 (last) MoE without context"

    tm = min(512, seq)
    lat_row = lambda bi: bi
    ctx_row = lambda bi: bsz

    c8 = jnp.zeros((8, d), F32).at[:bsz].set(c).at[bsz].set(c_ctx)
    mod_all = _modulation(c8, w_mod, b_mod).reshape(depth, 8, 6, d)

    xl, xc = x, ctx
    out = None
    for l in range(depth):
        last = l == depth - 1
        mod = mod_all[l]
        w_in_b = w_in[l].astype(BF16)
        w_out_b = w_out[l].astype(BF16)
        sw_b = sgu_w[l].astype(BF16)
        sbias = jnp.repeat(sgu_b[l].T, HEAD_W, axis=1)
        gg = group_g[l]
        conv_args = (conv_a_w[l], conv_b_w[l], conv_b_b[l], conv_ln_g[l], conv_ln_b[l],
                     sgu_ln_g[l], sgu_ln_b[l], sw_b, sbias, gg[:3 * GROUP_W])
        gg_d = gg[3 * GROUP_W:]

        p, q, kv = _norm_proj(xl, norm1_g[l], mod, lat_row, w_in_b, tm)
        pc, qc, kvc = _norm_proj(xc, norm1_g[l], mod, ctx_row, w_in_b, n_ctx)
        abc = _conv_mixer(p, *conv_args, ts=tm)
        dn = _nbr_attention(q, kv, kvc, _window_bias(rpb[l], rows), gg_d)

        if not last:
            xl, h = _merge(abc, dn, w_out_b, xl, mod, lat_row, norm2_g[l], tm, BF16)
            abc_c = _conv_mixer(pc, *conv_args, ts=n_ctx)
            dn_c = _ctx_attention(qc, kvc, gg_d)
            xc, hc = _merge(abc_c, dn_c, w_out_b, xc, mod, ctx_row, norm2_g[l], n_ctx, BF16)
            w1, w3, w2 = (w[l // 2].astype(BF16) for w in (ffn_w1, ffn_w3, ffn_w2))
            tf = 256
            xl = _dense_ffn(h.reshape(bsz * seq, d), w1, w3, w2, xl.reshape(bsz * seq, d), mod,
                            lambda i: i * tm // seq, tm, tf).reshape(bsz, seq, d)
            xc = _dense_ffn(hc.reshape(bsz * n_ctx, d), w1, w3, w2, xc.reshape(bsz * n_ctx, d), mod,
                            lambda i: bsz, n_ctx, tf).reshape(bsz, n_ctx, d)
        else:
            rw = jnp.zeros((d, LANES), F32).at[:, :N_EXPERTS].set(router_w[l // 2]).astype(BF16)
            rb = jnp.full((1, LANES), NEG_INF, F32).at[0, :N_EXPERTS].set(router_b[l // 2])
            xl, h, info = _merge(abc, dn, w_out_b, xl, mod, lat_row, norm2_g[l], tm, F32, router=(rw, rb))
            t = bsz * seq
            info = info.reshape(t, LANES)
            moe_tm, moe_tf = 512, 512
            slots, slot_tok, blk_exp, n_active = _route_plan(info[:, :2].astype(jnp.int32), moe_tm)
            w1, w3, w2 = (w[l // 2].astype(BF16) for w in (moe_w1, moe_w3, moe_w2))
            y = _moe_experts(h.reshape(t, d), slot_tok, blk_exp, n_active, w1, w3, w2, moe_tm, moe_tf)
            tt = 256
            out = _combine_final(slots, y, info, xl.reshape(t, d), mod, lambda i: i * tt // seq,
                                 final_g, tt).reshape(bsz, seq, d)
    return out
```

```python
import functools

import numpy as np
import jax
import jax.numpy as jnp
from jax import lax
from jax.experimental import pallas as pl
from jax.experimental.pallas import tpu as pltpu

F32 = jnp.float32
BF16 = jnp.bfloat16

D_MODEL = 1024
GRID_W = 64
EPS = 1e-6
NEG_INF = -1e30
GROUP_W = 256
N_HEADS = 4
HEAD_W = GROUP_W // N_HEADS
CONV_A = 3
CONV_B = 31
CHUNK = 128
WIN_R = 8
WIN_C = 16
OFF_B = 3 * GROUP_W
OFF_C = OFF_B + 2 * GROUP_W
OFF_D = OFF_C + 2 * GROUP_W
OFF_KV = OFF_D + GROUP_W
IN_COLS = OFF_KV + 2 * GROUP_W
N_EXPERTS = 8
LANES = 128
HALO = 16
Q_BLOCK_ROWS = 8
KEY_ROWS_BEFORE = 4
KEY_ROWS_AFTER = 4
VMEM_LIMIT = 56 * 1024 * 1024

M_SH1, M_SC1, M_G1, M_SH2, M_SC2, M_G2 = range(6)


def _params(*dims):
    return pltpu.CompilerParams(dimension_semantics=dims, vmem_limit_bytes=VMEM_LIMIT)


def _rms(x, g):
    return x * lax.rsqrt(jnp.mean(x * x, axis=-1, keepdims=True) + EPS) * g


def _layer_norm(x, g, b):
    mu = jnp.mean(x, axis=-1, keepdims=True)
    xc = x - mu
    var = jnp.mean(xc * xc, axis=-1, keepdims=True)
    return xc * lax.rsqrt(var + EPS) * g + b


def _mod_body(c_ref, w_ref, b_ref, o_ref):
    s = c_ref[...]
    s = s * jax.nn.sigmoid(s)
    o_ref[0] = jnp.dot(s, w_ref[0], preferred_element_type=F32,
                       precision=lax.Precision.HIGHEST) + b_ref[0]


def _modulation(c8, w_mod, b_mod):
    n_layers, d, six_d = w_mod.shape
    return pl.pallas_call(
        _mod_body,
        grid=(n_layers, six_d // d),
        in_specs=[pl.BlockSpec((8, d), lambda l, j: (0, 0)),
                  pl.BlockSpec((1, d, d), lambda l, j: (l, 0, j)),
                  pl.BlockSpec((1, 1, d), lambda l, j: (l, 0, j))],
        out_specs=pl.BlockSpec((1, 8, d), lambda l, j: (l, 0, j)),
        out_shape=jax.ShapeDtypeStruct((n_layers, 8, six_d), F32),
        compiler_params=_params("arbitrary", "arbitrary"),
        name="modulation",
    )(c8, w_mod, b_mod.reshape(n_layers, 1, six_d))


def _norm_proj_body(x_ref, g_ref, mod_ref, w_ref, p_ref, q_ref, kv_ref):
    m = mod_ref[0]
    h = _rms(x_ref[0], g_ref[...]) * (1.0 + m[M_SC1:M_SC1 + 1]) + m[M_SH1:M_SH1 + 1]
    hb = h.astype(BF16)
    p_ref[0] = jnp.dot(hb, w_ref[:, :OFF_D], preferred_element_type=F32)
    q = jnp.dot(hb, w_ref[:, OFF_D:OFF_KV], preferred_element_type=F32)
    q_ref[0] = (q * (HEAD_W ** -0.5)).astype(BF16)
    kv_ref[0] = jnp.dot(hb, w_ref[:, OFF_KV:], preferred_element_type=F32).astype(BF16)


def _norm_proj(x, g, mod, mod_row, w_in_bf16, tm):
    b, s, d = x.shape
    return pl.pallas_call(
        _norm_proj_body,
        grid=(b, s // tm),
        in_specs=[pl.BlockSpec((1, tm, d), lambda bi, i: (bi, i, 0)),
                  pl.BlockSpec((1, d), lambda bi, i: (0, 0)),
                  pl.BlockSpec((1, 6, d), lambda bi, i: (mod_row(bi), 0, 0)),
                  pl.BlockSpec((d, IN_COLS), lambda bi, i: (0, 0))],
        out_specs=[pl.BlockSpec((1, tm, OFF_D), lambda bi, i: (bi, i, 0)),
                   pl.BlockSpec((1, tm, GROUP_W), lambda bi, i: (bi, i, 0)),
                   pl.BlockSpec((1, tm, 2 * GROUP_W), lambda bi, i: (bi, i, 0))],
        out_shape=[jax.ShapeDtypeStruct((b, s, OFF_D), F32),
                   jax.ShapeDtypeStruct((b, s, GROUP_W), BF16),
                   jax.ShapeDtypeStruct((b, s, 2 * GROUP_W), BF16)],
        compiler_params=_params("parallel", "parallel"),
        name="norm_proj",
    )(x, g.reshape(1, d), mod, w_in_bf16)


def _conv_inputs(blk):
    za = blk[:, GROUP_W:2 * GROUP_W] * blk[:, 2 * GROUP_W:3 * GROUP_W]
    zb = blk[:, OFF_B:OFF_B + GROUP_W] * jax.nn.sigmoid(blk[:, OFF_B + GROUP_W:OFF_C])
    return za, zb


def _mixer_body(prev_ref, cur_ref, next_ref, wa_ref, wb_ref, bb_ref, blg_ref, blb_ref,
                slg_ref, slb_ref, sw_ref, sbias_ref, gg_ref, o_ref, za_s, zb_s, *, ts, rc):
    i = pl.program_id(1)
    n = pl.num_programs(1)
    has_prev = (i > 0).astype(F32)
    has_next = (i < n - 1).astype(F32)

    pa, pb = _conv_inputs(prev_ref[0])
    za_s[0:HALO] = pa * has_prev
    zb_s[0:HALO] = pb * has_prev
    ca, cb = _conv_inputs(cur_ref[0])
    za_s[HALO:HALO + ts] = ca
    zb_s[HALO:HALO + ts] = cb
    na, nb = _conv_inputs(next_ref[0])
    za_s[HALO + ts:] = na * has_next
    zb_s[HALO + ts:] = nb * has_next

    wa = wa_ref[...]
    wb = wb_ref[...]
    lane_head = lax.broadcasted_iota(jnp.int32, (1, GROUP_W), 1) // HEAD_W
    for r0 in range(0, ts, rc):
        acc = wa[0:1] * za_s[HALO + r0 - 1:HALO + r0 - 1 + rc]
        for j in range(1, CONV_A):
            acc = acc + wa[j:j + 1] * za_s[HALO + r0 - 1 + j:HALO + r0 - 1 + j + rc]
        ya = cur_ref[0, r0:r0 + rc, 0:GROUP_W] * acc
        o_ref[0, r0:r0 + rc, 0:GROUP_W] = _rms(ya, gg_ref[:, 0:GROUP_W]).astype(o_ref.dtype)

        base = HALO + r0 - CONV_B // 2
        acc = wb[0:1] * zb_s[base:base + rc]
        for j in range(1, CONV_B):
            acc = acc + wb[j:j + 1] * zb_s[base + j:base + j + rc]
        yb = _layer_norm(acc + bb_ref[...], blg_ref[...], blb_ref[...])
        yb = yb * jax.nn.sigmoid(yb)
        o_ref[0, r0:r0 + rc, GROUP_W:2 * GROUP_W] = _rms(yb, gg_ref[:, GROUP_W:2 * GROUP_W]).astype(o_ref.dtype)

    for r0 in range(0, ts, CHUNK):
        z = jax.nn.gelu(cur_ref[0, r0:r0 + CHUNK, OFF_C:OFF_D])
        u = z[:, :GROUP_W]
        v = _layer_norm(z[:, GROUP_W:], slg_ref[...], slb_ref[...]).astype(BF16)
        mixed = sbias_ref[...]
        for h in range(N_HEADS):
            vh = v * (lane_head == h).astype(BF16)
            mixed = mixed + jnp.dot(sw_ref[h], vh, preferred_element_type=F32)
        yc = u * mixed
        o_ref[0, r0:r0 + CHUNK, 2 * GROUP_W:3 * GROUP_W] = _rms(yc, gg_ref[:, 2 * GROUP_W:3 * GROUP_W]).astype(o_ref.dtype)


def _conv_mixer(p, wa, wb, bb, blg, blb, slg, slb, sw_bf16, sbias, gg, ts):
    b, s, _ = p.shape
    rc = min(64, ts)
    hb = ts // HALO
    n_halo = s // HALO
    row = lambda a: a.reshape(1, -1)
    const = lambda *shape: pl.BlockSpec(shape, lambda bi, i: (0,) * len(shape))
    return pl.pallas_call(
        functools.partial(_mixer_body, ts=ts, rc=rc),
        grid=(b, s // ts),
        in_specs=[pl.BlockSpec((1, HALO, OFF_D), lambda bi, i: (bi, jnp.maximum(i * hb - 1, 0), 0)),
                  pl.BlockSpec((1, ts, OFF_D), lambda bi, i: (bi, i, 0)),
                  pl.BlockSpec((1, HALO, OFF_D), lambda bi, i: (bi, jnp.minimum((i + 1) * hb, n_halo - 1), 0)),
                  const(CONV_A, GROUP_W), const(CONV_B, GROUP_W), const(1, GROUP_W),
                  const(1, GROUP_W), const(1, GROUP_W), const(1, GROUP_W), const(1, GROUP_W),
                  const(N_HEADS, CHUNK, CHUNK), const(CHUNK, GROUP_W), const(1, 3 * GROUP_W)],
        out_specs=pl.BlockSpec((1, ts, 3 * GROUP_W), lambda bi, i: (bi, i, 0)),
        out_shape=jax.ShapeDtypeStruct((b, s, 3 * GROUP_W), BF16),
        scratch_shapes=[pltpu.VMEM((ts + 2 * HALO, GROUP_W), F32),
                        pltpu.VMEM((ts + 2 * HALO, GROUP_W), F32)],
        compiler_params=_params("parallel", "parallel"),
        name="conv_mixer",
    )(p, p, p, wa, wb, row(bb), row(blg), row(blb), row(slg), row(slb), sw_bf16, sbias, row(gg))


def _head_masks():
    lane_head = lax.broadcasted_iota(jnp.int32, (1, GROUP_W), 1) // HEAD_W
    return [(lane_head == h).astype(BF16) for h in range(N_HEADS)]


_NT = (((1,), (1,)), ((), ()))


def _attend(q, masks, keys, values, biases, acc):
    for h in range(N_HEADS):
        qh = q * masks[h]
        scores = []
        for k, bias in zip(keys, biases):
            s = lax.dot_general(qh, k, _NT, preferred_element_type=F32)
            scores.append(s if bias is None else s + bias(h))
        m = scores[0].max(axis=-1, keepdims=True)
        for s in scores[1:]:
            m = jnp.maximum(m, s.max(axis=-1, keepdims=True))
        probs = [jnp.exp(s - m) for s in scores]
        denom = probs[0].sum(axis=-1, keepdims=True)
        for p in probs[1:]:
            denom = denom + p.sum(axis=-1, keepdims=True)
        o = jnp.dot(probs[0].astype(BF16), values[0](h), preferred_element_type=F32)
        for p, v in zip(probs[1:], values[1:]):
            o = o + jnp.dot(p.astype(BF16), v(h), preferred_element_type=F32)
        acc = acc + o / denom
    return acc


def _nbr_attn_body(q_ref, kp_ref, kc_ref, kn_ref, kx_ref, bias_ref, gg_ref, o_ref,
                   k_s, v_s, kx_s, vx_s, *, tq, n_sub):
    masks = _head_masks()
    n_half = kp_ref.shape[1]
    n_cur = kc_ref.shape[1]
    pieces = ((kp_ref, 0, n_half), (kc_ref, n_half, n_cur), (kn_ref, n_half + n_cur, n_half))
    for ref, off, n in pieces:
        k_s[off:off + n] = ref[0, :, 0:GROUP_W]
        for h in range(N_HEADS):
            v_s[h, off:off + n] = ref[0, :, GROUP_W:] * masks[h]
    kx_s[...] = kx_ref[0, :, 0:GROUP_W]
    for h in range(N_HEADS):
        vx_s[h] = kx_ref[0, :, GROUP_W:] * masks[h]

    def sub(sb, carry):
        r0 = pl.multiple_of(sb * tq, tq)
        q = q_ref[0, pl.ds(r0, tq), :]
        acc = _attend(
            q, masks,
            keys=[k_s[...], kx_s[...]],
            values=[lambda h: v_s[h], lambda h: vx_s[h]],
            biases=[lambda h: bias_ref[0, h, pl.ds(r0, tq), :], None],
            acc=jnp.zeros((tq, GROUP_W), F32))
        o_ref[0, pl.ds(r0, tq), :] = _rms(acc, gg_ref[...]).astype(o_ref.dtype)
        return carry

    lax.fori_loop(0, n_sub, sub, 0)


def _window_bias(rpb_l, rows):
    nb = rows // Q_BLOCK_ROWS
    n_key_rows = KEY_ROWS_BEFORE + Q_BLOCK_ROWS + KEY_ROWS_AFTER
    i = np.arange(Q_BLOCK_ROWS)[:, None, None, None]
    c = np.arange(GRID_W)[None, :, None, None]
    kr = np.arange(-KEY_ROWS_BEFORE, Q_BLOCK_ROWS + KEY_ROWS_AFTER)[None, None, :, None]
    kc = np.arange(GRID_W)[None, None, None, :]
    shape = (Q_BLOCK_ROWS, GRID_W, n_key_rows, GRID_W)
    edge = GRID_W - WIN_C
    cols = jnp.pad(rpb_l, ((0, 0), (0, 0), (edge, edge)), mode="edge")
    by_col = jnp.stack([cols[:, :, GRID_W - 1 - cc:2 * GRID_W - 1 - cc] for cc in range(GRID_W)],
                       axis=2)
    lo = KEY_ROWS_BEFORE
    hi = Q_BLOCK_ROWS + KEY_ROWS_AFTER - WIN_R
    by_col = jnp.pad(by_col, ((0, 0), (lo, hi), (0, 0), (0, 0)))
    per_row = [by_col[:, WIN_R - 1 - ii:WIN_R - 1 - ii + n_key_rows].transpose(0, 2, 1, 3)
               for ii in range(Q_BLOCK_ROWS)]
    values = jnp.stack(per_row, axis=1)
    tables = []
    for jv in (0, min(1, nb - 1), nb - 1):
        r = Q_BLOCK_ROWS * jv + i
        ks = np.clip(r - WIN_R // 2, 0, rows - WIN_R)
        krow = Q_BLOCK_ROWS * jv + kr
        c_start = np.clip(c - WIN_C // 2, 0, GRID_W - WIN_C)
        valid = ((krow >= ks) & (krow < ks + WIN_R) & (krow >= 0) & (krow < rows)
                 & (kc >= c_start) & (kc < c_start + WIN_C))
        bias = jnp.where(np.broadcast_to(valid, shape)[None], values, NEG_INF)
        tables.append(bias.reshape(N_HEADS, Q_BLOCK_ROWS * GRID_W, n_key_rows * GRID_W))
    return jnp.stack(tables)


def _nbr_attention(q, kv, kv_ctx, bias, gg):
    b, s, _ = q.shape
    n_ctx = kv_ctx.shape[1]
    tb = Q_BLOCK_ROWS * GRID_W
    half = KEY_ROWS_BEFORE * GRID_W
    nb = s // tb
    assert nb >= 2 and KEY_ROWS_BEFORE == KEY_ROWS_AFTER and tb == 2 * half
    n_loc = tb + 2 * half
    tq = 128
    variant = lambda j: jnp.where(j == 0, 0, jnp.where(j == nb - 1, 2, 1))
    return pl.pallas_call(
        functools.partial(_nbr_attn_body, tq=tq, n_sub=tb // tq),
        grid=(b, nb),
        in_specs=[pl.BlockSpec((1, tb, GROUP_W), lambda bi, j: (bi, j, 0)),
                  pl.BlockSpec((1, half, 2 * GROUP_W), lambda bi, j: (bi, jnp.maximum(2 * j - 1, 0), 0)),
                  pl.BlockSpec((1, tb, 2 * GROUP_W), lambda bi, j: (bi, j, 0)),
                  pl.BlockSpec((1, half, 2 * GROUP_W), lambda bi, j: (bi, jnp.minimum(2 * j + 2, 2 * nb - 1), 0)),
                  pl.BlockSpec((1, n_ctx, 2 * GROUP_W), lambda bi, j: (bi, 0, 0)),
                  pl.BlockSpec((1, N_HEADS, tb, n_loc), lambda bi, j: (variant(j), 0, 0, 0)),
                  pl.BlockSpec((1, GROUP_W), lambda bi, j: (0, 0))],
        out_specs=pl.BlockSpec((1, tb, GROUP_W), lambda bi, j: (bi, j, 0)),
        out_shape=jax.ShapeDtypeStruct((b, s, GROUP_W), BF16),
        scratch_shapes=[pltpu.VMEM((n_loc, GROUP_W), BF16),
                        pltpu.VMEM((N_HEADS, n_loc, GROUP_W), BF16),
                        pltpu.VMEM((n_ctx, GROUP_W), BF16),
                        pltpu.VMEM((N_HEADS, n_ctx, GROUP_W), BF16)],
        compiler_params=_params("parallel", "arbitrary"),
        name="nbr_attention",
    )(q, kv, kv, kv, kv_ctx, bias, gg.reshape(1, GROUP_W))


def _ctx_attn_body(q_ref, kx_ref, gg_ref, o_ref, vx_s):
    masks = _head_masks()
    for h in range(N_HEADS):
        vx_s[h] = kx_ref[0, :, GROUP_W:] * masks[h]
    tq = q_ref.shape[1]
    acc = _attend(q_ref[0], masks, keys=[kx_ref[0, :, 0:GROUP_W]], values=[lambda h: vx_s[h]],
                  biases=[None], acc=jnp.zeros((tq, GROUP_W), F32))
    o_ref[0] = _rms(acc, gg_ref[...]).astype(o_ref.dtype)


def _ctx_attention(q, kv_ctx, gg):
    b, n_ctx, _ = q.shape
    return pl.pallas_call(
        _ctx_attn_body,
        grid=(b,),
        in_specs=[pl.BlockSpec((1, n_ctx, GROUP_W), lambda bi: (bi, 0, 0)),
                  pl.BlockSpec((1, n_ctx, 2 * GROUP_W), lambda bi: (bi, 0, 0)),
                  pl.BlockSpec((1, GROUP_W), lambda bi: (0, 0))],
        out_specs=pl.BlockSpec((1, n_ctx, GROUP_W), lambda bi: (bi, 0, 0)),
        out_shape=jax.ShapeDtypeStruct((b, n_ctx, GROUP_W), BF16),
        scratch_shapes=[pltpu.VMEM((N_HEADS, n_ctx, GROUP_W), BF16)],
        compiler_params=_params("parallel"),
        name="ctx_attention",
    )(q, kv_ctx, gg.reshape(1, GROUP_W))


def _merge_body(abc_ref, dn_ref, w_ref, x_ref, mod_ref, g_ref, *rest, with_router):
    if with_router:
        rw_ref, rb_ref, xo_ref, h_ref, info_ref = rest
    else:
        xo_ref, h_ref = rest
    y = (jnp.dot(abc_ref[0], w_ref[0:3 * GROUP_W], preferred_element_type=F32)
         + jnp.dot(dn_ref[0], w_ref[3 * GROUP_W:], preferred_element_type=F32))
    m = mod_ref[0]
    xn = x_ref[0] + m[M_G1:M_G1 + 1] * y
    xo_ref[0] = xn
    h = _rms(xn, g_ref[...]) * (1.0 + m[M_SC2:M_SC2 + 1]) + m[M_SH2:M_SH2 + 1]
    h_ref[0] = h.astype(h_ref.dtype)
    if with_router:
        logits = jnp.dot(h.astype(BF16), rw_ref[...], preferred_element_type=F32) + rb_ref[...]
        lane = lax.broadcasted_iota(jnp.int32, logits.shape, 1).astype(F32)
        m1 = logits.max(axis=-1, keepdims=True)
        i1 = jnp.where(logits == m1, lane, float(LANES)).min(axis=-1, keepdims=True)
        rest_logits = jnp.where(lane == i1, NEG_INF, logits)
        m2 = rest_logits.max(axis=-1, keepdims=True)
        i2 = jnp.where(rest_logits == m2, lane, float(LANES)).min(axis=-1, keepdims=True)
        e2 = jnp.exp(m2 - m1)
        den = 1.0 + e2
        info = jnp.where(lane == 0, i1, jnp.where(lane == 1, i2,
               jnp.where(lane == 2, 1.0 / den, jnp.where(lane == 3, e2 / den, 0.0))))
        info_ref[0] = info


def _merge(abc, dn, w_out_bf16, x, mod, mod_row, g2, tm, h_dtype, router=None):
    b, s, d = x.shape
    in_specs = [pl.BlockSpec((1, tm, 3 * GROUP_W), lambda bi, i: (bi, i, 0)),
                pl.BlockSpec((1, tm, GROUP_W), lambda bi, i: (bi, i, 0)),
                pl.BlockSpec((4 * GROUP_W, d), lambda bi, i: (0, 0)),
                pl.BlockSpec((1, tm, d), lambda bi, i: (bi, i, 0)),
                pl.BlockSpec((1, 6, d), lambda bi, i: (mod_row(bi), 0, 0)),
                pl.BlockSpec((1, d), lambda bi, i: (0, 0))]
    out_specs = [pl.BlockSpec((1, tm, d), lambda bi, i: (bi, i, 0)),
                 pl.BlockSpec((1, tm, d), lambda bi, i: (bi, i, 0))]
    out_shape = [jax.ShapeDtypeStruct((b, s, d), F32), jax.ShapeDtypeStruct((b, s, d), h_dtype)]
    args = [abc, dn, w_out_bf16, x, mod, g2.reshape(1, d)]
    if router is not None:
        in_specs += [pl.BlockSpec((d, LANES), lambda bi, i: (0, 0)),
                     pl.BlockSpec((1, LANES), lambda bi, i: (0, 0))]
        out_specs.append(pl.BlockSpec((1, tm, LANES), lambda bi, i: (bi, i, 0)))
        out_shape.append(jax.ShapeDtypeStruct((b, s, LANES), F32))
        args += list(router)
    return pl.pallas_call(
        functools.partial(_merge_body, with_router=router is not None),
        grid=(b, s // tm),
        in_specs=in_specs, out_specs=out_specs, out_shape=out_shape,
        compiler_params=_params("parallel", "parallel"),
        name="merge_router" if router is not None else "merge",
    )(*args)


def _ffn_body(h_ref, w1_ref, w3_ref, w2_ref, x_ref, mod_ref, o_ref, acc_ref):
    k = pl.program_id(1)
    h = h_ref[...]
    a = jnp.dot(h, w1_ref[...], preferred_element_type=F32)
    g = jnp.dot(h, w3_ref[...], preferred_element_type=F32)
    part = jnp.dot((a * jax.nn.sigmoid(a) * g).astype(BF16), w2_ref[...], preferred_element_type=F32)

    @pl.when(k == 0)
    def _():
        acc_ref[...] = part

    @pl.when(k > 0)
    def _():
        acc_ref[...] += part

    @pl.when(k == pl.num_programs(1) - 1)
    def _():
        o_ref[...] = x_ref[...] + mod_ref[0, M_G2:M_G2 + 1] * acc_ref[...]


def _dense_ffn(h, w1, w3, w2, x, mod, mod_row, tm, tf):
    t, d = h.shape
    f = w1.shape[1]
    return pl.pallas_call(
        _ffn_body,
        grid=(t // tm, f // tf),
        in_specs=[pl.BlockSpec((tm, d), lambda i, k: (i, 0)),
                  pl.BlockSpec((d, tf), lambda i, k: (0, k)),
                  pl.BlockSpec((d, tf), lambda i, k: (0, k)),
                  pl.BlockSpec((tf, d), lambda i, k: (k, 0)),
                  pl.BlockSpec((tm, d), lambda i, k: (i, 0)),
                  pl.BlockSpec((1, 6, d), lambda i, k: (mod_row(i), 0, 0))],
        out_specs=pl.BlockSpec((tm, d), lambda i, k: (i, 0)),
        out_shape=jax.ShapeDtypeStruct((t, d), F32),
        scratch_shapes=[pltpu.VMEM((tm, d), F32)],
        compiler_params=_params("parallel", "arbitrary"),
        name="dense_ffn",
    )(h, w1, w3, w2, x, mod)


WINDOW_ALIGN = 16


def _route_plan(expert_idx, tm, tt):
    n_assign = expert_idx.size
    e_flat = expert_idx.reshape(-1)
    onehot = (e_flat[:, None] == jnp.arange(N_EXPERTS, dtype=jnp.int32)[None, :]).astype(jnp.int32)
    csum = jnp.cumsum(onehot, axis=0)
    counts = csum[-1]
    rank = jnp.sum((csum - onehot) * onehot, axis=1)
    padded = (counts + tm - 1) // tm * tm
    pad_end = jnp.cumsum(padded)
    pad_start = pad_end - padded
    slot = pad_start[e_flat] + rank
    total = n_assign + N_EXPERTS * tm
    n_blk = total // tm
    slot_tok = jnp.zeros((total,), jnp.int32).at[slot].set(jnp.arange(n_assign, dtype=jnp.int32) // 2)
    n_active = pad_end[-1] // tm
    blk = jnp.arange(n_blk, dtype=jnp.int32)
    blk_exp = jnp.minimum(jnp.searchsorted(pad_end, blk * tm, side="right"), N_EXPERTS - 1).astype(jnp.int32)
    blk_exp = jnp.where(blk < n_active, blk_exp, blk_exp[jnp.maximum(n_active - 1, 0)])

    wa = tt + WINDOW_ALIGN
    n_tiles = n_assign // (2 * tt)
    before = jnp.concatenate([jnp.zeros((1, N_EXPERTS), jnp.int32), csum[2 * tt - 1::2 * tt][:-1]], axis=0)
    first = pad_start[None, :] + before
    win_base = jnp.minimum(first // WINDOW_ALIGN * WINDOW_ALIGN, total - wa)
    tile_of = jnp.arange(n_assign, dtype=jnp.int32) // (2 * tt)
    win_loc = e_flat * wa + slot - win_base[tile_of, e_flat]
    return (slot_tok.reshape(n_blk, tm), blk_exp, n_active.astype(jnp.int32).reshape(1),
            win_base.reshape(n_tiles * N_EXPERTS).astype(jnp.int32), win_loc.reshape(-1, 2))


def _pipelined_gather(i, n_steps, idx_hbm, idx_s, sem_idx, start_rows, wait_rows):
    def idx_copy(step):
        return pltpu.make_async_copy(idx_hbm.at[step], idx_s.at[step % 2], sem_idx.at[step % 2])

    @pl.when((i == 0) & (n_steps > 0))
    def _():
        idx_copy(0).start()
        idx_copy(0).wait()
        start_rows(0)

        @pl.when(n_steps > 1)
        def _():
            idx_copy(1).start()

    @pl.when(i < n_steps)
    def _():
        wait_rows(i)

    @pl.when(i + 1 < n_steps)
    def _():
        idx_copy(i + 1).wait()
        start_rows(i + 1)

        @pl.when(i + 2 < n_steps)
        def _():
            idx_copy(i + 2).start()


ROW_ISSUE_UNROLL = 8


def _moe_body(bexp_ref, nact_ref, tok_hbm, h_hbm, w1_ref, w3_ref, w2_ref, o_ref,
              tok_s, x_s, xb_s, acc_s, sem_idx, sem_row, *, tm, tf):
    i = pl.program_id(0)
    n_active = nact_ref[0]

    def row_copy(slot, r, tok):
        return pltpu.make_async_copy(h_hbm.at[pl.ds(tok, 1)], x_s.at[slot, pl.ds(r, 1)], sem_row.at[slot])

    def start_rows(step):
        slot = step % 2

        def issue(r, c):
            row_copy(slot, r, tok_s[slot, r]).start()
            return c

        lax.fori_loop(0, tm, issue, 0, unroll=ROW_ISSUE_UNROLL)

    def wait_rows(step):
        slot = step % 2

        def drain(r, c):
            row_copy(slot, r, 0).wait()
            return c

        lax.fori_loop(0, tm, drain, 0, unroll=ROW_ISSUE_UNROLL)
        xb_s[...] = x_s[slot].astype(BF16)

    _pipelined_gather(i, n_active, tok_hbm, tok_s, sem_idx, start_rows, wait_rows)

    acc_s[...] = jnp.zeros(acc_s.shape, acc_s.dtype)

    @pl.when(i < n_active)
    def _():
        def chunk(c, carry):
            c0 = pl.multiple_of(c * tf, tf)
            xb = xb_s[...]
            a = jnp.dot(xb, w1_ref[0, :, pl.ds(c0, tf)], preferred_element_type=F32)
            g = jnp.dot(xb, w3_ref[0, :, pl.ds(c0, tf)], preferred_element_type=F32)
            act = (a * jax.nn.sigmoid(a) * g).astype(BF16)
            acc_s[...] += jnp.dot(act, w2_ref[0, pl.ds(c0, tf), :], preferred_element_type=F32)
            return carry

        lax.fori_loop(0, w1_ref.shape[2] // tf, chunk, 0)

    o_ref[...] = acc_s[...].astype(o_ref.dtype)


def _moe_experts(h, slot_tok, blk_exp, n_active, w1, w3, w2, tm, tf):
    t, d = h.shape
    n_blk = slot_tok.shape[0]
    f = w1.shape[2]
    resident = pl.Buffered(1)
    grid_spec = pltpu.PrefetchScalarGridSpec(
        num_scalar_prefetch=2,
        grid=(n_blk,),
        in_specs=[pl.BlockSpec(memory_space=pl.ANY),
                  pl.BlockSpec(memory_space=pl.ANY),
                  pl.BlockSpec((1, d, f), lambda i, bexp, nact: (bexp[i], 0, 0), pipeline_mode=resident),
                  pl.BlockSpec((1, d, f), lambda i, bexp, nact: (bexp[i], 0, 0), pipeline_mode=resident),
                  pl.BlockSpec((1, f, d), lambda i, bexp, nact: (bexp[i], 0, 0), pipeline_mode=resident)],
        out_specs=pl.BlockSpec((tm, d), lambda i, bexp, nact: (i, 0)),
        scratch_shapes=[pltpu.SMEM((2, tm), jnp.int32),
                        pltpu.VMEM((2, tm, d), F32),
                        pltpu.VMEM((tm, d), BF16),
                        pltpu.VMEM((tm, d), F32),
                        pltpu.SemaphoreType.DMA((2,)),
                        pltpu.SemaphoreType.DMA((2,))])
    return pl.pallas_call(
        functools.partial(_moe_body, tm=tm, tf=tf),
        grid_spec=grid_spec,
        out_shape=jax.ShapeDtypeStruct((n_blk * tm, d), BF16),
        compiler_params=_params("arbitrary"),
        name="moe_experts",
    )(blk_exp, n_active, slot_tok, h, w1, w3, w2)


def _combine_body(base_ref, y_hbm, info_ref, x_ref, mod_ref, g_ref, o_ref, win_s, sem, *, wa):
    i = pl.program_id(0)
    n = pl.num_programs(0)

    def window_copy(step, e):
        slot = step % 2
        base = pl.multiple_of(base_ref[step * N_EXPERTS + e], WINDOW_ALIGN)
        return pltpu.make_async_copy(y_hbm.at[pl.ds(base, wa)], win_s.at[slot, pl.ds(e * wa, wa)], sem.at[slot])

    @pl.when(i == 0)
    def _():
        for e in range(N_EXPERTS):
            window_copy(0, e).start()

    @pl.when(i + 1 < n)
    def _():
        for e in range(N_EXPERTS):
            window_copy(i + 1, e).start()

    for e in range(N_EXPERTS):
        window_copy(i, e).wait()

    info = info_ref[...]
    win = win_s[i % 2]
    row = lax.broadcasted_iota(jnp.int32, (info.shape[0], win.shape[0]), 1).astype(F32)
    pick0 = jnp.where(row == info[:, 0:1], 1.0, 0.0).astype(BF16)
    pick1 = jnp.where(row == info[:, 1:2], 1.0, 0.0).astype(BF16)
    f = (info[:, 2:3] * jnp.dot(pick0, win, preferred_element_type=F32)
         + info[:, 3:4] * jnp.dot(pick1, win, preferred_element_type=F32))
    xn = x_ref[...] + mod_ref[0, M_G2:M_G2 + 1] * f
    o_ref[...] = _rms(xn, g_ref[...])


def _combine_final(win_base, y, info, x, mod, mod_row, final_g, tt):
    t, d = x.shape
    wa = tt + WINDOW_ALIGN
    grid_spec = pltpu.PrefetchScalarGridSpec(
        num_scalar_prefetch=1,
        grid=(t // tt,),
        in_specs=[pl.BlockSpec(memory_space=pl.ANY),
                  pl.BlockSpec((tt, LANES), lambda i, base: (i, 0)),
                  pl.BlockSpec((tt, d), lambda i, base: (i, 0)),
                  pl.BlockSpec((1, 6, d), lambda i, base: (mod_row(i), 0, 0)),
                  pl.BlockSpec((1, d), lambda i, base: (0, 0))],
        out_specs=pl.BlockSpec((tt, d), lambda i, base: (i, 0)),
        scratch_shapes=[pltpu.VMEM((2, N_EXPERTS * wa, d), y.dtype),
                        pltpu.SemaphoreType.DMA((2,))])
    return pl.pallas_call(
        functools.partial(_combine_body, wa=wa),
        grid_spec=grid_spec,
        out_shape=jax.ShapeDtypeStruct((t, d), F32),
        compiler_params=_params("arbitrary"),
        name="moe_combine_final",
    )(win_base, y, info, x, mod, final_g.reshape(1, d))


def kernel(x, c, ctx, c_ctx, w_mod, b_mod, norm1_g, norm2_g, w_in, conv_a_w, conv_b_w, conv_b_b, conv_ln_g, conv_ln_b, sgu_ln_g, sgu_ln_b, sgu_w, sgu_b, rpb, group_g, w_out, ffn_w1, ffn_w3, ffn_w2, router_w, router_b, moe_w1, moe_w3, moe_w2, final_g):
    bsz, seq, d = x.shape
    n_ctx = ctx.shape[1]
    depth = w_mod.shape[0]
    rows = seq // GRID_W
    assert d == D_MODEL and seq % (Q_BLOCK_ROWS * GRID_W) == 0 and n_ctx % CHUNK == 0 and bsz + 1 <= 8
    assert depth == 2, "layer 0 dense with context, layer 1 (last) MoE without context"

    tm = min(512, seq)
    lat_row = lambda bi: bi
    ctx_row = lambda bi: bsz

    c8 = jnp.zeros((8, d), F32).at[:bsz].set(c).at[bsz].set(c_ctx)
    mod_all = _modulation(c8, w_mod, b_mod).reshape(depth, 8, 6, d)

    xl, xc = x, ctx
    out = None
    for l in range(depth):
        last = l == depth - 1
        mod = mod_all[l]
        w_in_b = w_in[l].astype(BF16)
        w_out_b = w_out[l].astype(BF16)
        sw_b = sgu_w[l].astype(BF16)
        sbias = jnp.repeat(sgu_b[l].T, HEAD_W, axis=1)
        gg = group_g[l]
        conv_args = (conv_a_w[l], conv_b_w[l], conv_b_b[l], conv_ln_g[l], conv_ln_b[l],
                     sgu_ln_g[l], sgu_ln_b[l], sw_b, sbias, gg[:3 * GROUP_W])
        gg_d = gg[3 * GROUP_W:]

        p, q, kv = _norm_proj(xl, norm1_g[l], mod, lat_row, w_in_b, tm)
        pc, qc, kvc = _norm_proj(xc, norm1_g[l], mod, ctx_row, w_in_b, n_ctx)
        abc = _conv_mixer(p, *conv_args, ts=tm)
        dn = _nbr_attention(q, kv, kvc, _window_bias(rpb[l], rows), gg_d)

        if not last:
            xl, h = _merge(abc, dn, w_out_b, xl, mod, lat_row, norm2_g[l], tm, BF16)
            abc_c = _conv_mixer(pc, *conv_args, ts=n_ctx)
            dn_c = _ctx_attention(qc, kvc, gg_d)
            xc, hc = _merge(abc_c, dn_c, w_out_b, xc, mod, ctx_row, norm2_g[l], n_ctx, BF16)
            w1, w3, w2 = (w[l // 2].astype(BF16) for w in (ffn_w1, ffn_w3, ffn_w2))
            tf = 256
            xl = _dense_ffn(h.reshape(bsz * seq, d), w1, w3, w2, xl.reshape(bsz * seq, d), mod,
                            lambda i: i * tm // seq, tm, tf).reshape(bsz, seq, d)
            xc = _dense_ffn(hc.reshape(bsz * n_ctx, d), w1, w3, w2, xc.reshape(bsz * n_ctx, d), mod,
                            lambda i: bsz, n_ctx, tf).reshape(bsz, n_ctx, d)
        else:
            rw = jnp.zeros((d, LANES), F32).at[:, :N_EXPERTS].set(router_w[l // 2]).astype(BF16)
            rb = jnp.full((1, LANES), NEG_INF, F32).at[0, :N_EXPERTS].set(router_b[l // 2])
            xl, h, info = _merge(abc, dn, w_out_b, xl, mod, lat_row, norm2_g[l], tm, F32, router=(rw, rb))
            t = bsz * seq
            info = info.reshape(t, LANES)
            moe_tm, moe_tf, tt = 512, 512, 256
            slot_tok, blk_exp, n_active, win_base, win_loc = _route_plan(
                info[:, :2].astype(jnp.int32), moe_tm, tt)
            w1, w3, w2 = (w[l // 2].astype(BF16) for w in (moe_w1, moe_w3, moe_w2))
            y = _moe_experts(h.reshape(t, d), slot_tok, blk_exp, n_active, w1, w3, w2, moe_tm, moe_tf)
            info = jnp.concatenate([win_loc.astype(F32), info[:, 2:]], axis=1)
            out = _combine_final(win_base, y, info, xl.reshape(t, d), mod, lambda i: i * tt // seq,
                                 final_g, tt).reshape(bsz, seq, d)
    return out
```

```python
import functools

import numpy as np
import jax
import jax.numpy as jnp
from jax import lax
from jax.experimental import pallas as pl
from jax.experimental.pallas import tpu as pltpu

F32 = jnp.float32
BF16 = jnp.bfloat16

D_MODEL = 1024
GRID_W = 64
EPS = 1e-6
NEG_INF = -1e30
GROUP_W = 256
N_HEADS = 4
HEAD_W = GROUP_W // N_HEADS
CONV_A = 3
CONV_B = 31
CHUNK = 128
WIN_R = 8
WIN_C = 16
OFF_B = 3 * GROUP_W
OFF_C = OFF_B + 2 * GROUP_W
OFF_D = OFF_C + 2 * GROUP_W
OFF_KV = OFF_D + GROUP_W
IN_COLS = OFF_KV + 2 * GROUP_W
N_EXPERTS = 8
LANES = 128
HALO = 16
Q_BLOCK_ROWS = 8
KEY_ROWS_BEFORE = 4
KEY_ROWS_AFTER = 4
VMEM_LIMIT = 56 * 1024 * 1024

M_SH1, M_SC1, M_G1, M_SH2, M_SC2, M_G2 = range(6)


def _params(*dims):
    return pltpu.CompilerParams(dimension_semantics=dims, vmem_limit_bytes=VMEM_LIMIT)


def _rms(x, g):
    return x * lax.rsqrt(jnp.mean(x * x, axis=-1, keepdims=True) + EPS) * g


def _layer_norm(x, g, b):
    mu = jnp.mean(x, axis=-1, keepdims=True)
    xc = x - mu
    var = jnp.mean(xc * xc, axis=-1, keepdims=True)
    return xc * lax.rsqrt(var + EPS) * g + b


def _mod_body(c_ref, w_ref, b_ref, o_ref):
    s = c_ref[...]
    s = s * jax.nn.sigmoid(s)
    o_ref[0] = jnp.dot(s, w_ref[0], preferred_element_type=F32,
                       precision=lax.Precision.HIGHEST) + b_ref[0]


def _modulation(c8, w_mod, b_mod):
    n_layers, d, six_d = w_mod.shape
    return pl.pallas_call(
        _mod_body,
        grid=(n_layers, six_d // d),
        in_specs=[pl.BlockSpec((8, d), lambda l, j: (0, 0)),
                  pl.BlockSpec((1, d, d), lambda l, j: (l, 0, j)),
                  pl.BlockSpec((1, 1, d), lambda l, j: (l, 0, j))],
        out_specs=pl.BlockSpec((1, 8, d), lambda l, j: (l, 0, j)),
        out_shape=jax.ShapeDtypeStruct((n_layers, 8, six_d), F32),
        compiler_params=_params("arbitrary", "arbitrary"),
        name="modulation",
    )(c8, w_mod, b_mod.reshape(n_layers, 1, six_d))


def _norm_proj_body(x_ref, g_ref, mod_ref, w_ref, p_ref, q_ref, kv_ref):
    m = mod_ref[0]
    h = _rms(x_ref[0], g_ref[...]) * (1.0 + m[M_SC1:M_SC1 + 1]) + m[M_SH1:M_SH1 + 1]
    hb = h.astype(BF16)
    p_ref[0] = jnp.dot(hb, w_ref[:, :OFF_D], preferred_element_type=F32)
    q = jnp.dot(hb, w_ref[:, OFF_D:OFF_KV], preferred_element_type=F32)
    q_ref[0] = (q * (HEAD_W ** -0.5)).astype(BF16)
    kv_ref[0] = jnp.dot(hb, w_ref[:, OFF_KV:], preferred_element_type=F32).astype(BF16)


def _norm_proj(x, g, mod, mod_row, w_in_bf16, tm):
    b, s, d = x.shape
    return pl.pallas_call(
        _norm_proj_body,
        grid=(b, s // tm),
        in_specs=[pl.BlockSpec((1, tm, d), lambda bi, i: (bi, i, 0)),
                  pl.BlockSpec((1, d), lambda bi, i: (0, 0)),
                  pl.BlockSpec((1, 6, d), lambda bi, i: (mod_row(bi), 0, 0)),
                  pl.BlockSpec((d, IN_COLS), lambda bi, i: (0, 0))],
        out_specs=[pl.BlockSpec((1, tm, OFF_D), lambda bi, i: (bi, i, 0)),
                   pl.BlockSpec((1, tm, GROUP_W), lambda bi, i: (bi, i, 0)),
                   pl.BlockSpec((1, tm, 2 * GROUP_W), lambda bi, i: (bi, i, 0))],
        out_shape=[jax.ShapeDtypeStruct((b, s, OFF_D), F32),
                   jax.ShapeDtypeStruct((b, s, GROUP_W), BF16),
                   jax.ShapeDtypeStruct((b, s, 2 * GROUP_W), BF16)],
        compiler_params=_params("parallel", "parallel"),
        name="norm_proj",
    )(x, g.reshape(1, d), mod, w_in_bf16)


def _conv_inputs(blk):
    za = blk[:, GROUP_W:2 * GROUP_W] * blk[:, 2 * GROUP_W:3 * GROUP_W]
    zb = blk[:, OFF_B:OFF_B + GROUP_W] * jax.nn.sigmoid(blk[:, OFF_B + GROUP_W:OFF_C])
    return za, zb


def _mixer_body(prev_ref, cur_ref, next_ref, wa_ref, wb_ref, bb_ref, blg_ref, blb_ref,
                slg_ref, slb_ref, sw_ref, sbias_ref, gg_ref, o_ref, za_s, zb_s, *, ts, rc):
    i = pl.program_id(1)
    n = pl.num_programs(1)
    has_prev = (i > 0).astype(F32)
    has_next = (i < n - 1).astype(F32)

    pa, pb = _conv_inputs(prev_ref[0])
    za_s[0:HALO] = pa * has_prev
    zb_s[0:HALO] = pb * has_prev
    ca, cb = _conv_inputs(cur_ref[0])
    za_s[HALO:HALO + ts] = ca
    zb_s[HALO:HALO + ts] = cb
    na, nb = _conv_inputs(next_ref[0])
    za_s[HALO + ts:] = na * has_next
    zb_s[HALO + ts:] = nb * has_next

    wa = wa_ref[...]
    wb = wb_ref[...]
    lane_head = lax.broadcasted_iota(jnp.int32, (1, GROUP_W), 1) // HEAD_W
    for r0 in range(0, ts, rc):
        acc = wa[0:1] * za_s[HALO + r0 - 1:HALO + r0 - 1 + rc]
        for j in range(1, CONV_A):
            acc = acc + wa[j:j + 1] * za_s[HALO + r0 - 1 + j:HALO + r0 - 1 + j + rc]
        ya = cur_ref[0, r0:r0 + rc, 0:GROUP_W] * acc
        o_ref[0, r0:r0 + rc, 0:GROUP_W] = _rms(ya, gg_ref[:, 0:GROUP_W]).astype(o_ref.dtype)

        base = HALO + r0 - CONV_B // 2
        acc = wb[0:1] * zb_s[base:base + rc]
        for j in range(1, CONV_B):
            acc = acc + wb[j:j + 1] * zb_s[base + j:base + j + rc]
        yb = _layer_norm(acc + bb_ref[...], blg_ref[...], blb_ref[...])
        yb = yb * jax.nn.sigmoid(yb)
        o_ref[0, r0:r0 + rc, GROUP_W:2 * GROUP_W] = _rms(yb, gg_ref[:, GROUP_W:2 * GROUP_W]).astype(o_ref.dtype)

    for r0 in range(0, ts, CHUNK):
        z = jax.nn.gelu(cur_ref[0, r0:r0 + CHUNK, OFF_C:OFF_D])
        u = z[:, :GROUP_W]
        v = _layer_norm(z[:, GROUP_W:], slg_ref[...], slb_ref[...]).astype(BF16)
        mixed = sbias_ref[...]
        for h in range(N_HEADS):
            vh = v * (lane_head == h).astype(BF16)
            mixed = mixed + jnp.dot(sw_ref[h], vh, preferred_element_type=F32)
        yc = u * mixed
        o_ref[0, r0:r0 + CHUNK, 2 * GROUP_W:3 * GROUP_W] = _rms(yc, gg_ref[:, 2 * GROUP_W:3 * GROUP_W]).astype(o_ref.dtype)


def _conv_mixer(p, wa, wb, bb, blg, blb, slg, slb, sw_bf16, sbias, gg, ts):
    b, s, _ = p.shape
    rc = min(64, ts)
    hb = ts // HALO
    n_halo = s // HALO
    row = lambda a: a.reshape(1, -1)
    const = lambda *shape: pl.BlockSpec(shape, lambda bi, i: (0,) * len(shape))
    return pl.pallas_call(
        functools.partial(_mixer_body, ts=ts, rc=rc),
        grid=(b, s // ts),
        in_specs=[pl.BlockSpec((1, HALO, OFF_D), lambda bi, i: (bi, jnp.maximum(i * hb - 1, 0), 0)),
                  pl.BlockSpec((1, ts, OFF_D), lambda bi, i: (bi, i, 0)),
                  pl.BlockSpec((1, HALO, OFF_D), lambda bi, i: (bi, jnp.minimum((i + 1) * hb, n_halo - 1), 0)),
                  const(CONV_A, GROUP_W), const(CONV_B, GROUP_W), const(1, GROUP_W),
                  const(1, GROUP_W), const(1, GROUP_W), const(1, GROUP_W), const(1, GROUP_W),
                  const(N_HEADS, CHUNK, CHUNK), const(CHUNK, GROUP_W), const(1, 3 * GROUP_W)],
        out_specs=pl.BlockSpec((1, ts, 3 * GROUP_W), lambda bi, i: (bi, i, 0)),
        out_shape=jax.ShapeDtypeStruct((b, s, 3 * GROUP_W), BF16),
        scratch_shapes=[pltpu.VMEM((ts + 2 * HALO, GROUP_W), F32),
                        pltpu.VMEM((ts + 2 * HALO, GROUP_W), F32)],
        compiler_params=_params("parallel", "parallel"),
        name="conv_mixer",
    )(p, p, p, wa, wb, row(bb), row(blg), row(blb), row(slg), row(slb), sw_bf16, sbias, row(gg))


def _head_masks():
    lane_head = lax.broadcasted_iota(jnp.int32, (1, GROUP_W), 1) // HEAD_W
    return [(lane_head == h).astype(BF16) for h in range(N_HEADS)]


_NT = (((1,), (1,)), ((), ()))


def _attend(q, masks, keys, values, biases, acc):
    for h in range(N_HEADS):
        qh = q * masks[h]
        scores = []
        for k, bias in zip(keys, biases):
            s = lax.dot_general(qh, k, _NT, preferred_element_type=F32)
            scores.append(s if bias is None else s + bias(h))
        m = scores[0].max(axis=-1, keepdims=True)
        for s in scores[1:]:
            m = jnp.maximum(m, s.max(axis=-1, keepdims=True))
        probs = [jnp.exp(s - m) for s in scores]
        denom = probs[0].sum(axis=-1, keepdims=True)
        for p in probs[1:]:
            denom = denom + p.sum(axis=-1, keepdims=True)
        o = jnp.dot(probs[0].astype(BF16), values[0](h), preferred_element_type=F32)
        for p, v in zip(probs[1:], values[1:]):
            o = o + jnp.dot(p.astype(BF16), v(h), preferred_element_type=F32)
        acc = acc + o / denom
    return acc


def _nbr_attn_body(q_ref, kp_ref, kc_ref, kn_ref, kx_ref, bias_ref, gg_ref, o_ref,
                   k_s, v_s, kx_s, vx_s, *, starts):
    masks = _head_masks()
    n_half = kp_ref.shape[1]
    n_cur = kc_ref.shape[1]
    pieces = ((kp_ref, 0, n_half), (kc_ref, n_half, n_cur), (kn_ref, n_half + n_cur, n_half))
    for ref, off, n in pieces:
        k_s[off:off + n] = ref[0, :, 0:GROUP_W]
        for h in range(N_HEADS):
            v_s[h, off:off + n] = ref[0, :, GROUP_W:] * masks[h]
    kx_s[...] = kx_ref[0, :, 0:GROUP_W]
    for h in range(N_HEADS):
        vx_s[h] = kx_ref[0, :, GROUP_W:] * masks[h]

    j = pl.program_id(1)
    nb = pl.num_programs(1)
    n_win = bias_ref.shape[3]
    tq = q_ref.shape[1] // len(starts[0])
    for sb, (first, mid, last) in enumerate(zip(*starts)):
        st = jnp.where(j == 0, first, jnp.where(j == nb - 1, last, mid)) * GRID_W
        st = pl.multiple_of(st, GRID_W)
        r0 = sb * tq
        acc = _attend(
            q_ref[0, r0:r0 + tq, :], masks,
            keys=[k_s[pl.ds(st, n_win)], kx_s[...]],
            values=[lambda h: v_s[h, pl.ds(st, n_win)], lambda h: vx_s[h]],
            biases=[lambda h: bias_ref[0, h, r0:r0 + tq, :], None],
            acc=jnp.zeros((tq, GROUP_W), F32))
        o_ref[0, r0:r0 + tq, :] = _rms(acc, gg_ref[...]).astype(o_ref.dtype)


SUB_Q_ROWS = 2
SUB_KEY_ROWS = 10


def _block_kinds(rows):
    nb = rows // Q_BLOCK_ROWS
    return (0, min(1, nb - 1), nb - 1)


def _sub_window_starts(rows):
    n_key_rows = KEY_ROWS_BEFORE + Q_BLOCK_ROWS + KEY_ROWS_AFTER
    starts = []
    for jv in _block_kinds(rows):
        per_sub = []
        for sb in range(Q_BLOCK_ROWS // SUB_Q_ROWS):
            r = Q_BLOCK_ROWS * jv + SUB_Q_ROWS * sb + np.arange(SUB_Q_ROWS)
            ks = np.clip(r - WIN_R // 2, 0, rows - WIN_R) - Q_BLOCK_ROWS * jv + KEY_ROWS_BEFORE
            start = int(min(ks.min(), n_key_rows - SUB_KEY_ROWS))
            assert start <= ks.min() and ks.max() + WIN_R <= start + SUB_KEY_ROWS
            per_sub.append(start)
        starts.append(tuple(per_sub))
    return tuple(starts)


def _window_bias(rpb_l, rows):
    n_key_rows = KEY_ROWS_BEFORE + Q_BLOCK_ROWS + KEY_ROWS_AFTER
    sub_starts = _sub_window_starts(rows)
    tq = SUB_Q_ROWS * GRID_W
    i = np.arange(Q_BLOCK_ROWS)[:, None, None, None]
    c = np.arange(GRID_W)[None, :, None, None]
    kr = np.arange(-KEY_ROWS_BEFORE, Q_BLOCK_ROWS + KEY_ROWS_AFTER)[None, None, :, None]
    kc = np.arange(GRID_W)[None, None, None, :]
    shape = (Q_BLOCK_ROWS, GRID_W, n_key_rows, GRID_W)
    edge = GRID_W - WIN_C
    cols = jnp.pad(rpb_l, ((0, 0), (0, 0), (edge, edge)), mode="edge")
    by_col = jnp.stack([cols[:, :, GRID_W - 1 - cc:2 * GRID_W - 1 - cc] for cc in range(GRID_W)],
                       axis=2)
    lo = KEY_ROWS_BEFORE
    hi = Q_BLOCK_ROWS + KEY_ROWS_AFTER - WIN_R
    by_col = jnp.pad(by_col, ((0, 0), (lo, hi), (0, 0), (0, 0)))
    per_row = [by_col[:, WIN_R - 1 - ii:WIN_R - 1 - ii + n_key_rows].transpose(0, 2, 1, 3)
               for ii in range(Q_BLOCK_ROWS)]
    values = jnp.stack(per_row, axis=1)
    tables = []
    for jv, starts in zip(_block_kinds(rows), sub_starts):
        r = Q_BLOCK_ROWS * jv + i
        ks = np.clip(r - WIN_R // 2, 0, rows - WIN_R)
        krow = Q_BLOCK_ROWS * jv + kr
        c_start = np.clip(c - WIN_C // 2, 0, GRID_W - WIN_C)
        valid = ((krow >= ks) & (krow < ks + WIN_R) & (krow >= 0) & (krow < rows)
                 & (kc >= c_start) & (kc < c_start + WIN_C))
        bias = jnp.where(np.broadcast_to(valid, shape)[None], values, NEG_INF)
        bias = bias.reshape(N_HEADS, Q_BLOCK_ROWS * GRID_W, n_key_rows * GRID_W)
        tables.append(jnp.concatenate(
            [bias[:, sb * tq:(sb + 1) * tq, st * GRID_W:(st + SUB_KEY_ROWS) * GRID_W]
             for sb, st in enumerate(starts)], axis=1))
    return jnp.stack(tables)


def _nbr_attention(q, kv, kv_ctx, bias, gg):
    b, s, _ = q.shape
    n_ctx = kv_ctx.shape[1]
    tb = Q_BLOCK_ROWS * GRID_W
    half = KEY_ROWS_BEFORE * GRID_W
    nb = s // tb
    assert nb >= 2 and KEY_ROWS_BEFORE == KEY_ROWS_AFTER and tb == 2 * half
    n_loc = tb + 2 * half
    n_win = SUB_KEY_ROWS * GRID_W
    variant = lambda j: jnp.where(j == 0, 0, jnp.where(j == nb - 1, 2, 1))
    return pl.pallas_call(
        functools.partial(_nbr_attn_body, starts=_sub_window_starts(s // GRID_W)),
        grid=(b, nb),
        in_specs=[pl.BlockSpec((1, tb, GROUP_W), lambda bi, j: (bi, j, 0)),
                  pl.BlockSpec((1, half, 2 * GROUP_W), lambda bi, j: (bi, jnp.maximum(2 * j - 1, 0), 0)),
                  pl.BlockSpec((1, tb, 2 * GROUP_W), lambda bi, j: (bi, j, 0)),
                  pl.BlockSpec((1, half, 2 * GROUP_W), lambda bi, j: (bi, jnp.minimum(2 * j + 2, 2 * nb - 1), 0)),
                  pl.BlockSpec((1, n_ctx, 2 * GROUP_W), lambda bi, j: (bi, 0, 0)),
                  pl.BlockSpec((1, N_HEADS, tb, n_win), lambda bi, j: (variant(j), 0, 0, 0)),
                  pl.BlockSpec((1, GROUP_W), lambda bi, j: (0, 0))],
        out_specs=pl.BlockSpec((1, tb, GROUP_W), lambda bi, j: (bi, j, 0)),
        out_shape=jax.ShapeDtypeStruct((b, s, GROUP_W), BF16),
        scratch_shapes=[pltpu.VMEM((n_loc, GROUP_W), BF16),
                        pltpu.VMEM((N_HEADS, n_loc, GROUP_W), BF16),
                        pltpu.VMEM((n_ctx, GROUP_W), BF16),
                        pltpu.VMEM((N_HEADS, n_ctx, GROUP_W), BF16)],
        compiler_params=_params("parallel", "arbitrary"),
        name="nbr_attention",
    )(q, kv, kv, kv, kv_ctx, bias, gg.reshape(1, GROUP_W))


def _ctx_attn_body(q_ref, kx_ref, gg_ref, o_ref, vx_s):
    masks = _head_masks()
    for h in range(N_HEADS):
        vx_s[h] = kx_ref[0, :, GROUP_W:] * masks[h]
    tq = q_ref.shape[1]
    acc = _attend(q_ref[0], masks, keys=[kx_ref[0, :, 0:GROUP_W]], values=[lambda h: vx_s[h]],
                  biases=[None], acc=jnp.zeros((tq, GROUP_W), F32))
    o_ref[0] = _rms(acc, gg_ref[...]).astype(o_ref.dtype)


def _ctx_attention(q, kv_ctx, gg):
    b, n_ctx, _ = q.shape
    return pl.pallas_call(
        _ctx_attn_body,
        grid=(b,),
        in_specs=[pl.BlockSpec((1, n_ctx, GROUP_W), lambda bi: (bi, 0, 0)),
                  pl.BlockSpec((1, n_ctx, 2 * GROUP_W), lambda bi: (bi, 0, 0)),
                  pl.BlockSpec((1, GROUP_W), lambda bi: (0, 0))],
        out_specs=pl.BlockSpec((1, n_ctx, GROUP_W), lambda bi: (bi, 0, 0)),
        out_shape=jax.ShapeDtypeStruct((b, n_ctx, GROUP_W), BF16),
        scratch_shapes=[pltpu.VMEM((N_HEADS, n_ctx, GROUP_W), BF16)],
        compiler_params=_params("parallel"),
        name="ctx_attention",
    )(q, kv_ctx, gg.reshape(1, GROUP_W))


def _merge_body(abc_ref, dn_ref, w_ref, x_ref, mod_ref, g_ref, *rest, with_router):
    if with_router:
        rw_ref, rb_ref, xo_ref, h_ref, info_ref = rest
    else:
        xo_ref, h_ref = rest
    y = (jnp.dot(abc_ref[0], w_ref[0:3 * GROUP_W], preferred_element_type=F32)
         + jnp.dot(dn_ref[0], w_ref[3 * GROUP_W:], preferred_element_type=F32))
    m = mod_ref[0]
    xn = x_ref[0] + m[M_G1:M_G1 + 1] * y
    xo_ref[0] = xn
    h = _rms(xn, g_ref[...]) * (1.0 + m[M_SC2:M_SC2 + 1]) + m[M_SH2:M_SH2 + 1]
    if not with_router:
        h_ref[0] = h.astype(h_ref.dtype)
    else:
        n_sub = h.shape[1] // LANES
        for s in range(n_sub):
            h_ref[0, pl.ds(s, h.shape[0], stride=n_sub), :] = h[:, s * LANES:(s + 1) * LANES]
        logits = jnp.dot(h.astype(BF16), rw_ref[...], preferred_element_type=F32) + rb_ref[...]
        lane = lax.broadcasted_iota(jnp.int32, logits.shape, 1).astype(F32)
        m1 = logits.max(axis=-1, keepdims=True)
        i1 = jnp.where(logits == m1, lane, float(LANES)).min(axis=-1, keepdims=True)
        rest_logits = jnp.where(lane == i1, NEG_INF, logits)
        m2 = rest_logits.max(axis=-1, keepdims=True)
        i2 = jnp.where(rest_logits == m2, lane, float(LANES)).min(axis=-1, keepdims=True)
        e2 = jnp.exp(m2 - m1)
        den = 1.0 + e2
        info = jnp.where(lane == 0, i1, jnp.where(lane == 1, i2,
               jnp.where(lane == 2, 1.0 / den, jnp.where(lane == 3, e2 / den, 0.0))))
        info_ref[0] = info


def _merge(abc, dn, w_out_bf16, x, mod, mod_row, g2, tm, h_dtype, router=None):
    b, s, d = x.shape
    in_specs = [pl.BlockSpec((1, tm, 3 * GROUP_W), lambda bi, i: (bi, i, 0)),
                pl.BlockSpec((1, tm, GROUP_W), lambda bi, i: (bi, i, 0)),
                pl.BlockSpec((4 * GROUP_W, d), lambda bi, i: (0, 0)),
                pl.BlockSpec((1, tm, d), lambda bi, i: (bi, i, 0)),
                pl.BlockSpec((1, 6, d), lambda bi, i: (mod_row(bi), 0, 0)),
                pl.BlockSpec((1, d), lambda bi, i: (0, 0))]
    out_specs = [pl.BlockSpec((1, tm, d), lambda bi, i: (bi, i, 0)),
                 pl.BlockSpec((1, tm, d), lambda bi, i: (bi, i, 0))]
    out_shape = [jax.ShapeDtypeStruct((b, s, d), F32), jax.ShapeDtypeStruct((b, s, d), h_dtype)]
    args = [abc, dn, w_out_bf16, x, mod, g2.reshape(1, d)]
    if router is not None:
        out_specs[1] = pl.BlockSpec((1, tm * d // LANES, LANES), lambda bi, i: (bi, i, 0))
        out_shape[1] = jax.ShapeDtypeStruct((b, s * d // LANES, LANES), h_dtype)
        in_specs += [pl.BlockSpec((d, LANES), lambda bi, i: (0, 0)),
                     pl.BlockSpec((1, LANES), lambda bi, i: (0, 0))]
        out_specs.append(pl.BlockSpec((1, tm, LANES), lambda bi, i: (bi, i, 0)))
        out_shape.append(jax.ShapeDtypeStruct((b, s, LANES), F32))
        args += list(router)
    return pl.pallas_call(
        functools.partial(_merge_body, with_router=router is not None),
        grid=(b, s // tm),
        in_specs=in_specs, out_specs=out_specs, out_shape=out_shape,
        compiler_params=_params("parallel", "parallel"),
        name="merge_router" if router is not None else "merge",
    )(*args)


def _swiglu_chunk(xb, w1_ref, w3_ref, w2_ref, c0, cw):
    a = jnp.dot(xb, w1_ref[:, c0:c0 + cw], preferred_element_type=F32)
    g = jnp.dot(xb, w3_ref[:, c0:c0 + cw], preferred_element_type=F32)
    act = (a * jax.nn.sigmoid(a) * g).astype(BF16)
    return jnp.dot(act, w2_ref[c0:c0 + cw, :], preferred_element_type=F32)


def _hidden_chunks(f, tf):
    return [(c0, min(tf, f - c0)) for c0 in range(0, f, tf)]


def _ffn_body(h_ref, w1_ref, w3_ref, w2_ref, x_ref, mod_ref, o_ref, acc_ref, *, tf):
    h = h_ref[...]
    for n, (c0, cw) in enumerate(_hidden_chunks(w1_ref.shape[1], tf)):
        part = _swiglu_chunk(h, w1_ref, w3_ref, w2_ref, c0, cw)
        if n == 0:
            acc_ref[...] = part
        else:
            acc_ref[...] += part
    o_ref[...] = x_ref[...] + mod_ref[0, M_G2:M_G2 + 1] * acc_ref[...]


def _dense_ffn(h, w1, w3, w2, x, mod, mod_row, tm, tf):
    t, d = h.shape
    f = w1.shape[1]
    resident = pl.Buffered(1)
    return pl.pallas_call(
        functools.partial(_ffn_body, tf=tf),
        grid=(t // tm,),
        in_specs=[pl.BlockSpec((tm, d), lambda i: (i, 0)),
                  pl.BlockSpec((d, f), lambda i: (0, 0), pipeline_mode=resident),
                  pl.BlockSpec((d, f), lambda i: (0, 0), pipeline_mode=resident),
                  pl.BlockSpec((f, d), lambda i: (0, 0), pipeline_mode=resident),
                  pl.BlockSpec((tm, d), lambda i: (i, 0)),
                  pl.BlockSpec((1, 6, d), lambda i: (mod_row(i), 0, 0))],
        out_specs=pl.BlockSpec((tm, d), lambda i: (i, 0)),
        out_shape=jax.ShapeDtypeStruct((t, d), F32),
        scratch_shapes=[pltpu.VMEM((tm, d), F32)],
        compiler_params=_params("parallel"),
        name="dense_ffn",
    )(h, w1, w3, w2, x, mod)


WINDOW_ALIGN = 16


def _route_plan(expert_idx, tm, tt):
    n_assign = expert_idx.size
    e_flat = expert_idx.reshape(-1)
    onehot = (e_flat[:, None] == jnp.arange(N_EXPERTS, dtype=jnp.int32)[None, :]).astype(jnp.int32)
    csum = jnp.cumsum(onehot, axis=0)
    counts = csum[-1]
    rank = jnp.sum((csum - onehot) * onehot, axis=1)
    padded = (counts + tm - 1) // tm * tm
    pad_end = jnp.cumsum(padded)
    pad_start = pad_end - padded
    slot = jnp.sum(onehot * pad_start[None, :], axis=1) + rank
    total = n_assign + N_EXPERTS * tm
    n_blk = total // tm
    slot_tok = jnp.zeros((total,), jnp.int32).at[slot].set(jnp.arange(n_assign, dtype=jnp.int32) // 2)
    n_active = pad_end[-1] // tm
    blk = jnp.arange(n_blk, dtype=jnp.int32)
    blk_exp = jnp.minimum(jnp.searchsorted(pad_end, blk * tm, side="right"), N_EXPERTS - 1).astype(jnp.int32)
    blk_exp = jnp.where(blk < n_active, blk_exp, blk_exp[jnp.maximum(n_active - 1, 0)])

    wa = tt + WINDOW_ALIGN
    n_tiles = n_assign // (2 * tt)
    before = jnp.concatenate([jnp.zeros((1, N_EXPERTS), jnp.int32), csum[2 * tt - 1::2 * tt][:-1]], axis=0)
    first = pad_start[None, :] + before
    win_base = jnp.minimum(first // WINDOW_ALIGN * WINDOW_ALIGN, total - wa)
    win_row0 = jnp.arange(N_EXPERTS, dtype=jnp.int32)[None, :] * wa - win_base
    win_loc = slot + jnp.sum(onehot.reshape(n_tiles, 2 * tt, N_EXPERTS) * win_row0[:, None, :],
                             axis=2).reshape(n_assign)
    return (slot_tok.reshape(n_blk, tm), blk_exp, n_active.astype(jnp.int32).reshape(1),
            win_base.reshape(n_tiles * N_EXPERTS).astype(jnp.int32), win_loc.reshape(-1, 2))


ROW_ISSUE_UNROLL = 8


def _moe_body(bexp_ref, nact_ref, tok_hbm, h_hbm, w1_ref, w3_ref, w2_ref, o_ref,
              tok_s, x_s, xb_s, acc_s, sem_idx, sem_row, *, tm, tf):
    i = pl.program_id(0)
    n_active = nact_ref[0]
    chunks = _hidden_chunks(w1_ref.shape[2], tf)
    rows_per_chunk = -(-tm // len(chunks))

    def idx_copy(step):
        return pltpu.make_async_copy(tok_hbm.at[step], tok_s.at[step % 2], sem_idx.at[step % 2])

    n_sub = h_hbm.shape[1]

    def row_copy(slot, r, tok):
        return pltpu.make_async_copy(h_hbm.at[tok], x_s.at[slot, pl.ds(r * n_sub, n_sub)], sem_row.at[slot])

    @pl.when(i == 0)
    def _():
        idx_copy(0).start()
        idx_copy(0).wait()

        def issue(r, c):
            row_copy(0, r, tok_s[0, r]).start()
            return c

        lax.fori_loop(0, tm, issue, 0, unroll=ROW_ISSUE_UNROLL)
        idx_copy(1).start()

    @pl.when(i <= n_active)
    def _():
        slot = i % 2
        pltpu.make_async_copy(x_s.at[slot], x_s.at[slot], sem_row.at[slot]).wait()
        for s in range(n_sub):
            xb_s[:, s * LANES:(s + 1) * LANES] = x_s[slot, pl.ds(s, tm, stride=n_sub), :].astype(BF16)

    @pl.when(i < n_active)
    def _():
        nxt = (i + 1) % 2
        idx_copy(i + 1).wait()
        xb = xb_s[...]
        for n, (c0, cw) in enumerate(chunks):
            part = _swiglu_chunk(xb, w1_ref.at[0], w3_ref.at[0], w2_ref.at[0], c0, cw)
            if n == 0:
                acc_s[...] = part
            else:
                acc_s[...] += part
            for r in range(n * rows_per_chunk, min((n + 1) * rows_per_chunk, tm)):
                row_copy(nxt, r, tok_s[nxt, r]).start()

        @pl.when(i + 2 <= n_active)
        def _():
            idx_copy(i + 2).start()

    @pl.when(i >= n_active)
    def _():
        acc_s[...] = jnp.zeros(acc_s.shape, acc_s.dtype)

    o_ref[...] = acc_s[...].astype(o_ref.dtype)


def _moe_experts(h, slot_tok, blk_exp, n_active, w1, w3, w2, tm, tf):
    t, n_sub, _ = h.shape
    d = n_sub * LANES
    n_blk = slot_tok.shape[0]
    f = w1.shape[2]
    resident = pl.Buffered(1)
    grid_spec = pltpu.PrefetchScalarGridSpec(
        num_scalar_prefetch=2,
        grid=(n_blk,),
        in_specs=[pl.BlockSpec(memory_space=pl.ANY),
                  pl.BlockSpec(memory_space=pl.ANY),
                  pl.BlockSpec((1, d, f), lambda i, bexp, nact: (bexp[i], 0, 0), pipeline_mode=resident),
                  pl.BlockSpec((1, d, f), lambda i, bexp, nact: (bexp[i], 0, 0), pipeline_mode=resident),
                  pl.BlockSpec((1, f, d), lambda i, bexp, nact: (bexp[i], 0, 0), pipeline_mode=resident)],
        out_specs=pl.BlockSpec((tm, d), lambda i, bexp, nact: (i, 0)),
        scratch_shapes=[pltpu.SMEM((2, tm), jnp.int32),
                        pltpu.VMEM((2, tm * n_sub, LANES), F32),
                        pltpu.VMEM((tm, d), BF16),
                        pltpu.VMEM((tm, d), F32),
                        pltpu.SemaphoreType.DMA((2,)),
                        pltpu.SemaphoreType.DMA((2,))])
    return pl.pallas_call(
        functools.partial(_moe_body, tm=tm, tf=tf),
        grid_spec=grid_spec,
        out_shape=jax.ShapeDtypeStruct((n_blk * tm, d), BF16),
        compiler_params=_params("arbitrary"),
        name="moe_experts",
    )(blk_exp, n_active, slot_tok, h, w1, w3, w2)


def _combine_body(base_ref, y_hbm, info_ref, x_ref, mod_ref, g_ref, o_ref, win_s, sem, *, wa):
    i = pl.program_id(0)
    n = pl.num_programs(0)

    def window_copy(step, e):
        slot = step % 2
        base = pl.multiple_of(base_ref[step * N_EXPERTS + e], WINDOW_ALIGN)
        return pltpu.make_async_copy(y_hbm.at[pl.ds(base, wa)], win_s.at[slot, pl.ds(e * wa, wa)], sem.at[slot])

    @pl.when(i == 0)
    def _():
        for e in range(N_EXPERTS):
            window_copy(0, e).start()

    @pl.when(i + 1 < n)
    def _():
        for e in range(N_EXPERTS):
            window_copy(i + 1, e).start()

    for e in range(N_EXPERTS):
        window_copy(i, e).wait()

    info = info_ref[...]
    win = win_s[i % 2]
    row = lax.broadcasted_iota(jnp.int32, (info.shape[0], win.shape[0]), 1).astype(F32)
    pick0 = jnp.where(row == info[:, 0:1], 1.0, 0.0).astype(BF16)
    pick1 = jnp.where(row == info[:, 1:2], 1.0, 0.0).astype(BF16)
    f = (info[:, 2:3] * jnp.dot(pick0, win, preferred_element_type=F32)
         + info[:, 3:4] * jnp.dot(pick1, win, preferred_element_type=F32))
    xn = x_ref[...] + mod_ref[0, M_G2:M_G2 + 1] * f
    o_ref[...] = _rms(xn, g_ref[...])


def _combine_final(win_base, y, info, x, mod, mod_row, final_g, tt):
    t, d = x.shape
    wa = tt + WINDOW_ALIGN
    grid_spec = pltpu.PrefetchScalarGridSpec(
        num_scalar_prefetch=1,
        grid=(t // tt,),
        in_specs=[pl.BlockSpec(memory_space=pl.ANY),
                  pl.BlockSpec((tt, LANES), lambda i, base: (i, 0)),
                  pl.BlockSpec((tt, d), lambda i, base: (i, 0)),
                  pl.BlockSpec((1, 6, d), lambda i, base: (mod_row(i), 0, 0)),
                  pl.BlockSpec((1, d), lambda i, base: (0, 0))],
        out_specs=pl.BlockSpec((tt, d), lambda i, base: (i, 0)),
        scratch_shapes=[pltpu.VMEM((2, N_EXPERTS * wa, d), y.dtype),
                        pltpu.SemaphoreType.DMA((2,))])
    return pl.pallas_call(
        functools.partial(_combine_body, wa=wa),
        grid_spec=grid_spec,
        out_shape=jax.ShapeDtypeStruct((t, d), F32),
        compiler_params=_params("arbitrary"),
        name="moe_combine_final",
    )(win_base, y, info, x, mod, final_g.reshape(1, d))


def kernel(x, c, ctx, c_ctx, w_mod, b_mod, norm1_g, norm2_g, w_in, conv_a_w, conv_b_w, conv_b_b, conv_ln_g, conv_ln_b, sgu_ln_g, sgu_ln_b, sgu_w, sgu_b, rpb, group_g, w_out, ffn_w1, ffn_w3, ffn_w2, router_w, router_b, moe_w1, moe_w3, moe_w2, final_g):
    bsz, seq, d = x.shape
    n_ctx = ctx.shape[1]
    depth = w_mod.shape[0]
    rows = seq // GRID_W
    assert d == D_MODEL and seq % (Q_BLOCK_ROWS * GRID_W) == 0 and n_ctx % CHUNK == 0 and bsz + 1 <= 8
    assert depth == 2, "layer 0 dense with context, layer 1 (last) MoE without context"

    tm = min(512, seq)
    lat_row = lambda bi: bi
    ctx_row = lambda bi: bsz

    c8 = jnp.zeros((8, d), F32).at[:bsz].set(c).at[bsz].set(c_ctx)
    mod_all = _modulation(c8, w_mod, b_mod).reshape(depth, 8, 6, d)

    xl, xc = x, ctx
    out = None
    for l in range(depth):
        last = l == depth - 1
        mod = mod_all[l]
        w_in_b = w_in[l].astype(BF16)
        w_out_b = w_out[l].astype(BF16)
        sw_b = sgu_w[l].astype(BF16)
        sbias = jnp.repeat(sgu_b[l].T, HEAD_W, axis=1)
        gg = group_g[l]
        conv_args = (conv_a_w[l], conv_b_w[l], conv_b_b[l], conv_ln_g[l], conv_ln_b[l],
                     sgu_ln_g[l], sgu_ln_b[l], sw_b, sbias, gg[:3 * GROUP_W])
        gg_d = gg[3 * GROUP_W:]

        p, q, kv = _norm_proj(xl, norm1_g[l], mod, lat_row, w_in_b, tm)
        pc, qc, kvc = _norm_proj(xc, norm1_g[l], mod, ctx_row, w_in_b, n_ctx)
        abc = _conv_mixer(p, *conv_args, ts=tm)
        dn = _nbr_attention(q, kv, kvc, _window_bias(rpb[l], rows), gg_d)

        if not last:
            xl, h = _merge(abc, dn, w_out_b, xl, mod, lat_row, norm2_g[l], tm, BF16)
            abc_c = _conv_mixer(pc, *conv_args, ts=n_ctx)
            dn_c = _ctx_attention(qc, kvc, gg_d)
            xc, hc = _merge(abc_c, dn_c, w_out_b, xc, mod, ctx_row, norm2_g[l], n_ctx, BF16)
            w1, w3, w2 = (w[l // 2].astype(BF16) for w in (ffn_w1, ffn_w3, ffn_w2))
            tf = 512
            xl = _dense_ffn(h.reshape(bsz * seq, d), w1, w3, w2, xl.reshape(bsz * seq, d), mod,
                            lambda i: i * tm // seq, tm, tf).reshape(bsz, seq, d)
            xc = _dense_ffn(hc.reshape(bsz * n_ctx, d), w1, w3, w2, xc.reshape(bsz * n_ctx, d), mod,
                            lambda i: bsz, n_ctx, tf).reshape(bsz, n_ctx, d)
        else:
            rw = jnp.zeros((d, LANES), F32).at[:, :N_EXPERTS].set(router_w[l // 2]).astype(BF16)
            rb = jnp.full((1, LANES), NEG_INF, F32).at[0, :N_EXPERTS].set(router_b[l // 2])
            xl, h, info = _merge(abc, dn, w_out_b, xl, mod, lat_row, norm2_g[l], tm, F32, router=(rw, rb))
            t = bsz * seq
            info = info.reshape(t, LANES)
            moe_tm, moe_tf, tt = 512, 512, 256
            slot_tok, blk_exp, n_active, win_base, win_loc = _route_plan(
                info[:, :2].astype(jnp.int32), moe_tm, tt)
            w1, w3, w2 = (w[l // 2].astype(BF16) for w in (moe_w1, moe_w3, moe_w2))
            y = _moe_experts(h.reshape(t, d // LANES, LANES), slot_tok, blk_exp, n_active, w1, w3, w2,
                             moe_tm, moe_tf)
            info = jnp.concatenate([win_loc.astype(F32), info[:, 2:]], axis=1)
            out = _combine_final(win_base, y, info, xl.reshape(t, d), mod, lambda i: i * tt // seq,
                                 final_g, tt).reshape(bsz, seq, d)
    return out
```

```python
import functools
from typing import NamedTuple

import numpy as np
import jax
import jax.numpy as jnp
from jax import lax
from jax.experimental import pallas as pl
from jax.experimental.pallas import tpu as pltpu

F32 = jnp.float32
BF16 = jnp.bfloat16

D_MODEL = 1024
GRID_W = 64
EPS = 1e-6
NEG_INF = -1e30
GROUP_W = 256
N_HEADS = 4
HEAD_W = GROUP_W // N_HEADS
CONV_A = 3
CONV_B = 31
CHUNK = 128
WIN_R = 8
WIN_C = 16
OFF_B = 3 * GROUP_W
OFF_C = OFF_B + 2 * GROUP_W
OFF_D = OFF_C + 2 * GROUP_W
OFF_KV = OFF_D + GROUP_W
IN_COLS = OFF_KV + 2 * GROUP_W
N_EXPERTS = 8
LANES = 128
SUBLANES = 8
HALO = 16
Q_BLOCK_ROWS = 8
KEY_ROWS_BEFORE = 4
KEY_ROWS_AFTER = 4
VMEM_LIMIT = 56 * 1024 * 1024

M_SH1, M_SC1, M_G1, M_SH2, M_SC2, M_G2 = range(6)


def _params(*dims):
    return pltpu.CompilerParams(dimension_semantics=dims, vmem_limit_bytes=VMEM_LIMIT)


def _rms(x, g):
    return x * lax.rsqrt(jnp.mean(x * x, axis=-1, keepdims=True) + EPS) * g


def _layer_norm(x, g, b):
    mu = jnp.mean(x, axis=-1, keepdims=True)
    xc = x - mu
    var = jnp.mean(xc * xc, axis=-1, keepdims=True)
    return xc * lax.rsqrt(var + EPS) * g + b


def _mod_body(c_ref, w_ref, b_ref, o_ref):
    s = c_ref[...]
    s = s * jax.nn.sigmoid(s)
    o_ref[0] = jnp.dot(s, w_ref[0], preferred_element_type=F32,
                       precision=lax.Precision.HIGHEST) + b_ref[0]


def _modulation(c8, w_mod, b_mod):
    n_layers, d, six_d = w_mod.shape
    return pl.pallas_call(
        _mod_body,
        grid=(n_layers, six_d // d),
        in_specs=[pl.BlockSpec((8, d), lambda l, j: (0, 0)),
                  pl.BlockSpec((1, d, d), lambda l, j: (l, 0, j)),
                  pl.BlockSpec((1, 1, d), lambda l, j: (l, 0, j))],
        out_specs=pl.BlockSpec((1, 8, d), lambda l, j: (l, 0, j)),
        out_shape=jax.ShapeDtypeStruct((n_layers, 8, six_d), F32),
        compiler_params=_params("arbitrary", "arbitrary"),
        name="modulation",
    )(c8, w_mod, b_mod.reshape(n_layers, 1, six_d))


def _norm_proj_body(x_ref, g_ref, mod_ref, w_ref, p_ref, q_ref, kv_ref):
    m = mod_ref[0]
    h = _rms(x_ref[0], g_ref[...]) * (1.0 + m[M_SC1:M_SC1 + 1]) + m[M_SH1:M_SH1 + 1]
    hb = h.astype(BF16)
    p_ref[0] = jnp.dot(hb, w_ref[:, :OFF_D], preferred_element_type=F32)
    q = jnp.dot(hb, w_ref[:, OFF_D:OFF_KV], preferred_element_type=F32)
    q_ref[0] = (q * (HEAD_W ** -0.5)).astype(BF16)
    kv_ref[0] = jnp.dot(hb, w_ref[:, OFF_KV:], preferred_element_type=F32).astype(BF16)


def _norm_proj(x, g, mod, mod_row, w_in_bf16, tm):
    b, s, d = x.shape
    return pl.pallas_call(
        _norm_proj_body,
        grid=(b, s // tm),
        in_specs=[pl.BlockSpec((1, tm, d), lambda bi, i: (bi, i, 0)),
                  pl.BlockSpec((1, d), lambda bi, i: (0, 0)),
                  pl.BlockSpec((1, 6, d), lambda bi, i: (mod_row(bi), 0, 0)),
                  pl.BlockSpec((d, IN_COLS), lambda bi, i: (0, 0))],
        out_specs=[pl.BlockSpec((1, tm, OFF_D), lambda bi, i: (bi, i, 0)),
                   pl.BlockSpec((1, tm, GROUP_W), lambda bi, i: (bi, i, 0)),
                   pl.BlockSpec((1, tm, 2 * GROUP_W), lambda bi, i: (bi, i, 0))],
        out_shape=[jax.ShapeDtypeStruct((b, s, OFF_D), F32),
                   jax.ShapeDtypeStruct((b, s, GROUP_W), BF16),
                   jax.ShapeDtypeStruct((b, s, 2 * GROUP_W), BF16)],
        compiler_params=_params("parallel", "parallel"),
        name="norm_proj",
    )(x, g.reshape(1, d), mod, w_in_bf16)


def _conv_inputs(blk):
    za = blk[:, GROUP_W:2 * GROUP_W] * blk[:, 2 * GROUP_W:3 * GROUP_W]
    zb = blk[:, OFF_B:OFF_B + GROUP_W] * jax.nn.sigmoid(blk[:, OFF_B + GROUP_W:OFF_C])
    return za, zb


def _mixer_body(prev_ref, cur_ref, next_ref, wa_ref, wb_ref, bb_ref, blg_ref, blb_ref,
                slg_ref, slb_ref, sw_ref, sbias_ref, gg_ref, o_ref, za_s, zb_s, *, ts, rc):
    i = pl.program_id(1)
    n = pl.num_programs(1)
    has_prev = (i > 0).astype(F32)
    has_next = (i < n - 1).astype(F32)

    pa, pb = _conv_inputs(prev_ref[0])
    za_s[0:HALO] = pa * has_prev
    zb_s[0, 0:HALO] = pb * has_prev
    ca, cb = _conv_inputs(cur_ref[0])
    za_s[HALO:HALO + ts] = ca
    zb_s[0, HALO:HALO + ts] = cb
    na, nb = _conv_inputs(next_ref[0])
    za_s[HALO + ts:] = na * has_next
    zb_s[0, HALO + ts:] = nb * has_next
    n_shifted = ts + 2 * HALO - SUBLANES
    for b in range(1, SUBLANES):
        zb_s[b, 0:n_shifted] = zb_s[0, b:b + n_shifted]

    wa = wa_ref[...]
    wb = wb_ref[...]
    lane_head = lax.broadcasted_iota(jnp.int32, (1, GROUP_W), 1) // HEAD_W
    for r0 in range(0, ts, rc):
        acc = wa[0:1] * za_s[HALO + r0 - 1:HALO + r0 - 1 + rc]
        for j in range(1, CONV_A):
            acc = acc + wa[j:j + 1] * za_s[HALO + r0 - 1 + j:HALO + r0 - 1 + j + rc]
        ya = cur_ref[0, r0:r0 + rc, 0:GROUP_W] * acc
        o_ref[0, r0:r0 + rc, 0:GROUP_W] = _rms(ya, gg_ref[:, 0:GROUP_W]).astype(o_ref.dtype)

        base = HALO + r0 - CONV_B // 2
        acc = None
        for j in range(CONV_B):
            b, a = (base + j) % SUBLANES, (base + j) // SUBLANES * SUBLANES
            term = wb[j:j + 1] * zb_s[b, a:a + rc]
            acc = term if acc is None else acc + term
        yb = _layer_norm(acc + bb_ref[...], blg_ref[...], blb_ref[...])
        yb = yb * jax.nn.sigmoid(yb)
        o_ref[0, r0:r0 + rc, GROUP_W:2 * GROUP_W] = _rms(yb, gg_ref[:, GROUP_W:2 * GROUP_W]).astype(o_ref.dtype)

    for r0 in range(0, ts, CHUNK):
        z = jax.nn.gelu(cur_ref[0, r0:r0 + CHUNK, OFF_C:OFF_D])
        u = z[:, :GROUP_W]
        v = _layer_norm(z[:, GROUP_W:], slg_ref[...], slb_ref[...]).astype(BF16)
        mixed = sbias_ref[...]
        for h in range(N_HEADS):
            vh = v * (lane_head == h).astype(BF16)
            mixed = mixed + jnp.dot(sw_ref[h], vh, preferred_element_type=F32)
        yc = u * mixed
        o_ref[0, r0:r0 + CHUNK, 2 * GROUP_W:3 * GROUP_W] = _rms(yc, gg_ref[:, 2 * GROUP_W:3 * GROUP_W]).astype(o_ref.dtype)


def _conv_mixer(p, wa, wb, bb, blg, blb, slg, slb, sw_bf16, sbias, gg, ts):
    b, s, _ = p.shape
    rc = min(64, ts)
    hb = ts // HALO
    n_halo = s // HALO
    row = lambda a: a.reshape(1, -1)
    const = lambda *shape: pl.BlockSpec(shape, lambda bi, i: (0,) * len(shape))
    return pl.pallas_call(
        functools.partial(_mixer_body, ts=ts, rc=rc),
        grid=(b, s // ts),
        in_specs=[pl.BlockSpec((1, HALO, OFF_D), lambda bi, i: (bi, jnp.maximum(i * hb - 1, 0), 0)),
                  pl.BlockSpec((1, ts, OFF_D), lambda bi, i: (bi, i, 0)),
                  pl.BlockSpec((1, HALO, OFF_D), lambda bi, i: (bi, jnp.minimum((i + 1) * hb, n_halo - 1), 0)),
                  const(CONV_A, GROUP_W), const(CONV_B, GROUP_W), const(1, GROUP_W),
                  const(1, GROUP_W), const(1, GROUP_W), const(1, GROUP_W), const(1, GROUP_W),
                  const(N_HEADS, CHUNK, CHUNK), const(CHUNK, GROUP_W), const(1, 3 * GROUP_W)],
        out_specs=pl.BlockSpec((1, ts, 3 * GROUP_W), lambda bi, i: (bi, i, 0)),
        out_shape=jax.ShapeDtypeStruct((b, s, 3 * GROUP_W), BF16),
        scratch_shapes=[pltpu.VMEM((ts + 2 * HALO, GROUP_W), F32),
                        pltpu.VMEM((SUBLANES, ts + 2 * HALO, GROUP_W), F32)],
        compiler_params=_params("parallel", "parallel"),
        name="conv_mixer",
    )(p, p, p, wa, wb, row(bb), row(blg), row(blb), row(slg), row(slb), sw_bf16, sbias, row(gg))


def _head_masks():
    lane_head = lax.broadcasted_iota(jnp.int32, (1, GROUP_W), 1) // HEAD_W
    return [(lane_head == h).astype(BF16) for h in range(N_HEADS)]


_NT = (((1,), (1,)), ((), ()))


def _attend(q, masks, keys, values, biases, acc):
    for h in range(N_HEADS):
        qh = q * masks[h]
        scores = []
        for k, bias in zip(keys, biases):
            s = lax.dot_general(qh, k, _NT, preferred_element_type=F32)
            scores.append(s if bias is None else s + bias(h))
        m = scores[0].max(axis=-1, keepdims=True)
        for s in scores[1:]:
            m = jnp.maximum(m, s.max(axis=-1, keepdims=True))
        probs = [jnp.exp(s - m) for s in scores]
        denom = probs[0].sum(axis=-1, keepdims=True)
        for p in probs[1:]:
            denom = denom + p.sum(axis=-1, keepdims=True)
        o = jnp.dot(probs[0].astype(BF16), values[0](h), preferred_element_type=F32)
        for p, v in zip(probs[1:], values[1:]):
            o = o + jnp.dot(p.astype(BF16), v(h), preferred_element_type=F32)
        acc = acc + o / denom
    return acc


def _nbr_attn_body(q_ref, kp_ref, kc_ref, kn_ref, kx_ref, bias_ref, gg_ref, o_ref,
                   k_s, v_s, kx_s, vx_s, *, starts):
    masks = _head_masks()
    n_half = kp_ref.shape[1]
    n_cur = kc_ref.shape[1]
    pieces = ((kp_ref, 0, n_half), (kc_ref, n_half, n_cur), (kn_ref, n_half + n_cur, n_half))
    for ref, off, n in pieces:
        k_s[off:off + n] = ref[0, :, 0:GROUP_W]
        for h in range(N_HEADS):
            v_s[h, off:off + n] = ref[0, :, GROUP_W:] * masks[h]
    kx_s[...] = kx_ref[0, :, 0:GROUP_W]
    for h in range(N_HEADS):
        vx_s[h] = kx_ref[0, :, GROUP_W:] * masks[h]

    j = pl.program_id(1)
    nb = pl.num_programs(1)
    n_win = bias_ref.shape[3]
    tq = q_ref.shape[1] // len(starts[0])
    for sb, (first, mid, last) in enumerate(zip(*starts)):
        st = jnp.where(j == 0, first, jnp.where(j == nb - 1, last, mid)) * GRID_W
        st = pl.multiple_of(st, GRID_W)
        r0 = sb * tq
        acc = _attend(
            q_ref[0, r0:r0 + tq, :], masks,
            keys=[k_s[pl.ds(st, n_win)], kx_s[...]],
            values=[lambda h: v_s[h, pl.ds(st, n_win)], lambda h: vx_s[h]],
            biases=[lambda h: bias_ref[0, h, r0:r0 + tq, :], None],
            acc=jnp.zeros((tq, GROUP_W), F32))
        o_ref[0, r0:r0 + tq, :] = _rms(acc, gg_ref[...]).astype(o_ref.dtype)


SUB_Q_ROWS = 4
SUB_KEY_ROWS = 12


def _block_kinds(rows):
    nb = rows // Q_BLOCK_ROWS
    return (0, min(1, nb - 1), nb - 1)


def _sub_window_starts(rows):
    n_key_rows = KEY_ROWS_BEFORE + Q_BLOCK_ROWS + KEY_ROWS_AFTER
    starts = []
    for jv in _block_kinds(rows):
        per_sub = []
        for sb in range(Q_BLOCK_ROWS // SUB_Q_ROWS):
            r = Q_BLOCK_ROWS * jv + SUB_Q_ROWS * sb + np.arange(SUB_Q_ROWS)
            ks = np.clip(r - WIN_R // 2, 0, rows - WIN_R) - Q_BLOCK_ROWS * jv + KEY_ROWS_BEFORE
            start = int(min(ks.min(), n_key_rows - SUB_KEY_ROWS))
            assert start <= ks.min() and ks.max() + WIN_R <= start + SUB_KEY_ROWS
            per_sub.append(start)
        starts.append(tuple(per_sub))
    return tuple(starts)


def _window_bias(rpb_l, rows):
    n_key_rows = KEY_ROWS_BEFORE + Q_BLOCK_ROWS + KEY_ROWS_AFTER
    sub_starts = _sub_window_starts(rows)
    i = np.arange(Q_BLOCK_ROWS)[:, None, None, None]
    c = np.arange(GRID_W)[None, :, None, None]
    kr = np.arange(-KEY_ROWS_BEFORE, Q_BLOCK_ROWS + KEY_ROWS_AFTER)[None, None, :, None]
    kc = np.arange(GRID_W)[None, None, None, :]
    shape = (Q_BLOCK_ROWS, GRID_W, n_key_rows, GRID_W)
    edge = GRID_W - WIN_C
    cols = jnp.pad(rpb_l, ((0, 0), (0, 0), (edge, edge)), mode="edge")
    cols = jnp.pad(cols, ((0, 0), (0, 0), (0, 1)))
    skew = jnp.broadcast_to(cols[:, :, None, :], cols.shape[:2] + (GRID_W, 2 * GRID_W))
    skew = skew.reshape(cols.shape[:2] + (2 * GRID_W * GRID_W,))[:, :, :GRID_W * (2 * GRID_W - 1)]
    by_col = skew.reshape(cols.shape[:2] + (GRID_W, 2 * GRID_W - 1))[..., GRID_W - 1:]
    lo = KEY_ROWS_BEFORE
    hi = Q_BLOCK_ROWS + KEY_ROWS_AFTER - WIN_R
    by_col = jnp.pad(by_col, ((0, 0), (lo, hi), (0, 0), (0, 0)))
    tables = []
    for jv, starts in zip(_block_kinds(rows), sub_starts):
        r = Q_BLOCK_ROWS * jv + i
        ks = np.clip(r - WIN_R // 2, 0, rows - WIN_R)
        krow = Q_BLOCK_ROWS * jv + kr
        c_start = np.clip(c - WIN_C // 2, 0, GRID_W - WIN_C)
        valid = np.broadcast_to((krow >= ks) & (krow < ks + WIN_R) & (krow >= 0) & (krow < rows)
                                & (kc >= c_start) & (kc < c_start + WIN_C), shape)
        per_row, per_row_valid = [], []
        for ii in range(Q_BLOCK_ROWS):
            st = starts[ii // SUB_Q_ROWS]
            first = WIN_R - 1 - ii + st
            per_row.append(by_col[:, first:first + SUB_KEY_ROWS].transpose(0, 2, 1, 3))
            per_row_valid.append(valid[ii, :, st:st + SUB_KEY_ROWS, :])
        values = jnp.stack(per_row, axis=1)
        bias = jnp.where(np.stack(per_row_valid)[None], values, NEG_INF)
        tables.append(bias.reshape(N_HEADS, Q_BLOCK_ROWS * GRID_W, SUB_KEY_ROWS * GRID_W))
    return jnp.stack(tables)


def _nbr_attention(q, kv, kv_ctx, bias, gg):
    b, s, _ = q.shape
    n_ctx = kv_ctx.shape[1]
    tb = Q_BLOCK_ROWS * GRID_W
    half = KEY_ROWS_BEFORE * GRID_W
    nb = s // tb
    assert nb >= 2 and KEY_ROWS_BEFORE == KEY_ROWS_AFTER and tb == 2 * half
    n_loc = tb + 2 * half
    n_win = SUB_KEY_ROWS * GRID_W
    variant = lambda j: jnp.where(j == 0, 0, jnp.where(j == nb - 1, 2, 1))
    return pl.pallas_call(
        functools.partial(_nbr_attn_body, starts=_sub_window_starts(s // GRID_W)),
        grid=(b, nb),
        in_specs=[pl.BlockSpec((1, tb, GROUP_W), lambda bi, j: (bi, j, 0)),
                  pl.BlockSpec((1, half, 2 * GROUP_W), lambda bi, j: (bi, jnp.maximum(2 * j - 1, 0), 0)),
                  pl.BlockSpec((1, tb, 2 * GROUP_W), lambda bi, j: (bi, j, 0)),
                  pl.BlockSpec((1, half, 2 * GROUP_W), lambda bi, j: (bi, jnp.minimum(2 * j + 2, 2 * nb - 1), 0)),
                  pl.BlockSpec((1, n_ctx, 2 * GROUP_W), lambda bi, j: (bi, 0, 0)),
                  pl.BlockSpec((1, N_HEADS, tb, n_win), lambda bi, j: (variant(j), 0, 0, 0)),
                  pl.BlockSpec((1, GROUP_W), lambda bi, j: (0, 0))],
        out_specs=pl.BlockSpec((1, tb, GROUP_W), lambda bi, j: (bi, j, 0)),
        out_shape=jax.ShapeDtypeStruct((b, s, GROUP_W), BF16),
        scratch_shapes=[pltpu.VMEM((n_loc, GROUP_W), BF16),
                        pltpu.VMEM((N_HEADS, n_loc, GROUP_W), BF16),
                        pltpu.VMEM((n_ctx, GROUP_W), BF16),
                        pltpu.VMEM((N_HEADS, n_ctx, GROUP_W), BF16)],
        compiler_params=_params("parallel", "arbitrary"),
        name="nbr_attention",
    )(q, kv, kv, kv, kv_ctx, bias, gg.reshape(1, GROUP_W))


def _ctx_attn_body(q_ref, kx_ref, gg_ref, o_ref, vx_s):
    masks = _head_masks()
    for h in range(N_HEADS):
        vx_s[h] = kx_ref[0, :, GROUP_W:] * masks[h]
    tq = q_ref.shape[1]
    acc = _attend(q_ref[0], masks, keys=[kx_ref[0, :, 0:GROUP_W]], values=[lambda h: vx_s[h]],
                  biases=[None], acc=jnp.zeros((tq, GROUP_W), F32))
    o_ref[0] = _rms(acc, gg_ref[...]).astype(o_ref.dtype)


def _ctx_attention(q, kv_ctx, gg):
    b, n_ctx, _ = q.shape
    return pl.pallas_call(
        _ctx_attn_body,
        grid=(b,),
        in_specs=[pl.BlockSpec((1, n_ctx, GROUP_W), lambda bi: (bi, 0, 0)),
                  pl.BlockSpec((1, n_ctx, 2 * GROUP_W), lambda bi: (bi, 0, 0)),
                  pl.BlockSpec((1, GROUP_W), lambda bi: (0, 0))],
        out_specs=pl.BlockSpec((1, n_ctx, GROUP_W), lambda bi: (bi, 0, 0)),
        out_shape=jax.ShapeDtypeStruct((b, n_ctx, GROUP_W), BF16),
        scratch_shapes=[pltpu.VMEM((N_HEADS, n_ctx, GROUP_W), BF16)],
        compiler_params=_params("parallel"),
        name="ctx_attention",
    )(q, kv_ctx, gg.reshape(1, GROUP_W))


def _merge_body(abc_ref, dn_ref, w_ref, x_ref, mod_ref, g_ref, *rest, with_router):
    if with_router:
        rw_ref, rb_ref, xo_ref, h_ref, info_ref = rest
    else:
        xo_ref, h_ref = rest
    y = (jnp.dot(abc_ref[0], w_ref[0:3 * GROUP_W], preferred_element_type=F32)
         + jnp.dot(dn_ref[0], w_ref[3 * GROUP_W:], preferred_element_type=F32))
    m = mod_ref[0]
    xn = x_ref[0] + m[M_G1:M_G1 + 1] * y
    xo_ref[0] = xn
    h = _rms(xn, g_ref[...]) * (1.0 + m[M_SC2:M_SC2 + 1]) + m[M_SH2:M_SH2 + 1]
    if not with_router:
        h_ref[0] = h.astype(h_ref.dtype)
    else:
        n_sub = h.shape[1] // LANES
        for s in range(n_sub):
            h_ref[0, pl.ds(s, h.shape[0], stride=n_sub), :] = h[:, s * LANES:(s + 1) * LANES]
        logits = jnp.dot(h.astype(BF16), rw_ref[...], preferred_element_type=F32) + rb_ref[...]
        lane = lax.broadcasted_iota(jnp.int32, logits.shape, 1).astype(F32)
        m1 = logits.max(axis=-1, keepdims=True)
        i1 = jnp.where(logits == m1, lane, float(LANES)).min(axis=-1, keepdims=True)
        rest_logits = jnp.where(lane == i1, NEG_INF, logits)
        m2 = rest_logits.max(axis=-1, keepdims=True)
        i2 = jnp.where(rest_logits == m2, lane, float(LANES)).min(axis=-1, keepdims=True)
        e2 = jnp.exp(m2 - m1)
        den = 1.0 + e2
        info = jnp.where(lane == 0, i1, jnp.where(lane == 1, i2,
               jnp.where(lane == 2, 1.0 / den, jnp.where(lane == 3, e2 / den, 0.0))))
        info_ref[0] = info


def _merge(abc, dn, w_out_bf16, x, mod, mod_row, g2, tm, h_dtype, router=None):
    b, s, d = x.shape
    in_specs = [pl.BlockSpec((1, tm, 3 * GROUP_W), lambda bi, i: (bi, i, 0)),
                pl.BlockSpec((1, tm, GROUP_W), lambda bi, i: (bi, i, 0)),
                pl.BlockSpec((4 * GROUP_W, d), lambda bi, i: (0, 0)),
                pl.BlockSpec((1, tm, d), lambda bi, i: (bi, i, 0)),
                pl.BlockSpec((1, 6, d), lambda bi, i: (mod_row(bi), 0, 0)),
                pl.BlockSpec((1, d), lambda bi, i: (0, 0))]
    out_specs = [pl.BlockSpec((1, tm, d), lambda bi, i: (bi, i, 0)),
                 pl.BlockSpec((1, tm, d), lambda bi, i: (bi, i, 0))]
    out_shape = [jax.ShapeDtypeStruct((b, s, d), F32), jax.ShapeDtypeStruct((b, s, d), h_dtype)]
    args = [abc, dn, w_out_bf16, x, mod, g2.reshape(1, d)]
    if router is not None:
        out_specs[1] = pl.BlockSpec((1, tm * d // LANES, LANES), lambda bi, i: (bi, i, 0))
        out_shape[1] = jax.ShapeDtypeStruct((b, s * d // LANES, LANES), h_dtype)
        in_specs += [pl.BlockSpec((d, LANES), lambda bi, i: (0, 0)),
                     pl.BlockSpec((1, LANES), lambda bi, i: (0, 0))]
        out_specs.append(pl.BlockSpec((1, tm, LANES), lambda bi, i: (bi, i, 0)))
        out_shape.append(jax.ShapeDtypeStruct((b, s, LANES), F32))
        args += list(router)
    return pl.pallas_call(
        functools.partial(_merge_body, with_router=router is not None),
        grid=(b, s // tm),
        in_specs=in_specs, out_specs=out_specs, out_shape=out_shape,
        compiler_params=_params("parallel", "parallel"),
        name="merge_router" if router is not None else "merge",
    )(*args)


def _swiglu_chunk(xb, w1_ref, w3_ref, w2_ref, c0, cw):
    a = jnp.dot(xb, w1_ref[:, c0:c0 + cw], preferred_element_type=F32)
    g = jnp.dot(xb, w3_ref[:, c0:c0 + cw], preferred_element_type=F32)
    act = (a * jax.nn.sigmoid(a) * g).astype(BF16)
    return jnp.dot(act, w2_ref[c0:c0 + cw, :], preferred_element_type=F32)


def _hidden_chunks(f, tf):
    return [(c0, min(tf, f - c0)) for c0 in range(0, f, tf)]


def _ffn_body(h_ref, w1_ref, w3_ref, w2_ref, x_ref, mod_ref, o_ref, acc_ref, *, tf):
    h = h_ref[...]
    for n, (c0, cw) in enumerate(_hidden_chunks(w1_ref.shape[1], tf)):
        part = _swiglu_chunk(h, w1_ref, w3_ref, w2_ref, c0, cw)
        if n == 0:
            acc_ref[...] = part
        else:
            acc_ref[...] += part
    o_ref[...] = x_ref[...] + mod_ref[0, M_G2:M_G2 + 1] * acc_ref[...]


def _dense_ffn(h, w1, w3, w2, x, mod, mod_row, tm, tf):
    t, d = h.shape
    f = w1.shape[1]
    resident = pl.Buffered(1)
    return pl.pallas_call(
        functools.partial(_ffn_body, tf=tf),
        grid=(t // tm,),
        in_specs=[pl.BlockSpec((tm, d), lambda i: (i, 0)),
                  pl.BlockSpec((d, f), lambda i: (0, 0), pipeline_mode=resident),
                  pl.BlockSpec((d, f), lambda i: (0, 0), pipeline_mode=resident),
                  pl.BlockSpec((f, d), lambda i: (0, 0), pipeline_mode=resident),
                  pl.BlockSpec((tm, d), lambda i: (i, 0)),
                  pl.BlockSpec((1, 6, d), lambda i: (mod_row(i), 0, 0))],
        out_specs=pl.BlockSpec((tm, d), lambda i: (i, 0)),
        out_shape=jax.ShapeDtypeStruct((t, d), F32),
        scratch_shapes=[pltpu.VMEM((tm, d), F32)],
        compiler_params=_params("parallel"),
        name="dense_ffn",
    )(h, w1, w3, w2, x, mod)


WINDOW_ALIGN = 16


def _route_plan(expert_idx, tm, tt):
    n_assign = expert_idx.size
    e_flat = expert_idx.reshape(-1)
    onehot = (e_flat[:, None] == jnp.arange(N_EXPERTS, dtype=jnp.int32)[None, :]).astype(jnp.int32)
    csum = jnp.cumsum(onehot, axis=0)
    counts = csum[-1]
    rank = jnp.sum((csum - onehot) * onehot, axis=1)
    padded = (counts + tm - 1) // tm * tm
    pad_end = jnp.cumsum(padded)
    pad_start = pad_end - padded
    slot = jnp.sum(onehot * pad_start[None, :], axis=1) + rank
    total = n_assign + N_EXPERTS * tm
    n_blk = total // tm
    slot_tok = jnp.zeros((total,), jnp.int32).at[slot].set(jnp.arange(n_assign, dtype=jnp.int32) // 2)
    n_active = pad_end[-1] // tm
    blk = jnp.arange(n_blk, dtype=jnp.int32)
    blk_exp = jnp.minimum(jnp.searchsorted(pad_end, blk * tm, side="right"), N_EXPERTS - 1).astype(jnp.int32)
    blk_exp = jnp.where(blk < n_active, blk_exp, blk_exp[jnp.maximum(n_active - 1, 0)])

    wa = tt + WINDOW_ALIGN
    n_tiles = n_assign // (2 * tt)
    before = jnp.concatenate([jnp.zeros((1, N_EXPERTS), jnp.int32), csum[2 * tt - 1::2 * tt][:-1]], axis=0)
    first = pad_start[None, :] + before
    win_base = jnp.minimum(first // WINDOW_ALIGN * WINDOW_ALIGN, total - wa)
    win_row0 = jnp.arange(N_EXPERTS, dtype=jnp.int32)[None, :] * wa - win_base
    win_loc = slot + jnp.sum(onehot.reshape(n_tiles, 2 * tt, N_EXPERTS) * win_row0[:, None, :],
                             axis=2).reshape(n_assign)
    return (slot_tok.reshape(n_blk, tm), blk_exp, n_active.astype(jnp.int32).reshape(1),
            win_base.reshape(n_tiles * N_EXPERTS).astype(jnp.int32), win_loc.reshape(-1, 2))


ROW_ISSUE_UNROLL = 8


def _moe_body(bexp_ref, nact_ref, tok_hbm, h_hbm, w1_ref, w3_ref, w2_ref, o_ref,
              tok_s, x_s, xb_s, acc_s, sem_idx, sem_row, *, tm, tf):
    i = pl.program_id(0)
    n_active = nact_ref[0]
    chunks = _hidden_chunks(w1_ref.shape[2], tf)
    rows_per_chunk = -(-tm // len(chunks))

    def idx_copy(step):
        return pltpu.make_async_copy(tok_hbm.at[step], tok_s.at[step % 2], sem_idx.at[step % 2])

    n_sub = h_hbm.shape[1]

    def row_copy(slot, r, tok):
        return pltpu.make_async_copy(h_hbm.at[tok], x_s.at[slot, pl.ds(r * n_sub, n_sub)], sem_row.at[slot])

    @pl.when(i == 0)
    def _():
        idx_copy(0).start()
        idx_copy(0).wait()

        def issue(r, c):
            row_copy(0, r, tok_s[0, r]).start()
            return c

        lax.fori_loop(0, tm, issue, 0, unroll=ROW_ISSUE_UNROLL)
        idx_copy(1).start()

    @pl.when(i <= n_active)
    def _():
        slot = i % 2
        pltpu.make_async_copy(x_s.at[slot], x_s.at[slot], sem_row.at[slot]).wait()
        for s in range(n_sub):
            xb_s[:, s * LANES:(s + 1) * LANES] = x_s[slot, pl.ds(s, tm, stride=n_sub), :].astype(BF16)

    @pl.when(i < n_active)
    def _():
        nxt = (i + 1) % 2
        idx_copy(i + 1).wait()
        xb = xb_s[...]
        for n, (c0, cw) in enumerate(chunks):
            part = _swiglu_chunk(xb, w1_ref.at[0], w3_ref.at[0], w2_ref.at[0], c0, cw)
            if n == 0:
                acc_s[...] = part
            else:
                acc_s[...] += part
            for r in range(n * rows_per_chunk, min((n + 1) * rows_per_chunk, tm)):
                row_copy(nxt, r, tok_s[nxt, r]).start()

        @pl.when(i + 2 <= n_active)
        def _():
            idx_copy(i + 2).start()

    @pl.when(i >= n_active)
    def _():
        acc_s[...] = jnp.zeros(acc_s.shape, acc_s.dtype)

    o_ref[...] = acc_s[...].astype(o_ref.dtype)


def _moe_experts(h, slot_tok, blk_exp, n_active, w1, w3, w2, tm, tf):
    t, n_sub, _ = h.shape
    d = n_sub * LANES
    n_blk = slot_tok.shape[0]
    f = w1.shape[2]
    resident = pl.Buffered(1)
    grid_spec = pltpu.PrefetchScalarGridSpec(
        num_scalar_prefetch=2,
        grid=(n_blk,),
        in_specs=[pl.BlockSpec(memory_space=pl.ANY),
                  pl.BlockSpec(memory_space=pl.ANY),
                  pl.BlockSpec((1, d, f), lambda i, bexp, nact: (bexp[i], 0, 0), pipeline_mode=resident),
                  pl.BlockSpec((1, d, f), lambda i, bexp, nact: (bexp[i], 0, 0), pipeline_mode=resident),
                  pl.BlockSpec((1, f, d), lambda i, bexp, nact: (bexp[i], 0, 0), pipeline_mode=resident)],
        out_specs=pl.BlockSpec((tm, d), lambda i, bexp, nact: (i, 0)),
        scratch_shapes=[pltpu.SMEM((2, tm), jnp.int32),
                        pltpu.VMEM((2, tm * n_sub, LANES), F32),
                        pltpu.VMEM((tm, d), BF16),
                        pltpu.VMEM((tm, d), F32),
                        pltpu.SemaphoreType.DMA((2,)),
                        pltpu.SemaphoreType.DMA((2,))])
    return pl.pallas_call(
        functools.partial(_moe_body, tm=tm, tf=tf),
        grid_spec=grid_spec,
        out_shape=jax.ShapeDtypeStruct((n_blk * tm, d), BF16),
        compiler_params=_params("arbitrary"),
        name="moe_experts",
    )(blk_exp, n_active, slot_tok, h, w1, w3, w2)


def _combine_body(base_ref, y_hbm, info_ref, x_ref, mod_ref, g_ref, o_ref, win_s, sem, *, wa):
    i = pl.program_id(0)
    n = pl.num_programs(0)

    def window_copy(step, e):
        slot = step % 2
        base = pl.multiple_of(base_ref[step * N_EXPERTS + e], WINDOW_ALIGN)
        return pltpu.make_async_copy(y_hbm.at[pl.ds(base, wa)], win_s.at[slot, pl.ds(e * wa, wa)], sem.at[slot])

    @pl.when(i == 0)
    def _():
        for e in range(N_EXPERTS):
            window_copy(0, e).start()

    @pl.when(i + 1 < n)
    def _():
        for e in range(N_EXPERTS):
            window_copy(i + 1, e).start()

    for e in range(N_EXPERTS):
        window_copy(i, e).wait()

    info = info_ref[...]
    win = win_s[i % 2]
    row = lax.broadcasted_iota(jnp.int32, (info.shape[0], win.shape[0]), 1).astype(F32)
    pick0 = jnp.where(row == info[:, 0:1], 1.0, 0.0).astype(BF16)
    pick1 = jnp.where(row == info[:, 1:2], 1.0, 0.0).astype(BF16)
    f = (info[:, 2:3] * jnp.dot(pick0, win, preferred_element_type=F32)
         + info[:, 3:4] * jnp.dot(pick1, win, preferred_element_type=F32))
    xn = x_ref[...] + mod_ref[0, M_G2:M_G2 + 1] * f
    o_ref[...] = _rms(xn, g_ref[...])


def _combine_final(win_base, y, info, x, mod, mod_row, final_g, tt):
    t, d = x.shape
    wa = tt + WINDOW_ALIGN
    grid_spec = pltpu.PrefetchScalarGridSpec(
        num_scalar_prefetch=1,
        grid=(t // tt,),
        in_specs=[pl.BlockSpec(memory_space=pl.ANY),
                  pl.BlockSpec((tt, LANES), lambda i, base: (i, 0)),
                  pl.BlockSpec((tt, d), lambda i, base: (i, 0)),
                  pl.BlockSpec((1, 6, d), lambda i, base: (mod_row(i), 0, 0)),
                  pl.BlockSpec((1, d), lambda i, base: (0, 0))],
        out_specs=pl.BlockSpec((tt, d), lambda i, base: (i, 0)),
        scratch_shapes=[pltpu.VMEM((2, N_EXPERTS * wa, d), y.dtype),
                        pltpu.SemaphoreType.DMA((2,))])
    return pl.pallas_call(
        functools.partial(_combine_body, wa=wa),
        grid_spec=grid_spec,
        out_shape=jax.ShapeDtypeStruct((t, d), F32),
        compiler_params=_params("arbitrary"),
        name="moe_combine_final",
    )(win_base, y, info, x, mod, final_g.reshape(1, d))


class _Tiles(NamedTuple):
    rows: int
    hidden: int
    expert_rows: int
    combine_rows: int


def _choose_tiles(seq):
    return _Tiles(rows=min(512, seq), hidden=512, expert_rows=512, combine_rows=min(128, seq))


def kernel(x, c, ctx, c_ctx, w_mod, b_mod, norm1_g, norm2_g, w_in, conv_a_w, conv_b_w, conv_b_b, conv_ln_g, conv_ln_b, sgu_ln_g, sgu_ln_b, sgu_w, sgu_b, rpb, group_g, w_out, ffn_w1, ffn_w3, ffn_w2, router_w, router_b, moe_w1, moe_w3, moe_w2, final_g):
    bsz, seq, d = x.shape
    n_ctx = ctx.shape[1]
    depth = w_mod.shape[0]
    rows = seq // GRID_W
    assert d == D_MODEL and seq % (Q_BLOCK_ROWS * GRID_W) == 0 and n_ctx % CHUNK == 0 and bsz + 1 <= 8
    assert depth == 2, "layer 0 dense with context, layer 1 (last) MoE without context"

    tiles = _choose_tiles(seq)
    tm = tiles.rows
    lat_row = lambda bi: bi
    ctx_row = lambda bi: bsz

    c8 = jnp.zeros((8, d), F32).at[:bsz].set(c).at[bsz].set(c_ctx)
    mod_all = _modulation(c8, w_mod, b_mod).reshape(depth, 8, 6, d)

    xl, xc = x, ctx
    out = None
    for l in range(depth):
        last = l == depth - 1
        mod = mod_all[l]
        w_in_b = w_in[l].astype(BF16)
        w_out_b = w_out[l].astype(BF16)
        sw_b = sgu_w[l].astype(BF16)
        sbias = jnp.repeat(sgu_b[l].T, HEAD_W, axis=1)
        gg = group_g[l]
        conv_args = (conv_a_w[l], conv_b_w[l], conv_b_b[l], conv_ln_g[l], conv_ln_b[l],
                     sgu_ln_g[l], sgu_ln_b[l], sw_b, sbias, gg[:3 * GROUP_W])
        gg_d = gg[3 * GROUP_W:]

        p, q, kv = _norm_proj(xl, norm1_g[l], mod, lat_row, w_in_b, tm)
        pc, qc, kvc = _norm_proj(xc, norm1_g[l], mod, ctx_row, w_in_b, n_ctx)
        abc = _conv_mixer(p, *conv_args, ts=tm)
        dn = _nbr_attention(q, kv, kvc, _window_bias(rpb[l], rows), gg_d)

        if not last:
            xl, h = _merge(abc, dn, w_out_b, xl, mod, lat_row, norm2_g[l], tm, BF16)
            abc_c = _conv_mixer(pc, *conv_args, ts=n_ctx)
            dn_c = _ctx_attention(qc, kvc, gg_d)
            xc, hc = _merge(abc_c, dn_c, w_out_b, xc, mod, ctx_row, norm2_g[l], n_ctx, BF16)
            w1, w3, w2 = (w[l // 2].astype(BF16) for w in (ffn_w1, ffn_w3, ffn_w2))
            xl = _dense_ffn(h.reshape(bsz * seq, d), w1, w3, w2, xl.reshape(bsz * seq, d), mod,
                            lambda i: i * tm // seq, tm, tiles.hidden).reshape(bsz, seq, d)
            xc = _dense_ffn(hc.reshape(bsz * n_ctx, d), w1, w3, w2, xc.reshape(bsz * n_ctx, d), mod,
                            lambda i: bsz, n_ctx, tiles.hidden).reshape(bsz, n_ctx, d)
        else:
            rw = jnp.zeros((d, LANES), F32).at[:, :N_EXPERTS].set(router_w[l // 2]).astype(BF16)
            rb = jnp.full((1, LANES), NEG_INF, F32).at[0, :N_EXPERTS].set(router_b[l // 2])
            xl, h, info = _merge(abc, dn, w_out_b, xl, mod, lat_row, norm2_g[l], tm, F32, router=(rw, rb))
            t = bsz * seq
            info = info.reshape(t, LANES)
            tt = tiles.combine_rows
            slot_tok, blk_exp, n_active, win_base, win_loc = _route_plan(
                info[:, :2].astype(jnp.int32), tiles.expert_rows, tt)
            w1, w3, w2 = (w[l // 2].astype(BF16) for w in (moe_w1, moe_w3, moe_w2))
            y = _moe_experts(h.reshape(t, d // LANES, LANES), slot_tok, blk_exp, n_active, w1, w3, w2,
                             tiles.expert_rows, tiles.hidden)
            info = jnp.concatenate([win_loc.astype(F32), info[:, 2:]], axis=1)
            out = _combine_final(win_base, y, info, xl.reshape(t, d), mod, lambda i: i * tt // seq,
                                 final_g, tt).reshape(bsz, seq, d)
    return out
```

```python
import functools
from typing import NamedTuple

import numpy as np
import jax
import jax.numpy as jnp
from jax import lax
from jax.experimental import pallas as pl
from jax.experimental.pallas import tpu as pltpu

F32 = jnp.float32
BF16 = jnp.bfloat16

D_MODEL = 1024
GRID_W = 64
EPS = 1e-6
NEG_INF = -1e30
GROUP_W = 256
N_HEADS = 4
HEAD_W = GROUP_W // N_HEADS
CONV_A = 3
CONV_B = 31
CHUNK = 128
WIN_R = 8
WIN_C = 16
OFF_B = 3 * GROUP_W
OFF_C = OFF_B + 2 * GROUP_W
OFF_D = OFF_C + 2 * GROUP_W
OFF_KV = OFF_D + GROUP_W
IN_COLS = OFF_KV + 2 * GROUP_W
N_EXPERTS = 8
LANES = 128
SUBLANES = 8
HALO = 16
Q_BLOCK_ROWS = 8
KEY_ROWS_BEFORE = 4
KEY_ROWS_AFTER = 4
VMEM_LIMIT = 56 * 1024 * 1024

M_SH1, M_SC1, M_G1, M_SH2, M_SC2, M_G2 = range(6)


def _params(*dims):
    return pltpu.CompilerParams(dimension_semantics=dims, vmem_limit_bytes=VMEM_LIMIT)


def _rms(x, g):
    return x * lax.rsqrt(jnp.mean(x * x, axis=-1, keepdims=True) + EPS) * g


def _layer_norm(x, g, b):
    mu = jnp.mean(x, axis=-1, keepdims=True)
    xc = x - mu
    var = jnp.mean(xc * xc, axis=-1, keepdims=True)
    return xc * lax.rsqrt(var + EPS) * g + b


def _mod_body(c_ref, w_ref, b_ref, o_ref):
    s = c_ref[...]
    s = s * jax.nn.sigmoid(s)
    o_ref[0] = jnp.dot(s, w_ref[0], preferred_element_type=F32,
                       precision=lax.Precision.HIGHEST) + b_ref[0]


def _modulation(c8, w_mod, b_mod):
    n_layers, d, six_d = w_mod.shape
    return pl.pallas_call(
        _mod_body,
        grid=(n_layers, six_d // d),
        in_specs=[pl.BlockSpec((8, d), lambda l, j: (0, 0)),
                  pl.BlockSpec((1, d, d), lambda l, j: (l, 0, j)),
                  pl.BlockSpec((1, 1, d), lambda l, j: (l, 0, j))],
        out_specs=pl.BlockSpec((1, 8, d), lambda l, j: (l, 0, j)),
        out_shape=jax.ShapeDtypeStruct((n_layers, 8, six_d), F32),
        compiler_params=_params("arbitrary", "arbitrary"),
        name="modulation",
    )(c8, w_mod, b_mod.reshape(n_layers, 1, six_d))


def _modulated_norm(x, g, m, shift, scale):
    return _rms(x, g) * (1.0 + m[scale:scale + 1]) + m[shift:shift + 1]


def _input_projection(x, g_ref, m, w_ref, p_ref, q_ref, kv_ref):
    hb = _modulated_norm(x, g_ref[...], m, M_SH1, M_SC1).astype(BF16)
    p_ref[0] = jnp.dot(hb, w_ref[:, :OFF_D], preferred_element_type=F32)
    q = jnp.dot(hb, w_ref[:, OFF_D:OFF_KV], preferred_element_type=F32)
    q_ref[0] = (q * (HEAD_W ** -0.5)).astype(BF16)
    kv_ref[0] = jnp.dot(hb, w_ref[:, OFF_KV:], preferred_element_type=F32).astype(BF16)


def _norm_proj_body(x_ref, g_ref, mod_ref, w_ref, p_ref, q_ref, kv_ref):
    _input_projection(x_ref[0], g_ref, mod_ref[0], w_ref, p_ref, q_ref, kv_ref)


def _norm_proj(x, g, mod, mod_row, w_in_bf16, tm):
    b, s, d = x.shape
    return pl.pallas_call(
        _norm_proj_body,
        grid=(b, s // tm),
        in_specs=[pl.BlockSpec((1, tm, d), lambda bi, i: (bi, i, 0)),
                  pl.BlockSpec((1, d), lambda bi, i: (0, 0)),
                  pl.BlockSpec((1, 6, d), lambda bi, i: (mod_row(bi), 0, 0)),
                  pl.BlockSpec((d, IN_COLS), lambda bi, i: (0, 0))],
        out_specs=[pl.BlockSpec((1, tm, OFF_D), lambda bi, i: (bi, i, 0)),
                   pl.BlockSpec((1, tm, GROUP_W), lambda bi, i: (bi, i, 0)),
                   pl.BlockSpec((1, tm, 2 * GROUP_W), lambda bi, i: (bi, i, 0))],
        out_shape=[jax.ShapeDtypeStruct((b, s, OFF_D), F32),
                   jax.ShapeDtypeStruct((b, s, GROUP_W), BF16),
                   jax.ShapeDtypeStruct((b, s, 2 * GROUP_W), BF16)],
        compiler_params=_params("parallel", "parallel"),
        name="norm_proj",
    )(x, g.reshape(1, d), mod, w_in_bf16)


def _conv_inputs(blk):
    za = blk[:, GROUP_W:2 * GROUP_W] * blk[:, 2 * GROUP_W:3 * GROUP_W]
    zb = blk[:, OFF_B:OFF_B + GROUP_W] * jax.nn.sigmoid(blk[:, OFF_B + GROUP_W:OFF_C])
    return za, zb


def _mixer_body(prev_ref, cur_ref, next_ref, wa_ref, wb_ref, bb_ref, blg_ref, blb_ref,
                slg_ref, slb_ref, sw_ref, sbias_ref, gg_ref, o_ref, za_s, zb_s, *, ts, rc):
    i = pl.program_id(1)
    n = pl.num_programs(1)
    has_prev = (i > 0).astype(F32)
    has_next = (i < n - 1).astype(F32)

    pa, pb = _conv_inputs(prev_ref[0])
    za_s[0:HALO] = pa * has_prev
    zb_s[0, 0:HALO] = pb * has_prev
    ca, cb = _conv_inputs(cur_ref[0])
    za_s[HALO:HALO + ts] = ca
    zb_s[0, HALO:HALO + ts] = cb
    na, nb = _conv_inputs(next_ref[0])
    za_s[HALO + ts:] = na * has_next
    zb_s[0, HALO + ts:] = nb * has_next
    n_shifted = ts + 2 * HALO - SUBLANES
    for b in range(1, SUBLANES):
        zb_s[b, 0:n_shifted] = zb_s[0, b:b + n_shifted]

    wa = wa_ref[...]
    wb = wb_ref[...]
    lane_head = lax.broadcasted_iota(jnp.int32, (1, GROUP_W), 1) // HEAD_W
    for r0 in range(0, ts, rc):
        acc = wa[0:1] * za_s[HALO + r0 - 1:HALO + r0 - 1 + rc]
        for j in range(1, CONV_A):
            acc = acc + wa[j:j + 1] * za_s[HALO + r0 - 1 + j:HALO + r0 - 1 + j + rc]
        ya = cur_ref[0, r0:r0 + rc, 0:GROUP_W] * acc
        o_ref[0, r0:r0 + rc, 0:GROUP_W] = _rms(ya, gg_ref[:, 0:GROUP_W]).astype(o_ref.dtype)

        base = HALO + r0 - CONV_B // 2
        acc = None
        for j in range(CONV_B):
            b, a = (base + j) % SUBLANES, (base + j) // SUBLANES * SUBLANES
            term = wb[j:j + 1] * zb_s[b, a:a + rc]
            acc = term if acc is None else acc + term
        yb = _layer_norm(acc + bb_ref[...], blg_ref[...], blb_ref[...])
        yb = yb * jax.nn.sigmoid(yb)
        o_ref[0, r0:r0 + rc, GROUP_W:2 * GROUP_W] = _rms(yb, gg_ref[:, GROUP_W:2 * GROUP_W]).astype(o_ref.dtype)

    for r0 in range(0, ts, CHUNK):
        z = jax.nn.gelu(cur_ref[0, r0:r0 + CHUNK, OFF_C:OFF_D])
        u = z[:, :GROUP_W]
        v = _layer_norm(z[:, GROUP_W:], slg_ref[...], slb_ref[...]).astype(BF16)
        mixed = sbias_ref[...]
        for h in range(N_HEADS):
            vh = v * (lane_head == h).astype(BF16)
            mixed = mixed + jnp.dot(sw_ref[h], vh, preferred_element_type=F32)
        yc = u * mixed
        o_ref[0, r0:r0 + CHUNK, 2 * GROUP_W:3 * GROUP_W] = _rms(yc, gg_ref[:, 2 * GROUP_W:3 * GROUP_W]).astype(o_ref.dtype)


def _conv_mixer(p, wa, wb, bb, blg, blb, slg, slb, sw_bf16, sbias, gg, ts):
    b, s, _ = p.shape
    rc = min(64, ts)
    hb = ts // HALO
    n_halo = s // HALO
    row = lambda a: a.reshape(1, -1)
    const = lambda *shape: pl.BlockSpec(shape, lambda bi, i: (0,) * len(shape))
    return pl.pallas_call(
        functools.partial(_mixer_body, ts=ts, rc=rc),
        grid=(b, s // ts),
        in_specs=[pl.BlockSpec((1, HALO, OFF_D), lambda bi, i: (bi, jnp.maximum(i * hb - 1, 0), 0)),
                  pl.BlockSpec((1, ts, OFF_D), lambda bi, i: (bi, i, 0)),
                  pl.BlockSpec((1, HALO, OFF_D), lambda bi, i: (bi, jnp.minimum((i + 1) * hb, n_halo - 1), 0)),
                  const(CONV_A, GROUP_W), const(CONV_B, GROUP_W), const(1, GROUP_W),
                  const(1, GROUP_W), const(1, GROUP_W), const(1, GROUP_W), const(1, GROUP_W),
                  const(N_HEADS, CHUNK, CHUNK), const(CHUNK, GROUP_W), const(1, 3 * GROUP_W)],
        out_specs=pl.BlockSpec((1, ts, 3 * GROUP_W), lambda bi, i: (bi, i, 0)),
        out_shape=jax.ShapeDtypeStruct((b, s, 3 * GROUP_W), BF16),
        scratch_shapes=[pltpu.VMEM((ts + 2 * HALO, GROUP_W), F32),
                        pltpu.VMEM((SUBLANES, ts + 2 * HALO, GROUP_W), F32)],
        compiler_params=_params("parallel", "parallel"),
        name="conv_mixer",
    )(p, p, p, wa, wb, row(bb), row(blg), row(blb), row(slg), row(slb), sw_bf16, sbias, row(gg))


def _head_masks():
    lane_head = lax.broadcasted_iota(jnp.int32, (1, GROUP_W), 1) // HEAD_W
    return [(lane_head == h).astype(BF16) for h in range(N_HEADS)]


_NT = (((1,), (1,)), ((), ()))


def _attend(q, masks, keys, values, biases, acc):
    for h in range(N_HEADS):
        qh = q * masks[h]
        scores = []
        for k, bias in zip(keys, biases):
            s = lax.dot_general(qh, k, _NT, preferred_element_type=F32)
            scores.append(s if bias is None else s + bias(h))
        m = scores[0].max(axis=-1, keepdims=True)
        for s in scores[1:]:
            m = jnp.maximum(m, s.max(axis=-1, keepdims=True))
        probs = [jnp.exp(s - m) for s in scores]
        denom = probs[0].sum(axis=-1, keepdims=True)
        for p in probs[1:]:
            denom = denom + p.sum(axis=-1, keepdims=True)
        o = jnp.dot(probs[0].astype(BF16), values[0](h), preferred_element_type=F32)
        for p, v in zip(probs[1:], values[1:]):
            o = o + jnp.dot(p.astype(BF16), v(h), preferred_element_type=F32)
        acc = acc + o / denom
    return acc


def _nbr_attn_body(q_ref, kp_ref, kc_ref, kn_ref, kx_ref, bias_ref, gg_ref, o_ref,
                   k_s, v_s, kx_s, vx_s, *, starts):
    masks = _head_masks()
    n_half = kp_ref.shape[1]
    n_cur = kc_ref.shape[1]
    pieces = ((kp_ref, 0, n_half), (kc_ref, n_half, n_cur), (kn_ref, n_half + n_cur, n_half))
    for ref, off, n in pieces:
        k_s[off:off + n] = ref[0, :, 0:GROUP_W]
        for h in range(N_HEADS):
            v_s[h, off:off + n] = ref[0, :, GROUP_W:] * masks[h]
    kx_s[...] = kx_ref[0, :, 0:GROUP_W]
    for h in range(N_HEADS):
        vx_s[h] = kx_ref[0, :, GROUP_W:] * masks[h]

    j = pl.program_id(1)
    nb = pl.num_programs(1)
    n_win = bias_ref.shape[3]
    tq = q_ref.shape[1] // len(starts[0])
    for sb, (first, mid, last) in enumerate(zip(*starts)):
        st = jnp.where(j == 0, first, jnp.where(j == nb - 1, last, mid)) * GRID_W
        st = pl.multiple_of(st, GRID_W)
        r0 = sb * tq
        acc = _attend(
            q_ref[0, r0:r0 + tq, :], masks,
            keys=[k_s[pl.ds(st, n_win)], kx_s[...]],
            values=[lambda h: v_s[h, pl.ds(st, n_win)], lambda h: vx_s[h]],
            biases=[lambda h: bias_ref[0, h, r0:r0 + tq, :], None],
            acc=jnp.zeros((tq, GROUP_W), F32))
        o_ref[0, r0:r0 + tq, :] = _rms(acc, gg_ref[...]).astype(o_ref.dtype)


SUB_Q_ROWS = 4
SUB_KEY_ROWS = 12


def _block_kinds(rows):
    nb = rows // Q_BLOCK_ROWS
    return (0, min(1, nb - 1), nb - 1)


def _sub_window_starts(rows):
    n_key_rows = KEY_ROWS_BEFORE + Q_BLOCK_ROWS + KEY_ROWS_AFTER
    starts = []
    for jv in _block_kinds(rows):
        per_sub = []
        for sb in range(Q_BLOCK_ROWS // SUB_Q_ROWS):
            r = Q_BLOCK_ROWS * jv + SUB_Q_ROWS * sb + np.arange(SUB_Q_ROWS)
            ks = np.clip(r - WIN_R // 2, 0, rows - WIN_R) - Q_BLOCK_ROWS * jv + KEY_ROWS_BEFORE
            start = int(min(ks.min(), n_key_rows - SUB_KEY_ROWS))
            assert start <= ks.min() and ks.max() + WIN_R <= start + SUB_KEY_ROWS
            per_sub.append(start)
        starts.append(tuple(per_sub))
    return tuple(starts)


def _window_bias(rpb_l, rows):
    n_key_rows = KEY_ROWS_BEFORE + Q_BLOCK_ROWS + KEY_ROWS_AFTER
    sub_starts = _sub_window_starts(rows)
    i = np.arange(Q_BLOCK_ROWS)[:, None, None, None]
    c = np.arange(GRID_W)[None, :, None, None]
    kr = np.arange(-KEY_ROWS_BEFORE, Q_BLOCK_ROWS + KEY_ROWS_AFTER)[None, None, :, None]
    kc = np.arange(GRID_W)[None, None, None, :]
    shape = (Q_BLOCK_ROWS, GRID_W, n_key_rows, GRID_W)
    edge = GRID_W - WIN_C
    cols = jnp.pad(rpb_l, ((0, 0), (0, 0), (edge, edge)), mode="edge")
    cols = jnp.pad(cols, ((0, 0), (0, 0), (0, 1)))
    skew = jnp.broadcast_to(cols[:, :, None, :], cols.shape[:2] + (GRID_W, 2 * GRID_W))
    skew = skew.reshape(cols.shape[:2] + (2 * GRID_W * GRID_W,))[:, :, :GRID_W * (2 * GRID_W - 1)]
    by_col = skew.reshape(cols.shape[:2] + (GRID_W, 2 * GRID_W - 1))[..., GRID_W - 1:]
    lo = KEY_ROWS_BEFORE
    hi = Q_BLOCK_ROWS + KEY_ROWS_AFTER - WIN_R
    by_col = jnp.pad(by_col, ((0, 0), (lo, hi), (0, 0), (0, 0)))
    tables = []
    for jv, starts in zip(_block_kinds(rows), sub_starts):
        r = Q_BLOCK_ROWS * jv + i
        ks = np.clip(r - WIN_R // 2, 0, rows - WIN_R)
        krow = Q_BLOCK_ROWS * jv + kr
        c_start = np.clip(c - WIN_C // 2, 0, GRID_W - WIN_C)
        valid = np.broadcast_to((krow >= ks) & (krow < ks + WIN_R) & (krow >= 0) & (krow < rows)
                                & (kc >= c_start) & (kc < c_start + WIN_C), shape)
        per_row, per_row_valid = [], []
        for ii in range(Q_BLOCK_ROWS):
            st = starts[ii // SUB_Q_ROWS]
            first = WIN_R - 1 - ii + st
            per_row.append(by_col[:, first:first + SUB_KEY_ROWS].transpose(0, 2, 1, 3))
            per_row_valid.append(valid[ii, :, st:st + SUB_KEY_ROWS, :])
        values = jnp.stack(per_row, axis=1)
        bias = jnp.where(np.stack(per_row_valid)[None], values, NEG_INF)
        tables.append(bias.reshape(N_HEADS, Q_BLOCK_ROWS * GRID_W, SUB_KEY_ROWS * GRID_W))
    return jnp.stack(tables)


def _nbr_attention(q, kv, kv_ctx, bias, gg):
    b, s, _ = q.shape
    n_ctx = kv_ctx.shape[1]
    tb = Q_BLOCK_ROWS * GRID_W
    half = KEY_ROWS_BEFORE * GRID_W
    nb = s // tb
    assert nb >= 2 and KEY_ROWS_BEFORE == KEY_ROWS_AFTER and tb == 2 * half
    n_loc = tb + 2 * half
    n_win = SUB_KEY_ROWS * GRID_W
    variant = lambda j: jnp.where(j == 0, 0, jnp.where(j == nb - 1, 2, 1))
    return pl.pallas_call(
        functools.partial(_nbr_attn_body, starts=_sub_window_starts(s // GRID_W)),
        grid=(b, nb),
        in_specs=[pl.BlockSpec((1, tb, GROUP_W), lambda bi, j: (bi, j, 0)),
                  pl.BlockSpec((1, half, 2 * GROUP_W), lambda bi, j: (bi, jnp.maximum(2 * j - 1, 0), 0)),
                  pl.BlockSpec((1, tb, 2 * GROUP_W), lambda bi, j: (bi, j, 0)),
                  pl.BlockSpec((1, half, 2 * GROUP_W), lambda bi, j: (bi, jnp.minimum(2 * j + 2, 2 * nb - 1), 0)),
                  pl.BlockSpec((1, n_ctx, 2 * GROUP_W), lambda bi, j: (bi, 0, 0)),
                  pl.BlockSpec((1, N_HEADS, tb, n_win), lambda bi, j: (variant(j), 0, 0, 0)),
                  pl.BlockSpec((1, GROUP_W), lambda bi, j: (0, 0))],
        out_specs=pl.BlockSpec((1, tb, GROUP_W), lambda bi, j: (bi, j, 0)),
        out_shape=jax.ShapeDtypeStruct((b, s, GROUP_W), BF16),
        scratch_shapes=[pltpu.VMEM((n_loc, GROUP_W), BF16),
                        pltpu.VMEM((N_HEADS, n_loc, GROUP_W), BF16),
                        pltpu.VMEM((n_ctx, GROUP_W), BF16),
                        pltpu.VMEM((N_HEADS, n_ctx, GROUP_W), BF16)],
        compiler_params=_params("parallel", "arbitrary"),
        name="nbr_attention",
    )(q, kv, kv, kv, kv_ctx, bias, gg.reshape(1, GROUP_W))


def _ctx_attn_body(q_ref, kx_ref, gg_ref, o_ref, vx_s):
    masks = _head_masks()
    for h in range(N_HEADS):
        vx_s[h] = kx_ref[0, :, GROUP_W:] * masks[h]
    tq = q_ref.shape[1]
    acc = _attend(q_ref[0], masks, keys=[kx_ref[0, :, 0:GROUP_W]], values=[lambda h: vx_s[h]],
                  biases=[None], acc=jnp.zeros((tq, GROUP_W), F32))
    o_ref[0] = _rms(acc, gg_ref[...]).astype(o_ref.dtype)


def _ctx_attention(q, kv_ctx, gg):
    b, n_ctx, _ = q.shape
    return pl.pallas_call(
        _ctx_attn_body,
        grid=(b,),
        in_specs=[pl.BlockSpec((1, n_ctx, GROUP_W), lambda bi: (bi, 0, 0)),
                  pl.BlockSpec((1, n_ctx, 2 * GROUP_W), lambda bi: (bi, 0, 0)),
                  pl.BlockSpec((1, GROUP_W), lambda bi: (0, 0))],
        out_specs=pl.BlockSpec((1, n_ctx, GROUP_W), lambda bi: (bi, 0, 0)),
        out_shape=jax.ShapeDtypeStruct((b, n_ctx, GROUP_W), BF16),
        scratch_shapes=[pltpu.VMEM((N_HEADS, n_ctx, GROUP_W), BF16)],
        compiler_params=_params("parallel"),
        name="ctx_attention",
    )(q, kv_ctx, gg.reshape(1, GROUP_W))


def _mixer_residual(abc_ref, dn_ref, w_ref, x_ref, m):
    y = (jnp.dot(abc_ref[0], w_ref[0:3 * GROUP_W], preferred_element_type=F32)
         + jnp.dot(dn_ref[0], w_ref[3 * GROUP_W:], preferred_element_type=F32))
    return x_ref[0] + m[M_G1:M_G1 + 1] * y


def _merge_router_body(abc_ref, dn_ref, w_ref, x_ref, mod_ref, g_ref, rw_ref, rb_ref, xo_ref, h_ref, info_ref):
    m = mod_ref[0]
    xn = _mixer_residual(abc_ref, dn_ref, w_ref, x_ref, m)
    xo_ref[0] = xn
    h = _modulated_norm(xn, g_ref[...], m, M_SH2, M_SC2)
    n_sub = h.shape[1] // LANES
    for s in range(n_sub):
        h_ref[0, pl.ds(s, h.shape[0], stride=n_sub), :] = h[:, s * LANES:(s + 1) * LANES]
    logits = jnp.dot(h.astype(BF16), rw_ref[...], preferred_element_type=F32) + rb_ref[...]
    lane = lax.broadcasted_iota(jnp.int32, logits.shape, 1).astype(F32)
    m1 = logits.max(axis=-1, keepdims=True)
    i1 = jnp.where(logits == m1, lane, float(LANES)).min(axis=-1, keepdims=True)
    rest_logits = jnp.where(lane == i1, NEG_INF, logits)
    m2 = rest_logits.max(axis=-1, keepdims=True)
    i2 = jnp.where(rest_logits == m2, lane, float(LANES)).min(axis=-1, keepdims=True)
    e2 = jnp.exp(m2 - m1)
    den = 1.0 + e2
    info = jnp.where(lane == 0, i1, jnp.where(lane == 1, i2,
           jnp.where(lane == 2, 1.0 / den, jnp.where(lane == 3, e2 / den, 0.0))))
    info_ref[0] = info


def _merge_router(abc, dn, w_out_bf16, x, mod, mod_row, g2, router_w, router_b, tm):
    b, s, d = x.shape
    const = lambda *shape: pl.BlockSpec(shape, lambda bi, i: (0,) * len(shape))
    return pl.pallas_call(
        _merge_router_body,
        grid=(b, s // tm),
        in_specs=[pl.BlockSpec((1, tm, 3 * GROUP_W), lambda bi, i: (bi, i, 0)),
                  pl.BlockSpec((1, tm, GROUP_W), lambda bi, i: (bi, i, 0)),
                  const(4 * GROUP_W, d),
                  pl.BlockSpec((1, tm, d), lambda bi, i: (bi, i, 0)),
                  pl.BlockSpec((1, 6, d), lambda bi, i: (mod_row(bi), 0, 0)),
                  const(1, d), const(d, LANES), const(1, LANES)],
        out_specs=[pl.BlockSpec((1, tm, d), lambda bi, i: (bi, i, 0)),
                   pl.BlockSpec((1, tm * d // LANES, LANES), lambda bi, i: (bi, i, 0)),
                   pl.BlockSpec((1, tm, LANES), lambda bi, i: (bi, i, 0))],
        out_shape=[jax.ShapeDtypeStruct((b, s, d), F32),
                   jax.ShapeDtypeStruct((b, s * d // LANES, LANES), F32),
                   jax.ShapeDtypeStruct((b, s, LANES), F32)],
        compiler_params=_params("parallel", "parallel"),
        name="merge_router",
    )(abc, dn, w_out_bf16, x, mod, g2.reshape(1, d), router_w, router_b)


def _swiglu_chunk(xb, w1_ref, w3_ref, w2_ref, c0, cw):
    a = jnp.dot(xb, w1_ref[:, c0:c0 + cw], preferred_element_type=F32)
    g = jnp.dot(xb, w3_ref[:, c0:c0 + cw], preferred_element_type=F32)
    act = (a * jax.nn.sigmoid(a) * g).astype(BF16)
    return jnp.dot(act, w2_ref[c0:c0 + cw, :], preferred_element_type=F32)


def _hidden_chunks(f, tf):
    return [(c0, min(tf, f - c0)) for c0 in range(0, f, tf)]


def _transition_body(abc_ref, dn_ref, wo_ref, x_ref, mod_ref, g2_ref, w1_ref, w3_ref, w2_ref,
                     modn_ref, g1n_ref, win_ref, xo_ref, p_ref, q_ref, kv_ref, acc_ref, *, tf):
    m = mod_ref[0]
    xn = _mixer_residual(abc_ref, dn_ref, wo_ref, x_ref, m)
    h = _modulated_norm(xn, g2_ref[...], m, M_SH2, M_SC2).astype(BF16)
    for n, (c0, cw) in enumerate(_hidden_chunks(w1_ref.shape[1], tf)):
        part = _swiglu_chunk(h, w1_ref, w3_ref, w2_ref, c0, cw)
        if n == 0:
            acc_ref[...] = part
        else:
            acc_ref[...] += part
    x_next = xn + m[M_G2:M_G2 + 1] * acc_ref[...]
    xo_ref[0] = x_next
    _input_projection(x_next, g1n_ref, modn_ref[0], win_ref, p_ref, q_ref, kv_ref)


def _layer_transition(abc, dn, w_out_bf16, x, mod, mod_next, mod_row, g2, ffn_w, g1_next, w_in_next, tm, tf):
    b, s, d = x.shape
    w1, w3, w2 = ffn_w
    f = w1.shape[1]
    resident = lambda *shape: pl.BlockSpec(shape, lambda bi, i: (0,) * len(shape), pipeline_mode=pl.Buffered(1))
    rows = lambda width: pl.BlockSpec((1, tm, width), lambda bi, i: (bi, i, 0))
    mod_spec = pl.BlockSpec((1, 6, d), lambda bi, i: (mod_row(bi), 0, 0))
    return pl.pallas_call(
        functools.partial(_transition_body, tf=tf),
        grid=(b, s // tm),
        in_specs=[rows(3 * GROUP_W), rows(GROUP_W), resident(4 * GROUP_W, d), rows(d), mod_spec, resident(1, d),
                  resident(d, f), resident(d, f), resident(f, d),
                  mod_spec, resident(1, d), resident(d, IN_COLS)],
        out_specs=[rows(d), rows(OFF_D), rows(GROUP_W), rows(2 * GROUP_W)],
        out_shape=[jax.ShapeDtypeStruct((b, s, d), F32),
                   jax.ShapeDtypeStruct((b, s, OFF_D), F32),
                   jax.ShapeDtypeStruct((b, s, GROUP_W), BF16),
                   jax.ShapeDtypeStruct((b, s, 2 * GROUP_W), BF16)],
        scratch_shapes=[pltpu.VMEM((tm, d), F32)],
        compiler_params=_params("parallel", "parallel"),
        name="layer_transition",
    )(abc, dn, w_out_bf16, x, mod, g2.reshape(1, d), w1, w3, w2, mod_next, g1_next.reshape(1, d), w_in_next)


WINDOW_ALIGN = 16


def _route_plan(expert_idx, tm, tt):
    n_assign = expert_idx.size
    e_flat = expert_idx.reshape(-1)
    onehot = (e_flat[:, None] == jnp.arange(N_EXPERTS, dtype=jnp.int32)[None, :]).astype(jnp.int32)
    csum = jnp.cumsum(onehot, axis=0)
    counts = csum[-1]
    rank = jnp.sum((csum - onehot) * onehot, axis=1)
    padded = (counts + tm - 1) // tm * tm
    pad_end = jnp.cumsum(padded)
    pad_start = pad_end - padded
    slot = jnp.sum(onehot * pad_start[None, :], axis=1) + rank
    total = n_assign + N_EXPERTS * tm
    n_blk = total // tm
    slot_tok = jnp.zeros((total,), jnp.int32).at[slot].set(jnp.arange(n_assign, dtype=jnp.int32) // 2)
    n_active = pad_end[-1] // tm
    blk = jnp.arange(n_blk, dtype=jnp.int32)
    blk_exp = jnp.minimum(jnp.searchsorted(pad_end, blk * tm, side="right"), N_EXPERTS - 1).astype(jnp.int32)
    blk_exp = jnp.where(blk < n_active, blk_exp, blk_exp[jnp.maximum(n_active - 1, 0)])

    wa = tt + WINDOW_ALIGN
    n_tiles = n_assign // (2 * tt)
    before = jnp.concatenate([jnp.zeros((1, N_EXPERTS), jnp.int32), csum[2 * tt - 1::2 * tt][:-1]], axis=0)
    first = pad_start[None, :] + before
    win_base = jnp.minimum(first // WINDOW_ALIGN * WINDOW_ALIGN, total - wa)
    win_row0 = jnp.arange(N_EXPERTS, dtype=jnp.int32)[None, :] * wa - win_base
    win_loc = slot + jnp.sum(onehot.reshape(n_tiles, 2 * tt, N_EXPERTS) * win_row0[:, None, :],
                             axis=2).reshape(n_assign)
    return (slot_tok.reshape(n_blk, tm), blk_exp, n_active.astype(jnp.int32).reshape(1),
            win_base.reshape(n_tiles * N_EXPERTS).astype(jnp.int32), win_loc.reshape(-1, 2))


ROW_ISSUE_UNROLL = 8


def _moe_body(bexp_ref, nact_ref, tok_hbm, h_hbm, w1_ref, w3_ref, w2_ref, o_ref,
              tok_s, x_s, xb_s, acc_s, sem_idx, sem_row, *, tm, tf):
    i = pl.program_id(0)
    n_active = nact_ref[0]
    chunks = _hidden_chunks(w1_ref.shape[2], tf)
    rows_per_chunk = -(-tm // len(chunks))

    def idx_copy(step):
        return pltpu.make_async_copy(tok_hbm.at[step], tok_s.at[step % 2], sem_idx.at[step % 2])

    n_sub = h_hbm.shape[1]

    def row_copy(slot, r, tok):
        return pltpu.make_async_copy(h_hbm.at[tok], x_s.at[slot, pl.ds(r * n_sub, n_sub)], sem_row.at[slot])

    @pl.when(i == 0)
    def _():
        idx_copy(0).start()
        idx_copy(0).wait()

        def issue(r, c):
            row_copy(0, r, tok_s[0, r]).start()
            return c

        lax.fori_loop(0, tm, issue, 0, unroll=ROW_ISSUE_UNROLL)
        idx_copy(1).start()

    @pl.when(i <= n_active)
    def _():
        slot = i % 2
        pltpu.make_async_copy(x_s.at[slot], x_s.at[slot], sem_row.at[slot]).wait()
        for s in range(n_sub):
            xb_s[:, s * LANES:(s + 1) * LANES] = x_s[slot, pl.ds(s, tm, stride=n_sub), :].astype(BF16)

    @pl.when(i < n_active)
    def _():
        nxt = (i + 1) % 2
        idx_copy(i + 1).wait()
        xb = xb_s[...]
        for n, (c0, cw) in enumerate(chunks):
            part = _swiglu_chunk(xb, w1_ref.at[0], w3_ref.at[0], w2_ref.at[0], c0, cw)
            if n == 0:
                acc_s[...] = part
            else:
                acc_s[...] += part
            for r in range(n * rows_per_chunk, min((n + 1) * rows_per_chunk, tm)):
                row_copy(nxt, r, tok_s[nxt, r]).start()

        @pl.when(i + 2 <= n_active)
        def _():
            idx_copy(i + 2).start()

    @pl.when(i >= n_active)
    def _():
        acc_s[...] = jnp.zeros(acc_s.shape, acc_s.dtype)

    o_ref[...] = acc_s[...].astype(o_ref.dtype)


def _moe_experts(h, slot_tok, blk_exp, n_active, w1, w3, w2, tm, tf):
    t, n_sub, _ = h.shape
    d = n_sub * LANES
    n_blk = slot_tok.shape[0]
    f = w1.shape[2]
    resident = pl.Buffered(1)
    grid_spec = pltpu.PrefetchScalarGridSpec(
        num_scalar_prefetch=2,
        grid=(n_blk,),
        in_specs=[pl.BlockSpec(memory_space=pl.ANY),
                  pl.BlockSpec(memory_space=pl.ANY),
                  pl.BlockSpec((1, d, f), lambda i, bexp, nact: (bexp[i], 0, 0), pipeline_mode=resident),
                  pl.BlockSpec((1, d, f), lambda i, bexp, nact: (bexp[i], 0, 0), pipeline_mode=resident),
                  pl.BlockSpec((1, f, d), lambda i, bexp, nact: (bexp[i], 0, 0), pipeline_mode=resident)],
        out_specs=pl.BlockSpec((tm, d), lambda i, bexp, nact: (i, 0)),
        scratch_shapes=[pltpu.SMEM((2, tm), jnp.int32),
                        pltpu.VMEM((2, tm * n_sub, LANES), F32),
                        pltpu.VMEM((tm, d), BF16),
                        pltpu.VMEM((tm, d), F32),
                        pltpu.SemaphoreType.DMA((2,)),
                        pltpu.SemaphoreType.DMA((2,))])
    return pl.pallas_call(
        functools.partial(_moe_body, tm=tm, tf=tf),
        grid_spec=grid_spec,
        out_shape=jax.ShapeDtypeStruct((n_blk * tm, d), BF16),
        compiler_params=_params("arbitrary"),
        name="moe_experts",
    )(blk_exp, n_active, slot_tok, h, w1, w3, w2)


COMBINE_SLOTS = 3


def _combine_body(base_ref, y_hbm, info_ref, x_ref, mod_ref, g_ref, o_ref, win_s, sem, *, wa):
    i = pl.program_id(0)
    n = pl.num_programs(0)

    def window_copy(step, e):
        slot = step % COMBINE_SLOTS
        base = pl.multiple_of(base_ref[step * N_EXPERTS + e], WINDOW_ALIGN)
        return pltpu.make_async_copy(y_hbm.at[pl.ds(base, wa)], win_s.at[slot, pl.ds(e * wa, wa)], sem.at[slot])

    def start_windows(step):
        for e in range(N_EXPERTS):
            window_copy(step, e).start()

    @pl.when(i == 0)
    def _():
        for step in range(COMBINE_SLOTS - 1):
            @pl.when(step < n)
            def _():
                start_windows(step)

    @pl.when(i + COMBINE_SLOTS - 1 < n)
    def _():
        start_windows(i + COMBINE_SLOTS - 1)

    for e in range(N_EXPERTS):
        window_copy(i, e).wait()

    info = info_ref[...]
    win = win_s[i % COMBINE_SLOTS]
    row = lax.broadcasted_iota(jnp.int32, (info.shape[0], win.shape[0]), 1).astype(F32)
    pick0 = jnp.where(row == info[:, 0:1], 1.0, 0.0).astype(BF16)
    pick1 = jnp.where(row == info[:, 1:2], 1.0, 0.0).astype(BF16)
    f = (info[:, 2:3] * jnp.dot(pick0, win, preferred_element_type=F32)
         + info[:, 3:4] * jnp.dot(pick1, win, preferred_element_type=F32))
    xn = x_ref[...] + mod_ref[0, M_G2:M_G2 + 1] * f
    o_ref[...] = _rms(xn, g_ref[...])


def _combine_final(win_base, y, info, x, mod, mod_row, final_g, tt):
    t, d = x.shape
    wa = tt + WINDOW_ALIGN
    grid_spec = pltpu.PrefetchScalarGridSpec(
        num_scalar_prefetch=1,
        grid=(t // tt,),
        in_specs=[pl.BlockSpec(memory_space=pl.ANY),
                  pl.BlockSpec((tt, LANES), lambda i, base: (i, 0)),
                  pl.BlockSpec((tt, d), lambda i, base: (i, 0)),
                  pl.BlockSpec((1, 6, d), lambda i, base: (mod_row(i), 0, 0)),
                  pl.BlockSpec((1, d), lambda i, base: (0, 0))],
        out_specs=pl.BlockSpec((tt, d), lambda i, base: (i, 0)),
        scratch_shapes=[pltpu.VMEM((COMBINE_SLOTS, N_EXPERTS * wa, d), y.dtype),
                        pltpu.SemaphoreType.DMA((COMBINE_SLOTS,))])
    return pl.pallas_call(
        functools.partial(_combine_body, wa=wa),
        grid_spec=grid_spec,
        out_shape=jax.ShapeDtypeStruct((t, d), F32),
        compiler_params=_params("arbitrary"),
        name="moe_combine_final",
    )(win_base, y, info, x, mod, final_g.reshape(1, d))


class _Tiles(NamedTuple):
    rows: int
    hidden: int
    expert_rows: int
    combine_rows: int


def _choose_tiles(seq):
    return _Tiles(rows=min(512, seq), hidden=512, expert_rows=512, combine_rows=min(128, seq))


def kernel(x, c, ctx, c_ctx, w_mod, b_mod, norm1_g, norm2_g, w_in, conv_a_w, conv_b_w, conv_b_b, conv_ln_g, conv_ln_b, sgu_ln_g, sgu_ln_b, sgu_w, sgu_b, rpb, group_g, w_out, ffn_w1, ffn_w3, ffn_w2, router_w, router_b, moe_w1, moe_w3, moe_w2, final_g):
    bsz, seq, d = x.shape
    n_ctx = ctx.shape[1]
    depth = w_mod.shape[0]
    rows = seq // GRID_W
    assert d == D_MODEL and seq % (Q_BLOCK_ROWS * GRID_W) == 0 and n_ctx % CHUNK == 0 and bsz + 1 <= 8
    assert depth == 2, "layer 0 dense with context, layer 1 (last) MoE without context"

    tiles = _choose_tiles(seq)
    tm = tiles.rows
    lat_row = lambda bi: bi
    ctx_row = lambda bi: bsz

    c8 = jnp.zeros((8, d), F32).at[:bsz].set(c).at[bsz].set(c_ctx)
    mod_all = _modulation(c8, w_mod, b_mod).reshape(depth, 8, 6, d)

    xl, xc = x, ctx
    proj = _norm_proj(xl, norm1_g[0], mod_all[0], lat_row, w_in[0].astype(BF16), tm)
    proj_c = _norm_proj(xc, norm1_g[0], mod_all[0], ctx_row, w_in[0].astype(BF16), n_ctx)
    out = None
    for l in range(depth):
        last = l == depth - 1
        mod = mod_all[l]
        w_out_b = w_out[l].astype(BF16)
        sw_b = sgu_w[l].astype(BF16)
        sbias = jnp.repeat(sgu_b[l].T, HEAD_W, axis=1)
        gg = group_g[l]
        conv_args = (conv_a_w[l], conv_b_w[l], conv_b_b[l], conv_ln_g[l], conv_ln_b[l],
                     sgu_ln_g[l], sgu_ln_b[l], sw_b, sbias, gg[:3 * GROUP_W])
        gg_d = gg[3 * GROUP_W:]

        (p, q, kv), (pc, qc, kvc) = proj, proj_c
        abc = _conv_mixer(p, *conv_args, ts=tm)
        dn = _nbr_attention(q, kv, kvc, _window_bias(rpb[l], rows), gg_d)

        if not last:
            abc_c = _conv_mixer(pc, *conv_args, ts=n_ctx)
            dn_c = _ctx_attention(qc, kvc, gg_d)
            ffn_w = tuple(w[l // 2].astype(BF16) for w in (ffn_w1, ffn_w3, ffn_w2))
            nxt = (mod_all[l + 1], norm1_g[l + 1], w_in[l + 1].astype(BF16))
            xl, *proj = _layer_transition(abc, dn, w_out_b, xl, mod, nxt[0], lat_row, norm2_g[l], ffn_w,
                                          nxt[1], nxt[2], tm, tiles.hidden)
            xc, *proj_c = _layer_transition(abc_c, dn_c, w_out_b, xc, mod, nxt[0], ctx_row, norm2_g[l], ffn_w,
                                            nxt[1], nxt[2], n_ctx, tiles.hidden)
        else:
            rw = jnp.zeros((d, LANES), F32).at[:, :N_EXPERTS].set(router_w[l // 2]).astype(BF16)
            rb = jnp.full((1, LANES), NEG_INF, F32).at[0, :N_EXPERTS].set(router_b[l // 2])
            xl, h, info = _merge_router(abc, dn, w_out_b, xl, mod, lat_row, norm2_g[l], rw, rb, tm)
            t = bsz * seq
            info = info.reshape(t, LANES)
            tt = tiles.combine_rows
            slot_tok, blk_exp, n_active, win_base, win_loc = _route_plan(
                info[:, :2].astype(jnp.int32), tiles.expert_rows, tt)
            w1, w3, w2 = (w[l // 2].astype(BF16) for w in (moe_w1, moe_w3, moe_w2))
            y = _moe_experts(h.reshape(t, d // LANES, LANES), slot_tok, blk_exp, n_active, w1, w3, w2,
                             tiles.expert_rows, tiles.hidden)
            info = jnp.concatenate([win_loc.astype(F32), info[:, 2:]], axis=1)
            out = _combine_final(win_base, y, info, xl.reshape(t, d), mod, lambda i: i * tt // seq,
                                 final_g, tt).reshape(bsz, seq, d)
    return out
```

```python
import functools
from typing import NamedTuple

import numpy as np
import jax
import jax.numpy as jnp
from jax import lax
from jax.experimental import pallas as pl
from jax.experimental.pallas import tpu as pltpu

F32 = jnp.float32
BF16 = jnp.bfloat16

D_MODEL = 1024
GRID_W = 64
EPS = 1e-6
NEG_INF = -1e30
GROUP_W = 256
N_HEADS = 4
HEAD_W = GROUP_W // N_HEADS
CONV_A = 3
CONV_B = 31
CHUNK = 128
WIN_R = 8
WIN_C = 16
OFF_B = 3 * GROUP_W
OFF_C = OFF_B + 2 * GROUP_W
OFF_D = OFF_C + 2 * GROUP_W
OFF_KV = OFF_D + GROUP_W
IN_COLS = OFF_KV + 2 * GROUP_W
N_EXPERTS = 8
LANES = 128
SUBLANES = 8
HALO = 16
Q_BLOCK_ROWS = 8
KEY_ROWS_BEFORE = 4
KEY_ROWS_AFTER = 4
VMEM_LIMIT = 56 * 1024 * 1024

M_SH1, M_SC1, M_G1, M_SH2, M_SC2, M_G2 = range(6)


def _params(*dims):
    return pltpu.CompilerParams(dimension_semantics=dims, vmem_limit_bytes=VMEM_LIMIT)


def _rms(x, g):
    return x * lax.rsqrt(jnp.mean(x * x, axis=-1, keepdims=True) + EPS) * g


def _layer_norm(x, g, b):
    mu = jnp.mean(x, axis=-1, keepdims=True)
    xc = x - mu
    var = jnp.mean(xc * xc, axis=-1, keepdims=True)
    return xc * lax.rsqrt(var + EPS) * g + b


def _mod_body(c_ref, w_ref, b_ref, o_ref):
    s = c_ref[...]
    s = s * jax.nn.sigmoid(s)
    o_ref[0] = jnp.dot(s, w_ref[0], preferred_element_type=F32,
                       precision=lax.Precision.HIGHEST) + b_ref[0]


def _modulation(c8, w_mod, b_mod):
    n_layers, d, six_d = w_mod.shape
    return pl.pallas_call(
        _mod_body,
        grid=(n_layers, six_d // d),
        in_specs=[pl.BlockSpec((8, d), lambda l, j: (0, 0)),
                  pl.BlockSpec((1, d, d), lambda l, j: (l, 0, j)),
                  pl.BlockSpec((1, 1, d), lambda l, j: (l, 0, j))],
        out_specs=pl.BlockSpec((1, 8, d), lambda l, j: (l, 0, j)),
        out_shape=jax.ShapeDtypeStruct((n_layers, 8, six_d), F32),
        compiler_params=_params("arbitrary", "arbitrary"),
        name="modulation",
    )(c8, w_mod, b_mod.reshape(n_layers, 1, six_d))


def _modulated_norm(x, g, m, shift, scale):
    return _rms(x, g) * (1.0 + m[scale:scale + 1]) + m[shift:shift + 1]


def _input_projection(x, g_ref, m, w_ref, p_ref, q_ref, kv_ref):
    hb = _modulated_norm(x, g_ref[...], m, M_SH1, M_SC1).astype(BF16)
    p_ref[0] = jnp.dot(hb, w_ref[:, :OFF_D], preferred_element_type=F32)
    q = jnp.dot(hb, w_ref[:, OFF_D:OFF_KV], preferred_element_type=F32)
    q_ref[0] = (q * (HEAD_W ** -0.5)).astype(BF16)
    kv_ref[0] = jnp.dot(hb, w_ref[:, OFF_KV:], preferred_element_type=F32).astype(BF16)


def _norm_proj_body(x_ref, g_ref, mod_ref, w_ref, p_ref, q_ref, kv_ref):
    _input_projection(x_ref[0], g_ref, mod_ref[0], w_ref, p_ref, q_ref, kv_ref)


def _norm_proj(x, g, mod, mod_row, w_in_bf16, tm):
    b, s, d = x.shape
    return pl.pallas_call(
        _norm_proj_body,
        grid=(b, s // tm),
        in_specs=[pl.BlockSpec((1, tm, d), lambda bi, i: (bi, i, 0)),
                  pl.BlockSpec((1, d), lambda bi, i: (0, 0)),
                  pl.BlockSpec((1, 6, d), lambda bi, i: (mod_row(bi), 0, 0)),
                  pl.BlockSpec((d, IN_COLS), lambda bi, i: (0, 0))],
        out_specs=[pl.BlockSpec((1, tm, OFF_D), lambda bi, i: (bi, i, 0)),
                   pl.BlockSpec((1, tm, GROUP_W), lambda bi, i: (bi, i, 0)),
                   pl.BlockSpec((1, tm, 2 * GROUP_W), lambda bi, i: (bi, i, 0))],
        out_shape=[jax.ShapeDtypeStruct((b, s, OFF_D), F32),
                   jax.ShapeDtypeStruct((b, s, GROUP_W), BF16),
                   jax.ShapeDtypeStruct((b, s, 2 * GROUP_W), BF16)],
        compiler_params=_params("parallel", "parallel"),
        name="norm_proj",
    )(x, g.reshape(1, d), mod, w_in_bf16)


def _conv_inputs(blk):
    za = blk[:, GROUP_W:2 * GROUP_W] * blk[:, 2 * GROUP_W:3 * GROUP_W]
    zb = blk[:, OFF_B:OFF_B + GROUP_W] * jax.nn.sigmoid(blk[:, OFF_B + GROUP_W:OFF_C])
    return za, zb


def _mixer_body(prev_ref, cur_ref, next_ref, wa_ref, wb_ref, bb_ref, blg_ref, blb_ref,
                slg_ref, slb_ref, sw_ref, sbias_ref, gg_ref, o_ref, za_s, zb_s, *, ts, rc):
    i = pl.program_id(1)
    n = pl.num_programs(1)
    has_prev = (i > 0).astype(F32)
    has_next = (i < n - 1).astype(F32)

    pa, pb = _conv_inputs(prev_ref[0])
    za_s[0:HALO] = pa * has_prev
    zb_s[0, 0:HALO] = pb * has_prev
    ca, cb = _conv_inputs(cur_ref[0])
    za_s[HALO:HALO + ts] = ca
    zb_s[0, HALO:HALO + ts] = cb
    na, nb = _conv_inputs(next_ref[0])
    za_s[HALO + ts:] = na * has_next
    zb_s[0, HALO + ts:] = nb * has_next
    n_shifted = ts + 2 * HALO - SUBLANES
    for b in range(1, SUBLANES):
        zb_s[b, 0:n_shifted] = zb_s[0, b:b + n_shifted]

    wa = wa_ref[...]
    wb = wb_ref[...]
    lane_head = lax.broadcasted_iota(jnp.int32, (1, GROUP_W), 1) // HEAD_W
    for r0 in range(0, ts, rc):
        acc = wa[0:1] * za_s[HALO + r0 - 1:HALO + r0 - 1 + rc]
        for j in range(1, CONV_A):
            acc = acc + wa[j:j + 1] * za_s[HALO + r0 - 1 + j:HALO + r0 - 1 + j + rc]
        ya = cur_ref[0, r0:r0 + rc, 0:GROUP_W] * acc
        o_ref[0, r0:r0 + rc, 0:GROUP_W] = _rms(ya, gg_ref[:, 0:GROUP_W]).astype(o_ref.dtype)

        base = HALO + r0 - CONV_B // 2
        acc = None
        for j in range(CONV_B):
            b, a = (base + j) % SUBLANES, (base + j) // SUBLANES * SUBLANES
            term = wb[j:j + 1] * zb_s[b, a:a + rc]
            acc = term if acc is None else acc + term
        yb = _layer_norm(acc + bb_ref[...], blg_ref[...], blb_ref[...])
        yb = yb * jax.nn.sigmoid(yb)
        o_ref[0, r0:r0 + rc, GROUP_W:2 * GROUP_W] = _rms(yb, gg_ref[:, GROUP_W:2 * GROUP_W]).astype(o_ref.dtype)

    for r0 in range(0, ts, CHUNK):
        z = jax.nn.gelu(cur_ref[0, r0:r0 + CHUNK, OFF_C:OFF_D])
        u = z[:, :GROUP_W]
        v = _layer_norm(z[:, GROUP_W:], slg_ref[...], slb_ref[...]).astype(BF16)
        mixed = sbias_ref[...]
        for h in range(N_HEADS):
            vh = v * (lane_head == h).astype(BF16)
            mixed = mixed + jnp.dot(sw_ref[h], vh, preferred_element_type=F32)
        yc = u * mixed
        o_ref[0, r0:r0 + CHUNK, 2 * GROUP_W:3 * GROUP_W] = _rms(yc, gg_ref[:, 2 * GROUP_W:3 * GROUP_W]).astype(o_ref.dtype)


def _conv_mixer(p, wa, wb, bb, blg, blb, slg, slb, sw_bf16, sbias, gg, ts):
    b, s, _ = p.shape
    rc = min(64, ts)
    hb = ts // HALO
    n_halo = s // HALO
    row = lambda a: a.reshape(1, -1)
    const = lambda *shape: pl.BlockSpec(shape, lambda bi, i: (0,) * len(shape))
    return pl.pallas_call(
        functools.partial(_mixer_body, ts=ts, rc=rc),
        grid=(b, s // ts),
        in_specs=[pl.BlockSpec((1, HALO, OFF_D), lambda bi, i: (bi, jnp.maximum(i * hb - 1, 0), 0)),
                  pl.BlockSpec((1, ts, OFF_D), lambda bi, i: (bi, i, 0)),
                  pl.BlockSpec((1, HALO, OFF_D), lambda bi, i: (bi, jnp.minimum((i + 1) * hb, n_halo - 1), 0)),
                  const(CONV_A, GROUP_W), const(CONV_B, GROUP_W), const(1, GROUP_W),
                  const(1, GROUP_W), const(1, GROUP_W), const(1, GROUP_W), const(1, GROUP_W),
                  const(N_HEADS, CHUNK, CHUNK), const(CHUNK, GROUP_W), const(1, 3 * GROUP_W)],
        out_specs=pl.BlockSpec((1, ts, 3 * GROUP_W), lambda bi, i: (bi, i, 0)),
        out_shape=jax.ShapeDtypeStruct((b, s, 3 * GROUP_W), BF16),
        scratch_shapes=[pltpu.VMEM((ts + 2 * HALO, GROUP_W), F32),
                        pltpu.VMEM((SUBLANES, ts + 2 * HALO, GROUP_W), F32)],
        compiler_params=_params("parallel", "parallel"),
        name="conv_mixer",
    )(p, p, p, wa, wb, row(bb), row(blg), row(blb), row(slg), row(slb), sw_bf16, sbias, row(gg))


def _head_masks():
    lane_head = lax.broadcasted_iota(jnp.int32, (1, GROUP_W), 1) // HEAD_W
    return [(lane_head == h).astype(BF16) for h in range(N_HEADS)]


_NT = (((1,), (1,)), ((), ()))


def _attend(q, masks, keys, values, biases, acc):
    for h in range(N_HEADS):
        qh = q * masks[h]
        scores = []
        for k, bias in zip(keys, biases):
            s = lax.dot_general(qh, k, _NT, preferred_element_type=F32)
            scores.append(s if bias is None else s + bias(h))
        m = scores[0].max(axis=-1, keepdims=True)
        for s in scores[1:]:
            m = jnp.maximum(m, s.max(axis=-1, keepdims=True))
        probs = [jnp.exp(s - m) for s in scores]
        denom = probs[0].sum(axis=-1, keepdims=True)
        for p in probs[1:]:
            denom = denom + p.sum(axis=-1, keepdims=True)
        o = jnp.dot(probs[0].astype(BF16), values[0](h), preferred_element_type=F32)
        for p, v in zip(probs[1:], values[1:]):
            o = o + jnp.dot(p.astype(BF16), v(h), preferred_element_type=F32)
        acc = acc + o / denom
    return acc


def _nbr_attn_body(q_ref, kp_ref, kc_ref, kn_ref, kx_ref, bias_ref, gg_ref, o_ref,
                   k_s, v_s, kx_s, vx_s, *, starts):
    masks = _head_masks()
    n_half = kp_ref.shape[1]
    n_cur = kc_ref.shape[1]
    pieces = ((kp_ref, 0, n_half), (kc_ref, n_half, n_cur), (kn_ref, n_half + n_cur, n_half))
    for ref, off, n in pieces:
        k_s[off:off + n] = ref[0, :, 0:GROUP_W]
        for h in range(N_HEADS):
            v_s[h, off:off + n] = ref[0, :, GROUP_W:] * masks[h]
    kx_s[...] = kx_ref[0, :, 0:GROUP_W]
    for h in range(N_HEADS):
        vx_s[h] = kx_ref[0, :, GROUP_W:] * masks[h]

    j = pl.program_id(1)
    nb = pl.num_programs(1)
    n_win = bias_ref.shape[3]
    tq = q_ref.shape[1] // len(starts[0])
    for sb, (first, mid, last) in enumerate(zip(*starts)):
        st = jnp.where(j == 0, first, jnp.where(j == nb - 1, last, mid)) * GRID_W
        st = pl.multiple_of(st, GRID_W)
        r0 = sb * tq
        acc = _attend(
            q_ref[0, r0:r0 + tq, :], masks,
            keys=[k_s[pl.ds(st, n_win)], kx_s[...]],
            values=[lambda h: v_s[h, pl.ds(st, n_win)], lambda h: vx_s[h]],
            biases=[lambda h: bias_ref[0, h, r0:r0 + tq, :], None],
            acc=jnp.zeros((tq, GROUP_W), F32))
        o_ref[0, r0:r0 + tq, :] = _rms(acc, gg_ref[...]).astype(o_ref.dtype)


SUB_Q_ROWS = 4
SUB_KEY_ROWS = 12


def _block_kinds(rows):
    nb = rows // Q_BLOCK_ROWS
    return (0, min(1, nb - 1), nb - 1)


def _sub_window_starts(rows):
    n_key_rows = KEY_ROWS_BEFORE + Q_BLOCK_ROWS + KEY_ROWS_AFTER
    starts = []
    for jv in _block_kinds(rows):
        per_sub = []
        for sb in range(Q_BLOCK_ROWS // SUB_Q_ROWS):
            r = Q_BLOCK_ROWS * jv + SUB_Q_ROWS * sb + np.arange(SUB_Q_ROWS)
            ks = np.clip(r - WIN_R // 2, 0, rows - WIN_R) - Q_BLOCK_ROWS * jv + KEY_ROWS_BEFORE
            start = int(min(ks.min(), n_key_rows - SUB_KEY_ROWS))
            assert start <= ks.min() and ks.max() + WIN_R <= start + SUB_KEY_ROWS
            per_sub.append(start)
        starts.append(tuple(per_sub))
    return tuple(starts)


def _window_bias(rpb_l, rows):
    n_key_rows = KEY_ROWS_BEFORE + Q_BLOCK_ROWS + KEY_ROWS_AFTER
    sub_starts = _sub_window_starts(rows)
    i = np.arange(Q_BLOCK_ROWS)[:, None, None, None]
    c = np.arange(GRID_W)[None, :, None, None]
    kr = np.arange(-KEY_ROWS_BEFORE, Q_BLOCK_ROWS + KEY_ROWS_AFTER)[None, None, :, None]
    kc = np.arange(GRID_W)[None, None, None, :]
    shape = (Q_BLOCK_ROWS, GRID_W, n_key_rows, GRID_W)
    edge = GRID_W - WIN_C
    cols = jnp.pad(rpb_l, ((0, 0), (0, 0), (edge, edge)), mode="edge")
    cols = jnp.pad(cols, ((0, 0), (0, 0), (0, 1)))
    skew = jnp.broadcast_to(cols[:, :, None, :], cols.shape[:2] + (GRID_W, 2 * GRID_W))
    skew = skew.reshape(cols.shape[:2] + (2 * GRID_W * GRID_W,))[:, :, :GRID_W * (2 * GRID_W - 1)]
    by_col = skew.reshape(cols.shape[:2] + (GRID_W, 2 * GRID_W - 1))[..., GRID_W - 1:]
    lo = KEY_ROWS_BEFORE
    hi = Q_BLOCK_ROWS + KEY_ROWS_AFTER - WIN_R
    by_col = jnp.pad(by_col, ((0, 0), (lo, hi), (0, 0), (0, 0)))
    tables = []
    for jv, starts in zip(_block_kinds(rows), sub_starts):
        r = Q_BLOCK_ROWS * jv + i
        ks = np.clip(r - WIN_R // 2, 0, rows - WIN_R)
        krow = Q_BLOCK_ROWS * jv + kr
        c_start = np.clip(c - WIN_C // 2, 0, GRID_W - WIN_C)
        valid = np.broadcast_to((krow >= ks) & (krow < ks + WIN_R) & (krow >= 0) & (krow < rows)
                                & (kc >= c_start) & (kc < c_start + WIN_C), shape)
        per_row, per_row_valid = [], []
        for ii in range(Q_BLOCK_ROWS):
            st = starts[ii // SUB_Q_ROWS]
            first = WIN_R - 1 - ii + st
            per_row.append(by_col[:, first:first + SUB_KEY_ROWS].transpose(0, 2, 1, 3))
            per_row_valid.append(valid[ii, :, st:st + SUB_KEY_ROWS, :])
        values = jnp.stack(per_row, axis=1)
        bias = jnp.where(np.stack(per_row_valid)[None], values, NEG_INF)
        tables.append(bias.reshape(N_HEADS, Q_BLOCK_ROWS * GRID_W, SUB_KEY_ROWS * GRID_W))
    return jnp.stack(tables)


def _nbr_attention(q, kv, kv_ctx, bias, gg):
    b, s, _ = q.shape
    n_ctx = kv_ctx.shape[1]
    tb = Q_BLOCK_ROWS * GRID_W
    half = KEY_ROWS_BEFORE * GRID_W
    nb = s // tb
    assert nb >= 2 and KEY_ROWS_BEFORE == KEY_ROWS_AFTER and tb == 2 * half
    n_loc = tb + 2 * half
    n_win = SUB_KEY_ROWS * GRID_W
    variant = lambda j: jnp.where(j == 0, 0, jnp.where(j == nb - 1, 2, 1))
    return pl.pallas_call(
        functools.partial(_nbr_attn_body, starts=_sub_window_starts(s // GRID_W)),
        grid=(b, nb),
        in_specs=[pl.BlockSpec((1, tb, GROUP_W), lambda bi, j: (bi, j, 0)),
                  pl.BlockSpec((1, half, 2 * GROUP_W), lambda bi, j: (bi, jnp.maximum(2 * j - 1, 0), 0)),
                  pl.BlockSpec((1, tb, 2 * GROUP_W), lambda bi, j: (bi, j, 0)),
                  pl.BlockSpec((1, half, 2 * GROUP_W), lambda bi, j: (bi, jnp.minimum(2 * j + 2, 2 * nb - 1), 0)),
                  pl.BlockSpec((1, n_ctx, 2 * GROUP_W), lambda bi, j: (bi, 0, 0)),
                  pl.BlockSpec((1, N_HEADS, tb, n_win), lambda bi, j: (variant(j), 0, 0, 0)),
                  pl.BlockSpec((1, GROUP_W), lambda bi, j: (0, 0))],
        out_specs=pl.BlockSpec((1, tb, GROUP_W), lambda bi, j: (bi, j, 0)),
        out_shape=jax.ShapeDtypeStruct((b, s, GROUP_W), BF16),
        scratch_shapes=[pltpu.VMEM((n_loc, GROUP_W), BF16),
                        pltpu.VMEM((N_HEADS, n_loc, GROUP_W), BF16),
                        pltpu.VMEM((n_ctx, GROUP_W), BF16),
                        pltpu.VMEM((N_HEADS, n_ctx, GROUP_W), BF16)],
        compiler_params=_params("parallel", "arbitrary"),
        name="nbr_attention",
    )(q, kv, kv, kv, kv_ctx, bias, gg.reshape(1, GROUP_W))


def _ctx_attn_body(q_ref, kx_ref, gg_ref, o_ref, vx_s):
    masks = _head_masks()
    for h in range(N_HEADS):
        vx_s[h] = kx_ref[0, :, GROUP_W:] * masks[h]
    tq = q_ref.shape[1]
    acc = _attend(q_ref[0], masks, keys=[kx_ref[0, :, 0:GROUP_W]], values=[lambda h: vx_s[h]],
                  biases=[None], acc=jnp.zeros((tq, GROUP_W), F32))
    o_ref[0] = _rms(acc, gg_ref[...]).astype(o_ref.dtype)


def _ctx_attention(q, kv_ctx, gg):
    b, n_ctx, _ = q.shape
    return pl.pallas_call(
        _ctx_attn_body,
        grid=(b,),
        in_specs=[pl.BlockSpec((1, n_ctx, GROUP_W), lambda bi: (bi, 0, 0)),
                  pl.BlockSpec((1, n_ctx, 2 * GROUP_W), lambda bi: (bi, 0, 0)),
                  pl.BlockSpec((1, GROUP_W), lambda bi: (0, 0))],
        out_specs=pl.BlockSpec((1, n_ctx, GROUP_W), lambda bi: (bi, 0, 0)),
        out_shape=jax.ShapeDtypeStruct((b, n_ctx, GROUP_W), BF16),
        scratch_shapes=[pltpu.VMEM((N_HEADS, n_ctx, GROUP_W), BF16)],
        compiler_params=_params("parallel"),
        name="ctx_attention",
    )(q, kv_ctx, gg.reshape(1, GROUP_W))


def _mixer_residual(abc_ref, dn_ref, w_ref, x_ref, m):
    y = (jnp.dot(abc_ref[0], w_ref[0:3 * GROUP_W], preferred_element_type=F32)
         + jnp.dot(dn_ref[0], w_ref[3 * GROUP_W:], preferred_element_type=F32))
    return x_ref[0] + m[M_G1:M_G1 + 1] * y


def _merge_router_body(abc_ref, dn_ref, w_ref, x_ref, mod_ref, g_ref, rw_ref, rb_ref, xo_ref, h_ref, info_ref):
    m = mod_ref[0]
    xn = _mixer_residual(abc_ref, dn_ref, w_ref, x_ref, m)
    xo_ref[0] = xn
    h = _modulated_norm(xn, g_ref[...], m, M_SH2, M_SC2)
    n_sub = h.shape[1] // LANES
    for s in range(n_sub):
        h_ref[0, pl.ds(s, h.shape[0], stride=n_sub), :] = h[:, s * LANES:(s + 1) * LANES]
    logits = jnp.dot(h.astype(BF16), rw_ref[...], preferred_element_type=F32) + rb_ref[...]
    lane = lax.broadcasted_iota(jnp.int32, logits.shape, 1).astype(F32)
    m1 = logits.max(axis=-1, keepdims=True)
    i1 = jnp.where(logits == m1, lane, float(LANES)).min(axis=-1, keepdims=True)
    rest_logits = jnp.where(lane == i1, NEG_INF, logits)
    m2 = rest_logits.max(axis=-1, keepdims=True)
    i2 = jnp.where(rest_logits == m2, lane, float(LANES)).min(axis=-1, keepdims=True)
    e2 = jnp.exp(m2 - m1)
    den = 1.0 + e2
    info = jnp.where(lane == 0, i1, jnp.where(lane == 1, i2,
           jnp.where(lane == 2, 1.0 / den, jnp.where(lane == 3, e2 / den, 0.0))))
    info_ref[0] = info


def _merge_router(abc, dn, w_out_bf16, x, mod, mod_row, g2, router_w, router_b, tm):
    b, s, d = x.shape
    const = lambda *shape: pl.BlockSpec(shape, lambda bi, i: (0,) * len(shape))
    return pl.pallas_call(
        _merge_router_body,
        grid=(b, s // tm),
        in_specs=[pl.BlockSpec((1, tm, 3 * GROUP_W), lambda bi, i: (bi, i, 0)),
                  pl.BlockSpec((1, tm, GROUP_W), lambda bi, i: (bi, i, 0)),
                  const(4 * GROUP_W, d),
                  pl.BlockSpec((1, tm, d), lambda bi, i: (bi, i, 0)),
                  pl.BlockSpec((1, 6, d), lambda bi, i: (mod_row(bi), 0, 0)),
                  const(1, d), const(d, LANES), const(1, LANES)],
        out_specs=[pl.BlockSpec((1, tm, d), lambda bi, i: (bi, i, 0)),
                   pl.BlockSpec((1, tm * d // LANES, LANES), lambda bi, i: (bi, i, 0)),
                   pl.BlockSpec((1, tm, LANES), lambda bi, i: (bi, i, 0))],
        out_shape=[jax.ShapeDtypeStruct((b, s, d), F32),
                   jax.ShapeDtypeStruct((b, s * d // LANES, LANES), F32),
                   jax.ShapeDtypeStruct((b, s, LANES), F32)],
        compiler_params=_params("parallel", "parallel"),
        name="merge_router",
    )(abc, dn, w_out_bf16, x, mod, g2.reshape(1, d), router_w, router_b)


def _swiglu_chunk(xb, w1_ref, w3_ref, w2_ref, c0, cw):
    a = jnp.dot(xb, w1_ref[:, c0:c0 + cw], preferred_element_type=F32)
    g = jnp.dot(xb, w3_ref[:, c0:c0 + cw], preferred_element_type=F32)
    act = (a * jax.nn.sigmoid(a) * g).astype(BF16)
    return jnp.dot(act, w2_ref[c0:c0 + cw, :], preferred_element_type=F32)


def _hidden_chunks(f, tf):
    return [(c0, min(tf, f - c0)) for c0 in range(0, f, tf)]


def _transition_body(abc_ref, dn_ref, wo_ref, x_ref, mod_ref, g2_ref, w1_ref, w3_ref, w2_ref,
                     modn_ref, g1n_ref, win_ref, xo_ref, p_ref, q_ref, kv_ref, acc_ref, *, tf):
    m = mod_ref[0]
    xn = _mixer_residual(abc_ref, dn_ref, wo_ref, x_ref, m)
    h = _modulated_norm(xn, g2_ref[...], m, M_SH2, M_SC2).astype(BF16)
    for n, (c0, cw) in enumerate(_hidden_chunks(w1_ref.shape[1], tf)):
        part = _swiglu_chunk(h, w1_ref, w3_ref, w2_ref, c0, cw)
        if n == 0:
            acc_ref[...] = part
        else:
            acc_ref[...] += part
    x_next = xn + m[M_G2:M_G2 + 1] * acc_ref[...]
    xo_ref[0] = x_next
    _input_projection(x_next, g1n_ref, modn_ref[0], win_ref, p_ref, q_ref, kv_ref)


def _layer_transition(abc, dn, w_out_bf16, x, mod, mod_next, mod_row, g2, ffn_w, g1_next, w_in_next, tm, tf):
    b, s, d = x.shape
    w1, w3, w2 = ffn_w
    f = w1.shape[1]
    resident = lambda *shape: pl.BlockSpec(shape, lambda bi, i: (0,) * len(shape), pipeline_mode=pl.Buffered(1))
    rows = lambda width: pl.BlockSpec((1, tm, width), lambda bi, i: (bi, i, 0))
    mod_spec = pl.BlockSpec((1, 6, d), lambda bi, i: (mod_row(bi), 0, 0))
    return pl.pallas_call(
        functools.partial(_transition_body, tf=tf),
        grid=(b, s // tm),
        in_specs=[rows(3 * GROUP_W), rows(GROUP_W), resident(4 * GROUP_W, d), rows(d), mod_spec, resident(1, d),
                  resident(d, f), resident(d, f), resident(f, d),
                  mod_spec, resident(1, d), resident(d, IN_COLS)],
        out_specs=[rows(d), rows(OFF_D), rows(GROUP_W), rows(2 * GROUP_W)],
        out_shape=[jax.ShapeDtypeStruct((b, s, d), F32),
                   jax.ShapeDtypeStruct((b, s, OFF_D), F32),
                   jax.ShapeDtypeStruct((b, s, GROUP_W), BF16),
                   jax.ShapeDtypeStruct((b, s, 2 * GROUP_W), BF16)],
        scratch_shapes=[pltpu.VMEM((tm, d), F32)],
        compiler_params=_params("parallel", "parallel"),
        name="layer_transition",
    )(abc, dn, w_out_bf16, x, mod, g2.reshape(1, d), w1, w3, w2, mod_next, g1_next.reshape(1, d), w_in_next)


WINDOW_ALIGN = 16
SLOT_CHUNK = 8192


def _slot_tokens_body(slot_hbm, zeros_hbm, o_hbm, slot_s, out_s, sem_in, sem_out):
    k = pl.program_id(0)
    chunk = slot_s.shape[0]
    chunk_copy = pltpu.make_async_copy(slot_hbm.at[pl.ds(pl.multiple_of(k * chunk, chunk), chunk)], slot_s, sem_in)
    chunk_copy.start()

    @pl.when(k == 0)
    def _():
        fill = pltpu.make_async_copy(zeros_hbm, out_s, sem_out)
        fill.start()
        fill.wait()

    chunk_copy.wait()
    base = k * chunk

    def place(a, c):
        out_s[slot_s[a]] = (base + a) // 2
        return c

    lax.fori_loop(0, chunk, place, 0, unroll=8)

    @pl.when(k == pl.num_programs(0) - 1)
    def _():
        out_copy = pltpu.make_async_copy(out_s, o_hbm, sem_out)
        out_copy.start()
        out_copy.wait()


def _slot_tokens(slot, total):
    n_assign = slot.shape[0]
    chunk = min(SLOT_CHUNK, n_assign)
    assert n_assign % chunk == 0
    any_space = pl.BlockSpec(memory_space=pl.ANY)
    return pl.pallas_call(
        _slot_tokens_body,
        grid=(n_assign // chunk,),
        in_specs=[any_space, any_space],
        out_specs=any_space,
        out_shape=jax.ShapeDtypeStruct((total,), jnp.int32),
        scratch_shapes=[pltpu.SMEM((chunk,), jnp.int32),
                        pltpu.SMEM((total,), jnp.int32),
                        pltpu.SemaphoreType.DMA,
                        pltpu.SemaphoreType.DMA],
        compiler_params=_params("arbitrary"),
        name="slot_tokens",
    )(slot, jnp.zeros((total,), jnp.int32))


def _route_plan(expert_idx, tm, tt):
    n_assign = expert_idx.size
    e_flat = expert_idx.reshape(-1)
    onehot = (e_flat[:, None] == jnp.arange(N_EXPERTS, dtype=jnp.int32)[None, :]).astype(jnp.int32)
    csum = jnp.cumsum(onehot, axis=0)
    counts = csum[-1]
    rank = jnp.sum((csum - onehot) * onehot, axis=1)
    padded = (counts + tm - 1) // tm * tm
    pad_end = jnp.cumsum(padded)
    pad_start = pad_end - padded
    slot = jnp.sum(onehot * pad_start[None, :], axis=1) + rank
    total = n_assign + N_EXPERTS * tm
    n_blk = total // tm
    slot_tok = _slot_tokens(slot.astype(jnp.int32), total)
    n_active = pad_end[-1] // tm
    blk = jnp.arange(n_blk, dtype=jnp.int32)
    blk_exp = jnp.minimum(jnp.searchsorted(pad_end, blk * tm, side="right"), N_EXPERTS - 1).astype(jnp.int32)
    blk_exp = jnp.where(blk < n_active, blk_exp, blk_exp[jnp.maximum(n_active - 1, 0)])

    wa = tt + WINDOW_ALIGN
    n_tiles = n_assign // (2 * tt)
    before = jnp.concatenate([jnp.zeros((1, N_EXPERTS), jnp.int32), csum[2 * tt - 1::2 * tt][:-1]], axis=0)
    first = pad_start[None, :] + before
    win_base = jnp.minimum(first // WINDOW_ALIGN * WINDOW_ALIGN, total - wa)
    win_row0 = jnp.arange(N_EXPERTS, dtype=jnp.int32)[None, :] * wa - win_base
    win_loc = slot + jnp.sum(onehot.reshape(n_tiles, 2 * tt, N_EXPERTS) * win_row0[:, None, :],
                             axis=2).reshape(n_assign)
    return (slot_tok.reshape(n_blk, tm), blk_exp, n_active.astype(jnp.int32).reshape(1),
            win_base.reshape(n_tiles * N_EXPERTS).astype(jnp.int32), win_loc.reshape(-1, 2))


ROW_ISSUE_UNROLL = 8


def _moe_body(bexp_ref, nact_ref, tok_hbm, h_hbm, w1_hbm, w3_hbm, w2_hbm, o_ref,
              tok_s, x_s, xb_s, acc_s, w1_s, w3_s, w2_s, st1_s, st3_s, st2_s,
              sem_idx, sem_row, sem_w, *, tm, tf):
    i = pl.program_id(0)
    n_active = nact_ref[0]
    chunks = _hidden_chunks(w1_s.shape[1], tf)
    rows_per_chunk = -(-tm // len(chunks))

    def idx_copy(step):
        return pltpu.make_async_copy(tok_hbm.at[step], tok_s.at[step % 2], sem_idx.at[step % 2])

    n_sub = h_hbm.shape[1]

    def row_copy(slot, r, tok):
        return pltpu.make_async_copy(h_hbm.at[tok], x_s.at[slot, pl.ds(r * n_sub, n_sub)], sem_row.at[slot])

    @pl.when(i == 0)
    def _():
        idx_copy(0).start()
        idx_copy(0).wait()

        def issue(r, c):
            row_copy(0, r, tok_s[0, r]).start()
            return c

        lax.fori_loop(0, tm, issue, 0, unroll=ROW_ISSUE_UNROLL)
        idx_copy(1).start()

    expert = bexp_ref[i]
    new_expert = (i == 0) | (expert != bexp_ref[jnp.maximum(i - 1, 0)])

    @pl.when(new_expert & (i < n_active))
    def _():
        def weight_copies(n):
            c0, cw = chunks[n]
            slot = n % 2
            return (pltpu.make_async_copy(w1_hbm.at[expert, :, pl.ds(c0, cw)], st1_s.at[slot, :, pl.ds(0, cw)],
                                          sem_w.at[slot]),
                    pltpu.make_async_copy(w3_hbm.at[expert, :, pl.ds(c0, cw)], st3_s.at[slot, :, pl.ds(0, cw)],
                                          sem_w.at[slot]),
                    pltpu.make_async_copy(w2_hbm.at[expert, pl.ds(c0, cw), :], st2_s.at[slot, pl.ds(0, cw), :],
                                          sem_w.at[slot]))

        for cp in weight_copies(0):
            cp.start()
        for n, (c0, cw) in enumerate(chunks):
            if n + 1 < len(chunks):
                for cp in weight_copies(n + 1):
                    cp.start()
            for cp in weight_copies(n):
                cp.wait()
            slot = n % 2
            w1_s[:, c0:c0 + cw] = st1_s[slot, :, 0:cw].astype(BF16)
            w3_s[:, c0:c0 + cw] = st3_s[slot, :, 0:cw].astype(BF16)
            w2_s[c0:c0 + cw, :] = st2_s[slot, 0:cw, :].astype(BF16)

    @pl.when(i <= n_active)
    def _():
        slot = i % 2
        pltpu.make_async_copy(x_s.at[slot], x_s.at[slot], sem_row.at[slot]).wait()
        for s in range(n_sub):
            xb_s[:, s * LANES:(s + 1) * LANES] = x_s[slot, pl.ds(s, tm, stride=n_sub), :].astype(BF16)

    @pl.when(i < n_active)
    def _():
        nxt = (i + 1) % 2
        idx_copy(i + 1).wait()
        xb = xb_s[...]
        for n, (c0, cw) in enumerate(chunks):
            part = _swiglu_chunk(xb, w1_s, w3_s, w2_s, c0, cw)
            if n == 0:
                acc_s[...] = part
            else:
                acc_s[...] += part
            for r in range(n * rows_per_chunk, min((n + 1) * rows_per_chunk, tm)):
                row_copy(nxt, r, tok_s[nxt, r]).start()

        @pl.when(i + 2 <= n_active)
        def _():
            idx_copy(i + 2).start()

    @pl.when(i >= n_active)
    def _():
        acc_s[...] = jnp.zeros(acc_s.shape, acc_s.dtype)

    o_ref[...] = acc_s[...].astype(o_ref.dtype)


def _moe_experts(h, slot_tok, blk_exp, n_active, w1, w3, w2, tm, tf):
    t, n_sub, _ = h.shape
    d = n_sub * LANES
    n_blk = slot_tok.shape[0]
    f = w1.shape[2]
    any_space = pl.BlockSpec(memory_space=pl.ANY)
    grid_spec = pltpu.PrefetchScalarGridSpec(
        num_scalar_prefetch=2,
        grid=(n_blk,),
        in_specs=[any_space, any_space, any_space, any_space, any_space],
        out_specs=pl.BlockSpec((tm, d), lambda i, bexp, nact: (i, 0)),
        scratch_shapes=[pltpu.SMEM((2, tm), jnp.int32),
                        pltpu.VMEM((2, tm * n_sub, LANES), F32),
                        pltpu.VMEM((tm, d), BF16),
                        pltpu.VMEM((tm, d), F32),
                        pltpu.VMEM((d, f), BF16), pltpu.VMEM((d, f), BF16), pltpu.VMEM((f, d), BF16),
                        pltpu.VMEM((2, d, tf), F32), pltpu.VMEM((2, d, tf), F32), pltpu.VMEM((2, tf, d), F32),
                        pltpu.SemaphoreType.DMA((2,)),
                        pltpu.SemaphoreType.DMA((2,)),
                        pltpu.SemaphoreType.DMA((2,))])
    return pl.pallas_call(
        functools.partial(_moe_body, tm=tm, tf=tf),
        grid_spec=grid_spec,
        out_shape=jax.ShapeDtypeStruct((n_blk * tm, d), BF16),
        compiler_params=_params("arbitrary"),
        name="moe_experts",
    )(blk_exp, n_active, slot_tok, h, w1, w3, w2)


COMBINE_SLOTS = 3


def _combine_body(base_ref, y_hbm, info_ref, x_ref, mod_ref, g_ref, o_ref, win_s, sem, *, wa):
    i = pl.program_id(0)
    n = pl.num_programs(0)

    def window_copy(step, e):
        slot = step % COMBINE_SLOTS
        base = pl.multiple_of(base_ref[step * N_EXPERTS + e], WINDOW_ALIGN)
        return pltpu.make_async_copy(y_hbm.at[pl.ds(base, wa)], win_s.at[slot, pl.ds(e * wa, wa)], sem.at[slot])

    def start_windows(step):
        for e in range(N_EXPERTS):
            window_copy(step, e).start()

    @pl.when(i == 0)
    def _():
        for step in range(COMBINE_SLOTS - 1):
            @pl.when(step < n)
            def _():
                start_windows(step)

    @pl.when(i + COMBINE_SLOTS - 1 < n)
    def _():
        start_windows(i + COMBINE_SLOTS - 1)

    for e in range(N_EXPERTS):
        window_copy(i, e).wait()

    info = info_ref[...]
    win = win_s[i % COMBINE_SLOTS]
    row = lax.broadcasted_iota(jnp.int32, (info.shape[0], win.shape[0]), 1).astype(F32)
    pick0 = jnp.where(row == info[:, 0:1], 1.0, 0.0).astype(BF16)
    pick1 = jnp.where(row == info[:, 1:2], 1.0, 0.0).astype(BF16)
    f = (info[:, 2:3] * jnp.dot(pick0, win, preferred_element_type=F32)
         + info[:, 3:4] * jnp.dot(pick1, win, preferred_element_type=F32))
    xn = x_ref[...] + mod_ref[0, M_G2:M_G2 + 1] * f
    o_ref[...] = _rms(xn, g_ref[...])


def _combine_final(win_base, y, info, x, mod, mod_row, final_g, tt):
    t, d = x.shape
    wa = tt + WINDOW_ALIGN
    grid_spec = pltpu.PrefetchScalarGridSpec(
        num_scalar_prefetch=1,
        grid=(t // tt,),
        in_specs=[pl.BlockSpec(memory_space=pl.ANY),
                  pl.BlockSpec((tt, LANES), lambda i, base: (i, 0)),
                  pl.BlockSpec((tt, d), lambda i, base: (i, 0)),
                  pl.BlockSpec((1, 6, d), lambda i, base: (mod_row(i), 0, 0)),
                  pl.BlockSpec((1, d), lambda i, base: (0, 0))],
        out_specs=pl.BlockSpec((tt, d), lambda i, base: (i, 0)),
        scratch_shapes=[pltpu.VMEM((COMBINE_SLOTS, N_EXPERTS * wa, d), y.dtype),
                        pltpu.SemaphoreType.DMA((COMBINE_SLOTS,))])
    return pl.pallas_call(
        functools.partial(_combine_body, wa=wa),
        grid_spec=grid_spec,
        out_shape=jax.ShapeDtypeStruct((t, d), F32),
        compiler_params=_params("arbitrary"),
        name="moe_combine_final",
    )(win_base, y, info, x, mod, final_g.reshape(1, d))


class _Tiles(NamedTuple):
    rows: int
    hidden: int
    expert_rows: int
    combine_rows: int


def _choose_tiles(seq):
    return _Tiles(rows=min(512, seq), hidden=512, expert_rows=512, combine_rows=min(128, seq))


def kernel(x, c, ctx, c_ctx, w_mod, b_mod, norm1_g, norm2_g, w_in, conv_a_w, conv_b_w, conv_b_b, conv_ln_g, conv_ln_b, sgu_ln_g, sgu_ln_b, sgu_w, sgu_b, rpb, group_g, w_out, ffn_w1, ffn_w3, ffn_w2, router_w, router_b, moe_w1, moe_w3, moe_w2, final_g):
    bsz, seq, d = x.shape
    n_ctx = ctx.shape[1]
    depth = w_mod.shape[0]
    rows = seq // GRID_W
    assert d == D_MODEL and seq % (Q_BLOCK_ROWS * GRID_W) == 0 and n_ctx % CHUNK == 0 and bsz + 1 <= 8
    assert depth == 2, "layer 0 dense with context, layer 1 (last) MoE without context"

    tiles = _choose_tiles(seq)
    tm = tiles.rows
    lat_row = lambda bi: bi
    ctx_row = lambda bi: bsz

    c8 = jnp.zeros((8, d), F32).at[:bsz].set(c).at[bsz].set(c_ctx)
    mod_all = _modulation(c8, w_mod, b_mod).reshape(depth, 8, 6, d)

    xl, xc = x, ctx
    proj = _norm_proj(xl, norm1_g[0], mod_all[0], lat_row, w_in[0].astype(BF16), tm)
    proj_c = _norm_proj(xc, norm1_g[0], mod_all[0], ctx_row, w_in[0].astype(BF16), n_ctx)
    out = None
    for l in range(depth):
        last = l == depth - 1
        mod = mod_all[l]
        w_out_b = w_out[l].astype(BF16)
        sw_b = sgu_w[l].astype(BF16)
        sbias = jnp.repeat(sgu_b[l].T, HEAD_W, axis=1)
        gg = group_g[l]
        conv_args = (conv_a_w[l], conv_b_w[l], conv_b_b[l], conv_ln_g[l], conv_ln_b[l],
                     sgu_ln_g[l], sgu_ln_b[l], sw_b, sbias, gg[:3 * GROUP_W])
        gg_d = gg[3 * GROUP_W:]

        (p, q, kv), (pc, qc, kvc) = proj, proj_c
        abc = _conv_mixer(p, *conv_args, ts=tm)
        dn = _nbr_attention(q, kv, kvc, _window_bias(rpb[l], rows), gg_d)

        if not last:
            abc_c = _conv_mixer(pc, *conv_args, ts=n_ctx)
            dn_c = _ctx_attention(qc, kvc, gg_d)
            ffn_w = tuple(w[l // 2].astype(BF16) for w in (ffn_w1, ffn_w3, ffn_w2))
            nxt = (mod_all[l + 1], norm1_g[l + 1], w_in[l + 1].astype(BF16))
            xl, *proj = _layer_transition(abc, dn, w_out_b, xl, mod, nxt[0], lat_row, norm2_g[l], ffn_w,
                                          nxt[1], nxt[2], tm, tiles.hidden)
            xc, *proj_c = _layer_transition(abc_c, dn_c, w_out_b, xc, mod, nxt[0], ctx_row, norm2_g[l], ffn_w,
                                            nxt[1], nxt[2], n_ctx, tiles.hidden)
        else:
            rw = jnp.zeros((d, LANES), F32).at[:, :N_EXPERTS].set(router_w[l // 2]).astype(BF16)
            rb = jnp.full((1, LANES), NEG_INF, F32).at[0, :N_EXPERTS].set(router_b[l // 2])
            xl, h, info = _merge_router(abc, dn, w_out_b, xl, mod, lat_row, norm2_g[l], rw, rb, tm)
            t = bsz * seq
            info = info.reshape(t, LANES)
            tt = tiles.combine_rows
            slot_tok, blk_exp, n_active, win_base, win_loc = _route_plan(
                info[:, :2].astype(jnp.int32), tiles.expert_rows, tt)
            w1, w3, w2 = moe_w1[l // 2], moe_w3[l // 2], moe_w2[l // 2]
            y = _moe_experts(h.reshape(t, d // LANES, LANES), slot_tok, blk_exp, n_active, w1, w3, w2,
                             tiles.expert_rows, tiles.hidden)
            info = jnp.concatenate([win_loc.astype(F32), info[:, 2:]], axis=1)
            out = _combine_final(win_base, y, info, xl.reshape(t, d), mod, lambda i: i * tt // seq,
                                 final_g, tt).reshape(bsz, seq, d)
    return out
```

```python
import functools
from typing import NamedTuple

import numpy as np
import jax
import jax.numpy as jnp
from jax import lax
from jax.experimental import pallas as pl
from jax.experimental.pallas import tpu as pltpu

F32 = jnp.float32
BF16 = jnp.bfloat16

D_MODEL = 1024
GRID_W = 64
EPS = 1e-6
NEG_INF = -1e30
GROUP_W = 256
N_HEADS = 4
HEAD_W = GROUP_W // N_HEADS
CONV_A = 3
CONV_B = 31
CHUNK = 128
WIN_R = 8
WIN_C = 16
OFF_B = 3 * GROUP_W
OFF_C = OFF_B + 2 * GROUP_W
OFF_D = OFF_C + 2 * GROUP_W
OFF_KV = OFF_D + GROUP_W
IN_COLS = OFF_KV + 2 * GROUP_W
N_EXPERTS = 8
LANES = 128
SUBLANES = 8
HALO = 16
Q_BLOCK_ROWS = 8
KEY_ROWS_BEFORE = 4
KEY_ROWS_AFTER = 4
VMEM_LIMIT = 56 * 1024 * 1024

M_SH1, M_SC1, M_G1, M_SH2, M_SC2, M_G2 = range(6)


def _params(*dims):
    return pltpu.CompilerParams(dimension_semantics=dims, vmem_limit_bytes=VMEM_LIMIT)


def _rms(x, g):
    return x * lax.rsqrt(jnp.mean(x * x, axis=-1, keepdims=True) + EPS) * g


def _layer_norm(x, g, b):
    mu = jnp.mean(x, axis=-1, keepdims=True)
    xc = x - mu
    var = jnp.mean(xc * xc, axis=-1, keepdims=True)
    return xc * lax.rsqrt(var + EPS) * g + b


def _mod_body(c_ref, w_ref, b_ref, o_ref):
    s = c_ref[...]
    s = s * jax.nn.sigmoid(s)
    o_ref[0] = jnp.dot(s, w_ref[0], preferred_element_type=F32,
                       precision=lax.Precision.HIGHEST) + b_ref[0]


def _modulation(c8, w_mod, b_mod):
    n_layers, d, six_d = w_mod.shape
    return pl.pallas_call(
        _mod_body,
        grid=(n_layers, six_d // d),
        in_specs=[pl.BlockSpec((8, d), lambda l, j: (0, 0)),
                  pl.BlockSpec((1, d, d), lambda l, j: (l, 0, j)),
                  pl.BlockSpec((1, 1, d), lambda l, j: (l, 0, j))],
        out_specs=pl.BlockSpec((1, 8, d), lambda l, j: (l, 0, j)),
        out_shape=jax.ShapeDtypeStruct((n_layers, 8, six_d), F32),
        compiler_params=_params("arbitrary", "arbitrary"),
        name="modulation",
    )(c8, w_mod, b_mod.reshape(n_layers, 1, six_d))


def _modulated_norm(x, g, m, shift, scale):
    return _rms(x, g) * (1.0 + m[scale:scale + 1]) + m[shift:shift + 1]


def _input_projection(x, g_ref, m, w_ref, p_ref, q_ref, kv_ref):
    hb = _modulated_norm(x, g_ref[...], m, M_SH1, M_SC1).astype(BF16)
    p_ref[0] = jnp.dot(hb, w_ref[:, :OFF_D], preferred_element_type=F32)
    q = jnp.dot(hb, w_ref[:, OFF_D:OFF_KV], preferred_element_type=F32)
    q_ref[0] = (q * (HEAD_W ** -0.5)).astype(BF16)
    kv_ref[0] = jnp.dot(hb, w_ref[:, OFF_KV:], preferred_element_type=F32).astype(BF16)


def _norm_proj_body(x_ref, g_ref, mod_ref, w_ref, p_ref, q_ref, kv_ref):
    _input_projection(x_ref[0], g_ref, mod_ref[0], w_ref, p_ref, q_ref, kv_ref)


def _norm_proj(x, g, mod, mod_row, w_in_bf16, tm):
    b, s, d = x.shape
    return pl.pallas_call(
        _norm_proj_body,
        grid=(b, s // tm),
        in_specs=[pl.BlockSpec((1, tm, d), lambda bi, i: (bi, i, 0)),
                  pl.BlockSpec((1, d), lambda bi, i: (0, 0)),
                  pl.BlockSpec((1, 6, d), lambda bi, i: (mod_row(bi), 0, 0)),
                  pl.BlockSpec((d, IN_COLS), lambda bi, i: (0, 0))],
        out_specs=[pl.BlockSpec((1, tm, OFF_D), lambda bi, i: (bi, i, 0)),
                   pl.BlockSpec((1, tm, GROUP_W), lambda bi, i: (bi, i, 0)),
                   pl.BlockSpec((1, tm, 2 * GROUP_W), lambda bi, i: (bi, i, 0))],
        out_shape=[jax.ShapeDtypeStruct((b, s, OFF_D), F32),
                   jax.ShapeDtypeStruct((b, s, GROUP_W), BF16),
                   jax.ShapeDtypeStruct((b, s, 2 * GROUP_W), BF16)],
        compiler_params=_params("parallel", "parallel"),
        name="norm_proj",
    )(x, g.reshape(1, d), mod, w_in_bf16)


def _conv_inputs(blk):
    za = blk[:, GROUP_W:2 * GROUP_W] * blk[:, 2 * GROUP_W:3 * GROUP_W]
    zb = blk[:, OFF_B:OFF_B + GROUP_W] * jax.nn.sigmoid(blk[:, OFF_B + GROUP_W:OFF_C])
    return za, zb


def _mixer_body(prev_ref, cur_ref, next_ref, wa_ref, wb_ref, bb_ref, blg_ref, blb_ref,
                slg_ref, slb_ref, sw_ref, sbias_ref, gg_ref, o_ref, za_s, zb_s, *, ts, rc):
    i = pl.program_id(1)
    n = pl.num_programs(1)
    has_prev = (i > 0).astype(F32)
    has_next = (i < n - 1).astype(F32)

    pa, pb = _conv_inputs(prev_ref[0])
    za_s[0:HALO] = pa * has_prev
    zb_s[0, 0:HALO] = pb * has_prev
    ca, cb = _conv_inputs(cur_ref[0])
    za_s[HALO:HALO + ts] = ca
    zb_s[0, HALO:HALO + ts] = cb
    na, nb = _conv_inputs(next_ref[0])
    za_s[HALO + ts:] = na * has_next
    zb_s[0, HALO + ts:] = nb * has_next
    n_shifted = ts + 2 * HALO - SUBLANES
    for b in range(1, SUBLANES):
        zb_s[b, 0:n_shifted] = zb_s[0, b:b + n_shifted]

    wa = wa_ref[...]
    wb = wb_ref[...]
    lane_head = lax.broadcasted_iota(jnp.int32, (1, GROUP_W), 1) // HEAD_W
    for r0 in range(0, ts, rc):
        acc = wa[0:1] * za_s[HALO + r0 - 1:HALO + r0 - 1 + rc]
        for j in range(1, CONV_A):
            acc = acc + wa[j:j + 1] * za_s[HALO + r0 - 1 + j:HALO + r0 - 1 + j + rc]
        ya = cur_ref[0, r0:r0 + rc, 0:GROUP_W] * acc
        o_ref[0, r0:r0 + rc, 0:GROUP_W] = _rms(ya, gg_ref[:, 0:GROUP_W]).astype(o_ref.dtype)

        base = HALO + r0 - CONV_B // 2
        acc = None
        for j in range(CONV_B):
            b, a = (base + j) % SUBLANES, (base + j) // SUBLANES * SUBLANES
            term = wb[j:j + 1] * zb_s[b, a:a + rc]
            acc = term if acc is None else acc + term
        yb = _layer_norm(acc + bb_ref[...], blg_ref[...], blb_ref[...])
        yb = yb * jax.nn.sigmoid(yb)
        o_ref[0, r0:r0 + rc, GROUP_W:2 * GROUP_W] = _rms(yb, gg_ref[:, GROUP_W:2 * GROUP_W]).astype(o_ref.dtype)

    for r0 in range(0, ts, CHUNK):
        z = jax.nn.gelu(cur_ref[0, r0:r0 + CHUNK, OFF_C:OFF_D])
        u = z[:, :GROUP_W]
        v = _layer_norm(z[:, GROUP_W:], slg_ref[...], slb_ref[...]).astype(BF16)
        mixed = sbias_ref[...]
        for h in range(N_HEADS):
            vh = v * (lane_head == h).astype(BF16)
            mixed = mixed + jnp.dot(sw_ref[h], vh, preferred_element_type=F32)
        yc = u * mixed
        o_ref[0, r0:r0 + CHUNK, 2 * GROUP_W:3 * GROUP_W] = _rms(yc, gg_ref[:, 2 * GROUP_W:3 * GROUP_W]).astype(o_ref.dtype)


def _conv_mixer(p, wa, wb, bb, blg, blb, slg, slb, sw_bf16, sbias, gg, ts):
    b, s, _ = p.shape
    rc = min(64, ts)
    hb = ts // HALO
    n_halo = s // HALO
    row = lambda a: a.reshape(1, -1)
    const = lambda *shape: pl.BlockSpec(shape, lambda bi, i: (0,) * len(shape))
    return pl.pallas_call(
        functools.partial(_mixer_body, ts=ts, rc=rc),
        grid=(b, s // ts),
        in_specs=[pl.BlockSpec((1, HALO, OFF_D), lambda bi, i: (bi, jnp.maximum(i * hb - 1, 0), 0)),
                  pl.BlockSpec((1, ts, OFF_D), lambda bi, i: (bi, i, 0)),
                  pl.BlockSpec((1, HALO, OFF_D), lambda bi, i: (bi, jnp.minimum((i + 1) * hb, n_halo - 1), 0)),
                  const(CONV_A, GROUP_W), const(CONV_B, GROUP_W), const(1, GROUP_W),
                  const(1, GROUP_W), const(1, GROUP_W), const(1, GROUP_W), const(1, GROUP_W),
                  const(N_HEADS, CHUNK, CHUNK), const(CHUNK, GROUP_W), const(1, 3 * GROUP_W)],
        out_specs=pl.BlockSpec((1, ts, 3 * GROUP_W), lambda bi, i: (bi, i, 0)),
        out_shape=jax.ShapeDtypeStruct((b, s, 3 * GROUP_W), BF16),
        scratch_shapes=[pltpu.VMEM((ts + 2 * HALO, GROUP_W), F32),
                        pltpu.VMEM((SUBLANES, ts + 2 * HALO, GROUP_W), F32)],
        compiler_params=_params("parallel", "parallel"),
        name="conv_mixer",
    )(p, p, p, wa, wb, row(bb), row(blg), row(blb), row(slg), row(slb), sw_bf16, sbias, row(gg))


def _head_masks():
    lane_head = lax.broadcasted_iota(jnp.int32, (1, GROUP_W), 1) // HEAD_W
    return [(lane_head == h).astype(BF16) for h in range(N_HEADS)]


_NT = (((1,), (1,)), ((), ()))


def _attend(q, masks, keys, values, biases, acc):
    for h in range(N_HEADS):
        qh = q * masks[h]
        scores = []
        for k, bias in zip(keys, biases):
            s = lax.dot_general(qh, k, _NT, preferred_element_type=F32)
            scores.append(s if bias is None else s + bias(h))
        m = scores[0].max(axis=-1, keepdims=True)
        for s in scores[1:]:
            m = jnp.maximum(m, s.max(axis=-1, keepdims=True))
        probs = [jnp.exp(s - m) for s in scores]
        denom = probs[0].sum(axis=-1, keepdims=True)
        for p in probs[1:]:
            denom = denom + p.sum(axis=-1, keepdims=True)
        o = jnp.dot(probs[0].astype(BF16), values[0](h), preferred_element_type=F32)
        for p, v in zip(probs[1:], values[1:]):
            o = o + jnp.dot(p.astype(BF16), v(h), preferred_element_type=F32)
        acc = acc + o / denom
    return acc


def _nbr_attn_body(q_ref, kp_ref, kc_ref, kn_ref, kx_ref, bias_ref, gg_ref, o_ref,
                   k_s, v_s, kx_s, vx_s, *, starts):
    masks = _head_masks()
    n_half = kp_ref.shape[1]
    n_cur = kc_ref.shape[1]
    pieces = ((kp_ref, 0, n_half), (kc_ref, n_half, n_cur), (kn_ref, n_half + n_cur, n_half))
    for ref, off, n in pieces:
        k_s[off:off + n] = ref[0, :, 0:GROUP_W]
        for h in range(N_HEADS):
            v_s[h, off:off + n] = ref[0, :, GROUP_W:] * masks[h]
    kx_s[...] = kx_ref[0, :, 0:GROUP_W]
    for h in range(N_HEADS):
        vx_s[h] = kx_ref[0, :, GROUP_W:] * masks[h]

    j = pl.program_id(1)
    nb = pl.num_programs(1)
    n_win = bias_ref.shape[3]
    tq = q_ref.shape[1] // len(starts[0])
    for sb, (first, mid, last) in enumerate(zip(*starts)):
        st = jnp.where(j == 0, first, jnp.where(j == nb - 1, last, mid)) * GRID_W
        st = pl.multiple_of(st, GRID_W)
        r0 = sb * tq
        acc = _attend(
            q_ref[0, r0:r0 + tq, :], masks,
            keys=[k_s[pl.ds(st, n_win)], kx_s[...]],
            values=[lambda h: v_s[h, pl.ds(st, n_win)], lambda h: vx_s[h]],
            biases=[lambda h: bias_ref[0, h, r0:r0 + tq, :], None],
            acc=jnp.zeros((tq, GROUP_W), F32))
        o_ref[0, r0:r0 + tq, :] = _rms(acc, gg_ref[...]).astype(o_ref.dtype)


SUB_Q_ROWS = 4
SUB_KEY_ROWS = 12


def _block_kinds(rows):
    nb = rows // Q_BLOCK_ROWS
    return (0, min(1, nb - 1), nb - 1)


def _sub_window_starts(rows):
    n_key_rows = KEY_ROWS_BEFORE + Q_BLOCK_ROWS + KEY_ROWS_AFTER
    starts = []
    for jv in _block_kinds(rows):
        per_sub = []
        for sb in range(Q_BLOCK_ROWS // SUB_Q_ROWS):
            r = Q_BLOCK_ROWS * jv + SUB_Q_ROWS * sb + np.arange(SUB_Q_ROWS)
            ks = np.clip(r - WIN_R // 2, 0, rows - WIN_R) - Q_BLOCK_ROWS * jv + KEY_ROWS_BEFORE
            start = int(min(ks.min(), n_key_rows - SUB_KEY_ROWS))
            assert start <= ks.min() and ks.max() + WIN_R <= start + SUB_KEY_ROWS
            per_sub.append(start)
        starts.append(tuple(per_sub))
    return tuple(starts)


def _window_bias(rpb_l, rows):
    n_key_rows = KEY_ROWS_BEFORE + Q_BLOCK_ROWS + KEY_ROWS_AFTER
    sub_starts = _sub_window_starts(rows)
    i = np.arange(Q_BLOCK_ROWS)[:, None, None, None]
    c = np.arange(GRID_W)[None, :, None, None]
    kr = np.arange(-KEY_ROWS_BEFORE, Q_BLOCK_ROWS + KEY_ROWS_AFTER)[None, None, :, None]
    kc = np.arange(GRID_W)[None, None, None, :]
    shape = (Q_BLOCK_ROWS, GRID_W, n_key_rows, GRID_W)
    edge = GRID_W - WIN_C
    cols = jnp.pad(rpb_l, ((0, 0), (0, 0), (edge, edge)), mode="edge")
    cols = jnp.pad(cols, ((0, 0), (0, 0), (0, 1)))
    skew = jnp.broadcast_to(cols[:, :, None, :], cols.shape[:2] + (GRID_W, 2 * GRID_W))
    skew = skew.reshape(cols.shape[:2] + (2 * GRID_W * GRID_W,))[:, :, :GRID_W * (2 * GRID_W - 1)]
    by_col = skew.reshape(cols.shape[:2] + (GRID_W, 2 * GRID_W - 1))[..., GRID_W - 1:]
    lo = KEY_ROWS_BEFORE
    hi = Q_BLOCK_ROWS + KEY_ROWS_AFTER - WIN_R
    by_col = jnp.pad(by_col, ((0, 0), (lo, hi), (0, 0), (0, 0)))
    tables = []
    for jv, starts in zip(_block_kinds(rows), sub_starts):
        r = Q_BLOCK_ROWS * jv + i
        ks = np.clip(r - WIN_R // 2, 0, rows - WIN_R)
        krow = Q_BLOCK_ROWS * jv + kr
        c_start = np.clip(c - WIN_C // 2, 0, GRID_W - WIN_C)
        valid = np.broadcast_to((krow >= ks) & (krow < ks + WIN_R) & (krow >= 0) & (krow < rows)
                                & (kc >= c_start) & (kc < c_start + WIN_C), shape)
        n_win = SUB_KEY_ROWS * GRID_W
        per_row, per_row_valid = [], []
        for ii in range(Q_BLOCK_ROWS):
            st = starts[ii // SUB_Q_ROWS]
            first = WIN_R - 1 - ii + st
            piece = by_col[:, first:first + SUB_KEY_ROWS].transpose(0, 2, 1, 3)
            per_row.append(piece.reshape(N_HEADS, GRID_W, n_win))
            per_row_valid.append(valid[ii, :, st:st + SUB_KEY_ROWS, :].reshape(GRID_W, n_win))
        values = jnp.concatenate(per_row, axis=1)
        tables.append(jnp.where(np.concatenate(per_row_valid)[None], values, NEG_INF))
    return jnp.stack(tables)


def _nbr_attention(q, kv, kv_ctx, bias, gg):
    b, s, _ = q.shape
    n_ctx = kv_ctx.shape[1]
    tb = Q_BLOCK_ROWS * GRID_W
    half = KEY_ROWS_BEFORE * GRID_W
    nb = s // tb
    assert nb >= 2 and KEY_ROWS_BEFORE == KEY_ROWS_AFTER and tb == 2 * half
    n_loc = tb + 2 * half
    n_win = SUB_KEY_ROWS * GRID_W
    variant = lambda j: jnp.where(j == 0, 0, jnp.where(j == nb - 1, 2, 1))
    return pl.pallas_call(
        functools.partial(_nbr_attn_body, starts=_sub_window_starts(s // GRID_W)),
        grid=(b, nb),
        in_specs=[pl.BlockSpec((1, tb, GROUP_W), lambda bi, j: (bi, j, 0)),
                  pl.BlockSpec((1, half, 2 * GROUP_W), lambda bi, j: (bi, jnp.maximum(2 * j - 1, 0), 0)),
                  pl.BlockSpec((1, tb, 2 * GROUP_W), lambda bi, j: (bi, j, 0)),
                  pl.BlockSpec((1, half, 2 * GROUP_W), lambda bi, j: (bi, jnp.minimum(2 * j + 2, 2 * nb - 1), 0)),
                  pl.BlockSpec((1, n_ctx, 2 * GROUP_W), lambda bi, j: (bi, 0, 0)),
                  pl.BlockSpec((1, N_HEADS, tb, n_win), lambda bi, j: (variant(j), 0, 0, 0)),
                  pl.BlockSpec((1, GROUP_W), lambda bi, j: (0, 0))],
        out_specs=pl.BlockSpec((1, tb, GROUP_W), lambda bi, j: (bi, j, 0)),
        out_shape=jax.ShapeDtypeStruct((b, s, GROUP_W), BF16),
        scratch_shapes=[pltpu.VMEM((n_loc, GROUP_W), BF16),
                        pltpu.VMEM((N_HEADS, n_loc, GROUP_W), BF16),
                        pltpu.VMEM((n_ctx, GROUP_W), BF16),
                        pltpu.VMEM((N_HEADS, n_ctx, GROUP_W), BF16)],
        compiler_params=_params("parallel", "arbitrary"),
        name="nbr_attention",
    )(q, kv, kv, kv, kv_ctx, bias, gg.reshape(1, GROUP_W))


def _ctx_attn_body(q_ref, kx_ref, gg_ref, o_ref, vx_s):
    masks = _head_masks()
    for h in range(N_HEADS):
        vx_s[h] = kx_ref[0, :, GROUP_W:] * masks[h]
    tq = q_ref.shape[1]
    acc = _attend(q_ref[0], masks, keys=[kx_ref[0, :, 0:GROUP_W]], values=[lambda h: vx_s[h]],
                  biases=[None], acc=jnp.zeros((tq, GROUP_W), F32))
    o_ref[0] = _rms(acc, gg_ref[...]).astype(o_ref.dtype)


def _ctx_attention(q, kv_ctx, gg):
    b, n_ctx, _ = q.shape
    return pl.pallas_call(
        _ctx_attn_body,
        grid=(b,),
        in_specs=[pl.BlockSpec((1, n_ctx, GROUP_W), lambda bi: (bi, 0, 0)),
                  pl.BlockSpec((1, n_ctx, 2 * GROUP_W), lambda bi: (bi, 0, 0)),
                  pl.BlockSpec((1, GROUP_W), lambda bi: (0, 0))],
        out_specs=pl.BlockSpec((1, n_ctx, GROUP_W), lambda bi: (bi, 0, 0)),
        out_shape=jax.ShapeDtypeStruct((b, n_ctx, GROUP_W), BF16),
        scratch_shapes=[pltpu.VMEM((N_HEADS, n_ctx, GROUP_W), BF16)],
        compiler_params=_params("parallel"),
        name="ctx_attention",
    )(q, kv_ctx, gg.reshape(1, GROUP_W))


def _mixer_residual(abc_ref, dn_ref, w_ref, x_ref, m):
    y = (jnp.dot(abc_ref[0], w_ref[0:3 * GROUP_W], preferred_element_type=F32)
         + jnp.dot(dn_ref[0], w_ref[3 * GROUP_W:], preferred_element_type=F32))
    return x_ref[0] + m[M_G1:M_G1 + 1] * y


def _merge_router_body(abc_ref, dn_ref, w_ref, x_ref, mod_ref, g_ref, rw_ref, rb_ref, xo_ref, h_ref, info_ref):
    m = mod_ref[0]
    xn = _mixer_residual(abc_ref, dn_ref, w_ref, x_ref, m)
    xo_ref[0] = xn
    h = _modulated_norm(xn, g_ref[...], m, M_SH2, M_SC2)
    n_sub = h.shape[1] // LANES
    for s in range(n_sub):
        h_ref[0, pl.ds(s, h.shape[0], stride=n_sub), :] = h[:, s * LANES:(s + 1) * LANES]
    logits = jnp.dot(h.astype(BF16), rw_ref[...], preferred_element_type=F32) + rb_ref[...]
    lane = lax.broadcasted_iota(jnp.int32, logits.shape, 1).astype(F32)
    m1 = logits.max(axis=-1, keepdims=True)
    i1 = jnp.where(logits == m1, lane, float(LANES)).min(axis=-1, keepdims=True)
    rest_logits = jnp.where(lane == i1, NEG_INF, logits)
    m2 = rest_logits.max(axis=-1, keepdims=True)
    i2 = jnp.where(rest_logits == m2, lane, float(LANES)).min(axis=-1, keepdims=True)
    e2 = jnp.exp(m2 - m1)
    den = 1.0 + e2
    info = jnp.where(lane == 0, i1, jnp.where(lane == 1, i2,
           jnp.where(lane == 2, 1.0 / den, jnp.where(lane == 3, e2 / den, 0.0))))
    info_ref[0] = info


def _merge_router(abc, dn, w_out_bf16, x, mod, mod_row, g2, router_w, router_b, tm):
    b, s, d = x.shape
    const = lambda *shape: pl.BlockSpec(shape, lambda bi, i: (0,) * len(shape))
    return pl.pallas_call(
        _merge_router_body,
        grid=(b, s // tm),
        in_specs=[pl.BlockSpec((1, tm, 3 * GROUP_W), lambda bi, i: (bi, i, 0)),
                  pl.BlockSpec((1, tm, GROUP_W), lambda bi, i: (bi, i, 0)),
                  const(4 * GROUP_W, d),
                  pl.BlockSpec((1, tm, d), lambda bi, i: (bi, i, 0)),
                  pl.BlockSpec((1, 6, d), lambda bi, i: (mod_row(bi), 0, 0)),
                  const(1, d), const(d, LANES), const(1, LANES)],
        out_specs=[pl.BlockSpec((1, tm, d), lambda bi, i: (bi, i, 0)),
                   pl.BlockSpec((1, tm * d // LANES, LANES), lambda bi, i: (bi, i, 0)),
                   pl.BlockSpec((1, tm, LANES), lambda bi, i: (bi, i, 0))],
        out_shape=[jax.ShapeDtypeStruct((b, s, d), F32),
                   jax.ShapeDtypeStruct((b, s * d // LANES, LANES), F32),
                   jax.ShapeDtypeStruct((b, s, LANES), F32)],
        compiler_params=_params("parallel", "parallel"),
        name="merge_router",
    )(abc, dn, w_out_bf16, x, mod, g2.reshape(1, d), router_w, router_b)


def _swiglu_chunk(xb, w1_ref, w3_ref, w2_ref, c0, cw):
    a = jnp.dot(xb, w1_ref[:, c0:c0 + cw], preferred_element_type=F32)
    g = jnp.dot(xb, w3_ref[:, c0:c0 + cw], preferred_element_type=F32)
    act = (a * jax.nn.sigmoid(a) * g).astype(BF16)
    return jnp.dot(act, w2_ref[c0:c0 + cw, :], preferred_element_type=F32)


def _hidden_chunks(f, tf):
    return [(c0, min(tf, f - c0)) for c0 in range(0, f, tf)]


def _transition_body(abc_ref, dn_ref, wo_ref, x_ref, mod_ref, g2_ref, w1_ref, w3_ref, w2_ref,
                     modn_ref, g1n_ref, win_ref, xo_ref, p_ref, q_ref, kv_ref, acc_ref, *, tf):
    m = mod_ref[0]
    xn = _mixer_residual(abc_ref, dn_ref, wo_ref, x_ref, m)
    h = _modulated_norm(xn, g2_ref[...], m, M_SH2, M_SC2).astype(BF16)
    for n, (c0, cw) in enumerate(_hidden_chunks(w1_ref.shape[1], tf)):
        part = _swiglu_chunk(h, w1_ref, w3_ref, w2_ref, c0, cw)
        if n == 0:
            acc_ref[...] = part
        else:
            acc_ref[...] += part
    x_next = xn + m[M_G2:M_G2 + 1] * acc_ref[...]
    xo_ref[0] = x_next
    _input_projection(x_next, g1n_ref, modn_ref[0], win_ref, p_ref, q_ref, kv_ref)


def _layer_transition(abc, dn, w_out_bf16, x, mod, mod_next, mod_row, g2, ffn_w, g1_next, w_in_next, tm, tf):
    b, s, d = x.shape
    w1, w3, w2 = ffn_w
    f = w1.shape[1]
    resident = lambda *shape: pl.BlockSpec(shape, lambda bi, i: (0,) * len(shape), pipeline_mode=pl.Buffered(1))
    rows = lambda width: pl.BlockSpec((1, tm, width), lambda bi, i: (bi, i, 0))
    mod_spec = pl.BlockSpec((1, 6, d), lambda bi, i: (mod_row(bi), 0, 0))
    return pl.pallas_call(
        functools.partial(_transition_body, tf=tf),
        grid=(b, s // tm),
        in_specs=[rows(3 * GROUP_W), rows(GROUP_W), resident(4 * GROUP_W, d), rows(d), mod_spec, resident(1, d),
                  resident(d, f), resident(d, f), resident(f, d),
                  mod_spec, resident(1, d), resident(d, IN_COLS)],
        out_specs=[rows(d), rows(OFF_D), rows(GROUP_W), rows(2 * GROUP_W)],
        out_shape=[jax.ShapeDtypeStruct((b, s, d), F32),
                   jax.ShapeDtypeStruct((b, s, OFF_D), F32),
                   jax.ShapeDtypeStruct((b, s, GROUP_W), BF16),
                   jax.ShapeDtypeStruct((b, s, 2 * GROUP_W), BF16)],
        scratch_shapes=[pltpu.VMEM((tm, d), F32)],
        compiler_params=_params("parallel", "parallel"),
        name="layer_transition",
    )(abc, dn, w_out_bf16, x, mod, g2.reshape(1, d), w1, w3, w2, mod_next, g1_next.reshape(1, d), w_in_next)


WINDOW_ALIGN = 16


def _route_plan(expert_idx, tm, tt):
    n_assign = expert_idx.size
    e_flat = expert_idx.reshape(-1)
    onehot = (e_flat[:, None] == jnp.arange(N_EXPERTS, dtype=jnp.int32)[None, :]).astype(jnp.int32)
    csum = jnp.cumsum(onehot, axis=0)
    counts = csum[-1]
    rank = jnp.sum((csum - onehot) * onehot, axis=1)
    padded = (counts + tm - 1) // tm * tm
    pad_end = jnp.cumsum(padded)
    pad_start = pad_end - padded
    slot = jnp.sum(onehot * pad_start[None, :], axis=1) + rank
    total = n_assign + N_EXPERTS * tm
    n_blk = total // tm
    slot_tok = jnp.zeros((total,), jnp.int32).at[slot].set(
        jnp.arange(n_assign, dtype=jnp.int32) // 2, unique_indices=True, mode="promise_in_bounds")
    n_active = pad_end[-1] // tm
    blk = jnp.arange(n_blk, dtype=jnp.int32)
    blk_exp = jnp.minimum(jnp.searchsorted(pad_end, blk * tm, side="right"), N_EXPERTS - 1).astype(jnp.int32)
    blk_exp = jnp.where(blk < n_active, blk_exp, blk_exp[jnp.maximum(n_active - 1, 0)])

    wa = tt + WINDOW_ALIGN
    n_tiles = n_assign // (2 * tt)
    before = jnp.concatenate([jnp.zeros((1, N_EXPERTS), jnp.int32), csum[2 * tt - 1::2 * tt][:-1]], axis=0)
    first = pad_start[None, :] + before
    win_base = jnp.minimum(first // WINDOW_ALIGN * WINDOW_ALIGN, total - wa)
    win_row0 = jnp.arange(N_EXPERTS, dtype=jnp.int32)[None, :] * wa - win_base
    win_loc = slot + jnp.sum(onehot.reshape(n_tiles, 2 * tt, N_EXPERTS) * win_row0[:, None, :],
                             axis=2).reshape(n_assign)
    return (slot_tok.reshape(n_blk, tm), blk_exp, n_active.astype(jnp.int32).reshape(1),
            win_base.reshape(n_tiles * N_EXPERTS).astype(jnp.int32), win_loc.reshape(-1, 2))


ROW_ISSUE_UNROLL = 8


def _moe_body(bexp_ref, nact_ref, tok_hbm, h_hbm, w1_hbm, w3_hbm, w2_hbm, o_ref,
              tok_s, x_s, xb_s, acc_s, w1_s, w3_s, w2_s, st1_s, st3_s, st2_s,
              sem_idx, sem_row, sem_w, *, tm, tf):
    i = pl.program_id(0)
    n_active = nact_ref[0]
    chunks = _hidden_chunks(w1_s.shape[1], tf)
    rows_per_chunk = -(-tm // len(chunks))

    def idx_copy(step):
        return pltpu.make_async_copy(tok_hbm.at[step], tok_s.at[step % 2], sem_idx.at[step % 2])

    n_sub = h_hbm.shape[1]

    def row_copy(slot, r, tok):
        return pltpu.make_async_copy(h_hbm.at[tok], x_s.at[slot, pl.ds(r * n_sub, n_sub)], sem_row.at[slot])

    @pl.when(i == 0)
    def _():
        idx_copy(0).start()
        idx_copy(0).wait()

        def issue(r, c):
            row_copy(0, r, tok_s[0, r]).start()
            return c

        lax.fori_loop(0, tm, issue, 0, unroll=ROW_ISSUE_UNROLL)
        idx_copy(1).start()

    expert = bexp_ref[i]
    new_expert = (i == 0) | (expert != bexp_ref[jnp.maximum(i - 1, 0)])

    @pl.when(new_expert & (i < n_active))
    def _():
        def weight_copies(n):
            c0, cw = chunks[n]
            slot = n % 2
            return (pltpu.make_async_copy(w1_hbm.at[expert, :, pl.ds(c0, cw)], st1_s.at[slot, :, pl.ds(0, cw)],
                                          sem_w.at[slot]),
                    pltpu.make_async_copy(w3_hbm.at[expert, :, pl.ds(c0, cw)], st3_s.at[slot, :, pl.ds(0, cw)],
                                          sem_w.at[slot]),
                    pltpu.make_async_copy(w2_hbm.at[expert, pl.ds(c0, cw), :], st2_s.at[slot, pl.ds(0, cw), :],
                                          sem_w.at[slot]))

        for cp in weight_copies(0):
            cp.start()
        for n, (c0, cw) in enumerate(chunks):
            if n + 1 < len(chunks):
                for cp in weight_copies(n + 1):
                    cp.start()
            for cp in weight_copies(n):
                cp.wait()
            slot = n % 2
            w1_s[:, c0:c0 + cw] = st1_s[slot, :, 0:cw].astype(BF16)
            w3_s[:, c0:c0 + cw] = st3_s[slot, :, 0:cw].astype(BF16)
            w2_s[c0:c0 + cw, :] = st2_s[slot, 0:cw, :].astype(BF16)

    @pl.when(i <= n_active)
    def _():
        slot = i % 2
        pltpu.make_async_copy(x_s.at[slot], x_s.at[slot], sem_row.at[slot]).wait()
        for s in range(n_sub):
            xb_s[:, s * LANES:(s + 1) * LANES] = x_s[slot, pl.ds(s, tm, stride=n_sub), :].astype(BF16)

    @pl.when(i < n_active)
    def _():
        nxt = (i + 1) % 2
        idx_copy(i + 1).wait()
        xb = xb_s[...]
        for n, (c0, cw) in enumerate(chunks):
            part = _swiglu_chunk(xb, w1_s, w3_s, w2_s, c0, cw)
            if n == 0:
                acc_s[...] = part
            else:
                acc_s[...] += part
            for r in range(n * rows_per_chunk, min((n + 1) * rows_per_chunk, tm)):
                row_copy(nxt, r, tok_s[nxt, r]).start()

        @pl.when(i + 2 <= n_active)
        def _():
            idx_copy(i + 2).start()

    @pl.when(i >= n_active)
    def _():
        acc_s[...] = jnp.zeros(acc_s.shape, acc_s.dtype)

    o_ref[...] = acc_s[...].astype(o_ref.dtype)


def _moe_experts(h, slot_tok, blk_exp, n_active, w1, w3, w2, tm, tf):
    t, n_sub, _ = h.shape
    d = n_sub * LANES
    n_blk = slot_tok.shape[0]
    f = w1.shape[2]
    any_space = pl.BlockSpec(memory_space=pl.ANY)
    grid_spec = pltpu.PrefetchScalarGridSpec(
        num_scalar_prefetch=2,
        grid=(n_blk,),
        in_specs=[any_space, any_space, any_space, any_space, any_space],
        out_specs=pl.BlockSpec((tm, d), lambda i, bexp, nact: (i, 0)),
        scratch_shapes=[pltpu.SMEM((2, tm), jnp.int32),
                        pltpu.VMEM((2, tm * n_sub, LANES), F32),
                        pltpu.VMEM((tm, d), BF16),
                        pltpu.VMEM((tm, d), F32),
                        pltpu.VMEM((d, f), BF16), pltpu.VMEM((d, f), BF16), pltpu.VMEM((f, d), BF16),
                        pltpu.VMEM((2, d, tf), F32), pltpu.VMEM((2, d, tf), F32), pltpu.VMEM((2, tf, d), F32),
                        pltpu.SemaphoreType.DMA((2,)),
                        pltpu.SemaphoreType.DMA((2,)),
                        pltpu.SemaphoreType.DMA((2,))])
    return pl.pallas_call(
        functools.partial(_moe_body, tm=tm, tf=tf),
        grid_spec=grid_spec,
        out_shape=jax.ShapeDtypeStruct((n_blk * tm, d), BF16),
        compiler_params=_params("arbitrary"),
        name="moe_experts",
    )(blk_exp, n_active, slot_tok, h, w1, w3, w2)


COMBINE_SLOTS = 3


def _combine_body(base_ref, y_hbm, info_ref, x_ref, mod_ref, g_ref, o_ref, win_s, sem, *, wa):
    i = pl.program_id(0)
    n = pl.num_programs(0)

    def window_copy(step, e):
        slot = step % COMBINE_SLOTS
        base = pl.multiple_of(base_ref[step * N_EXPERTS + e], WINDOW_ALIGN)
        return pltpu.make_async_copy(y_hbm.at[pl.ds(base, wa)], win_s.at[slot, pl.ds(e * wa, wa)], sem.at[slot])

    def start_windows(step):
        for e in range(N_EXPERTS):
            window_copy(step, e).start()

    @pl.when(i == 0)
    def _():
        for step in range(COMBINE_SLOTS - 1):
            @pl.when(step < n)
            def _():
                start_windows(step)

    @pl.when(i + COMBINE_SLOTS - 1 < n)
    def _():
        start_windows(i + COMBINE_SLOTS - 1)

    for e in range(N_EXPERTS):
        window_copy(i, e).wait()

    info = info_ref[...]
    win = win_s[i % COMBINE_SLOTS]
    row = lax.broadcasted_iota(jnp.int32, (info.shape[0], win.shape[0]), 1).astype(F32)
    pick0 = jnp.where(row == info[:, 0:1], 1.0, 0.0).astype(BF16)
    pick1 = jnp.where(row == info[:, 1:2], 1.0, 0.0).astype(BF16)
    f = (info[:, 2:3] * jnp.dot(pick0, win, preferred_element_type=F32)
         + info[:, 3:4] * jnp.dot(pick1, win, preferred_element_type=F32))
    xn = x_ref[...] + mod_ref[0, M_G2:M_G2 + 1] * f
    o_ref[...] = _rms(xn, g_ref[...])


def _combine_final(win_base, y, info, x, mod, mod_row, final_g, tt):
    t, d = x.shape
    wa = tt + WINDOW_ALIGN
    grid_spec = pltpu.PrefetchScalarGridSpec(
        num_scalar_prefetch=1,
        grid=(t // tt,),
        in_specs=[pl.BlockSpec(memory_space=pl.ANY),
                  pl.BlockSpec((tt, LANES), lambda i, base: (i, 0)),
                  pl.BlockSpec((tt, d), lambda i, base: (i, 0)),
                  pl.BlockSpec((1, 6, d), lambda i, base: (mod_row(i), 0, 0)),
                  pl.BlockSpec((1, d), lambda i, base: (0, 0))],
        out_specs=pl.BlockSpec((tt, d), lambda i, base: (i, 0)),
        scratch_shapes=[pltpu.VMEM((COMBINE_SLOTS, N_EXPERTS * wa, d), y.dtype),
                        pltpu.SemaphoreType.DMA((COMBINE_SLOTS,))])
    return pl.pallas_call(
        functools.partial(_combine_body, wa=wa),
        grid_spec=grid_spec,
        out_shape=jax.ShapeDtypeStruct((t, d), F32),
        compiler_params=_params("arbitrary"),
        name="moe_combine_final",
    )(win_base, y, info, x, mod, final_g.reshape(1, d))


class _Tiles(NamedTuple):
    rows: int
    hidden: int
    expert_rows: int
    combine_rows: int


def _choose_tiles(seq):
    return _Tiles(rows=min(512, seq), hidden=512, expert_rows=512, combine_rows=min(128, seq))


def kernel(x, c, ctx, c_ctx, w_mod, b_mod, norm1_g, norm2_g, w_in, conv_a_w, conv_b_w, conv_b_b, conv_ln_g, conv_ln_b, sgu_ln_g, sgu_ln_b, sgu_w, sgu_b, rpb, group_g, w_out, ffn_w1, ffn_w3, ffn_w2, router_w, router_b, moe_w1, moe_w3, moe_w2, final_g):
    bsz, seq, d = x.shape
    n_ctx = ctx.shape[1]
    depth = w_mod.shape[0]
    rows = seq // GRID_W
    assert d == D_MODEL and seq % (Q_BLOCK_ROWS * GRID_W) == 0 and n_ctx % CHUNK == 0 and bsz + 1 <= 8
    assert depth == 2, "layer 0 dense with context, layer 1 (last) MoE without context"

    tiles = _choose_tiles(seq)
    tm = tiles.rows
    lat_row = lambda bi: bi
    ctx_row = lambda bi: bsz

    c8 = jnp.zeros((8, d), F32).at[:bsz].set(c).at[bsz].set(c_ctx)
    mod_all = _modulation(c8, w_mod, b_mod).reshape(depth, 8, 6, d)

    xl, xc = x, ctx
    proj = _norm_proj(xl, norm1_g[0], mod_all[0], lat_row, w_in[0].astype(BF16), tm)
    proj_c = _norm_proj(xc, norm1_g[0], mod_all[0], ctx_row, w_in[0].astype(BF16), n_ctx)
    out = None
    for l in range(depth):
        last = l == depth - 1
        mod = mod_all[l]
        w_out_b = w_out[l].astype(BF16)
        sw_b = sgu_w[l].astype(BF16)
        sbias = jnp.repeat(sgu_b[l].T, HEAD_W, axis=1)
        gg = group_g[l]
        conv_args = (conv_a_w[l], conv_b_w[l], conv_b_b[l], conv_ln_g[l], conv_ln_b[l],
                     sgu_ln_g[l], sgu_ln_b[l], sw_b, sbias, gg[:3 * GROUP_W])
        gg_d = gg[3 * GROUP_W:]

        (p, q, kv), (pc, qc, kvc) = proj, proj_c
        abc = _conv_mixer(p, *conv_args, ts=tm)
        dn = _nbr_attention(q, kv, kvc, _window_bias(rpb[l], rows), gg_d)

        if not last:
            abc_c = _conv_mixer(pc, *conv_args, ts=n_ctx)
            dn_c = _ctx_attention(qc, kvc, gg_d)
            ffn_w = tuple(w[l // 2].astype(BF16) for w in (ffn_w1, ffn_w3, ffn_w2))
            nxt = (mod_all[l + 1], norm1_g[l + 1], w_in[l + 1].astype(BF16))
            xl, *proj = _layer_transition(abc, dn, w_out_b, xl, mod, nxt[0], lat_row, norm2_g[l], ffn_w,
                                          nxt[1], nxt[2], tm, tiles.hidden)
            xc, *proj_c = _layer_transition(abc_c, dn_c, w_out_b, xc, mod, nxt[0], ctx_row, norm2_g[l], ffn_w,
                                            nxt[1], nxt[2], n_ctx, tiles.hidden)
        else:
            rw = jnp.zeros((d, LANES), F32).at[:, :N_EXPERTS].set(router_w[l // 2]).astype(BF16)
            rb = jnp.full((1, LANES), NEG_INF, F32).at[0, :N_EXPERTS].set(router_b[l // 2])
            xl, h, info = _merge_router(abc, dn, w_out_b, xl, mod, lat_row, norm2_g[l], rw, rb, tm)
            t = bsz * seq
            info = info.reshape(t, LANES)
            tt = tiles.combine_rows
            slot_tok, blk_exp, n_active, win_base, win_loc = _route_plan(
                info[:, :2].astype(jnp.int32), tiles.expert_rows, tt)
            w1, w3, w2 = moe_w1[l // 2], moe_w3[l // 2], moe_w2[l // 2]
            y = _moe_experts(h.reshape(t, d // LANES, LANES), slot_tok, blk_exp, n_active, w1, w3, w2,
                             tiles.expert_rows, tiles.hidden)
            info = info.at[:, :2].set(win_loc.astype(F32))
            out = _combine_final(win_base, y, info, xl.reshape(t, d), mod, lambda i: i * tt // seq,
                                 final_g, tt).reshape(bsz, seq, d)
    return out
```

```python
import functools
from typing import NamedTuple

import numpy as np
import jax
import jax.numpy as jnp
from jax import lax
from jax.experimental import pallas as pl
from jax.experimental.pallas import tpu as pltpu

F32 = jnp.float32
BF16 = jnp.bfloat16

D_MODEL = 1024
GRID_W = 64
EPS = 1e-6
NEG_INF = -1e30
GROUP_W = 256
N_HEADS = 4
HEAD_W = GROUP_W // N_HEADS
CONV_A = 3
CONV_B = 31
CHUNK = 128
WIN_R = 8
WIN_C = 16
OFF_B = 3 * GROUP_W
OFF_C = OFF_B + 2 * GROUP_W
OFF_D = OFF_C + 2 * GROUP_W
OFF_KV = OFF_D + GROUP_W
IN_COLS = OFF_KV + 2 * GROUP_W
N_EXPERTS = 8
LANES = 128
SUBLANES = 8
HALO = 16
Q_BLOCK_ROWS = 8
KEY_ROWS_BEFORE = 4
KEY_ROWS_AFTER = 4
VMEM_LIMIT = 56 * 1024 * 1024

M_SH1, M_SC1, M_G1, M_SH2, M_SC2, M_G2 = range(6)


def _params(*dims):
    return pltpu.CompilerParams(dimension_semantics=dims, vmem_limit_bytes=VMEM_LIMIT)


def _rms(x, g):
    return x * lax.rsqrt(jnp.mean(x * x, axis=-1, keepdims=True) + EPS) * g


def _layer_norm(x, g, b):
    mu = jnp.mean(x, axis=-1, keepdims=True)
    xc = x - mu
    var = jnp.mean(xc * xc, axis=-1, keepdims=True)
    return xc * lax.rsqrt(var + EPS) * g + b


def _mod_body(c_ref, w_ref, b_ref, o_ref):
    s = c_ref[...]
    s = s * jax.nn.sigmoid(s)
    o_ref[0] = jnp.dot(s, w_ref[0], preferred_element_type=F32,
                       precision=lax.Precision.HIGHEST) + b_ref[0]


def _modulation(c8, w_mod, b_mod):
    n_layers, d, six_d = w_mod.shape
    return pl.pallas_call(
        _mod_body,
        grid=(n_layers, six_d // d),
        in_specs=[pl.BlockSpec((8, d), lambda l, j: (0, 0)),
                  pl.BlockSpec((1, d, d), lambda l, j: (l, 0, j)),
                  pl.BlockSpec((1, 1, d), lambda l, j: (l, 0, j))],
        out_specs=pl.BlockSpec((1, 8, d), lambda l, j: (l, 0, j)),
        out_shape=jax.ShapeDtypeStruct((n_layers, 8, six_d), F32),
        compiler_params=_params("arbitrary", "arbitrary"),
        name="modulation",
    )(c8, w_mod, b_mod.reshape(n_layers, 1, six_d))


def _modulated_norm(x, g, m, shift, scale):
    return _rms(x, g) * (1.0 + m[scale:scale + 1]) + m[shift:shift + 1]


def _input_projection(x, g_ref, m, w_ref, p_ref, q_ref, kv_ref):
    hb = _modulated_norm(x, g_ref[...], m, M_SH1, M_SC1).astype(BF16)
    p_ref[0] = jnp.dot(hb, w_ref[:, :OFF_D], preferred_element_type=F32)
    q = jnp.dot(hb, w_ref[:, OFF_D:OFF_KV], preferred_element_type=F32)
    q_ref[0] = (q * (HEAD_W ** -0.5)).astype(BF16)
    kv_ref[0] = jnp.dot(hb, w_ref[:, OFF_KV:], preferred_element_type=F32).astype(BF16)


def _norm_proj_body(x_ref, g_ref, mod_ref, w_ref, p_ref, q_ref, kv_ref):
    _input_projection(x_ref[0], g_ref, mod_ref[0], w_ref, p_ref, q_ref, kv_ref)


def _norm_proj(x, g, mod, mod_row, w_in_bf16, tm):
    b, s, d = x.shape
    return pl.pallas_call(
        _norm_proj_body,
        grid=(b, s // tm),
        in_specs=[pl.BlockSpec((1, tm, d), lambda bi, i: (bi, i, 0)),
                  pl.BlockSpec((1, d), lambda bi, i: (0, 0)),
                  pl.BlockSpec((1, 6, d), lambda bi, i: (mod_row(bi), 0, 0)),
                  pl.BlockSpec((d, IN_COLS), lambda bi, i: (0, 0))],
        out_specs=[pl.BlockSpec((1, tm, OFF_D), lambda bi, i: (bi, i, 0)),
                   pl.BlockSpec((1, tm, GROUP_W), lambda bi, i: (bi, i, 0)),
                   pl.BlockSpec((1, tm, 2 * GROUP_W), lambda bi, i: (bi, i, 0))],
        out_shape=[jax.ShapeDtypeStruct((b, s, OFF_D), F32),
                   jax.ShapeDtypeStruct((b, s, GROUP_W), BF16),
                   jax.ShapeDtypeStruct((b, s, 2 * GROUP_W), BF16)],
        compiler_params=_params("parallel", "parallel"),
        name="norm_proj",
    )(x, g.reshape(1, d), mod, w_in_bf16)


def _conv_inputs(blk):
    za = blk[:, GROUP_W:2 * GROUP_W] * blk[:, 2 * GROUP_W:3 * GROUP_W]
    zb = blk[:, OFF_B:OFF_B + GROUP_W] * jax.nn.sigmoid(blk[:, OFF_B + GROUP_W:OFF_C])
    return za, zb


def _mixer_body(prev_ref, cur_ref, next_ref, wa_ref, wb_ref, bb_ref, blg_ref, blb_ref,
                slg_ref, slb_ref, sw_ref, sbias_ref, gg_ref, o_ref, za_s, zb_s, *, ts, rc):
    i = pl.program_id(1)
    n = pl.num_programs(1)
    has_prev = (i > 0).astype(F32)
    has_next = (i < n - 1).astype(F32)

    pa, pb = _conv_inputs(prev_ref[0])
    za_s[0:HALO] = pa * has_prev
    zb_s[0, 0:HALO] = pb * has_prev
    ca, cb = _conv_inputs(cur_ref[0])
    za_s[HALO:HALO + ts] = ca
    zb_s[0, HALO:HALO + ts] = cb
    na, nb = _conv_inputs(next_ref[0])
    za_s[HALO + ts:] = na * has_next
    zb_s[0, HALO + ts:] = nb * has_next
    n_shifted = ts + 2 * HALO - SUBLANES
    for b in range(1, SUBLANES):
        zb_s[b, 0:n_shifted] = zb_s[0, b:b + n_shifted]

    wa = wa_ref[...]
    wb = wb_ref[...]
    lane_head = lax.broadcasted_iota(jnp.int32, (1, GROUP_W), 1) // HEAD_W
    for r0 in range(0, ts, rc):
        acc = wa[0:1] * za_s[HALO + r0 - 1:HALO + r0 - 1 + rc]
        for j in range(1, CONV_A):
            acc = acc + wa[j:j + 1] * za_s[HALO + r0 - 1 + j:HALO + r0 - 1 + j + rc]
        ya = cur_ref[0, r0:r0 + rc, 0:GROUP_W] * acc
        o_ref[0, r0:r0 + rc, 0:GROUP_W] = _rms(ya, gg_ref[:, 0:GROUP_W]).astype(o_ref.dtype)

        base = HALO + r0 - CONV_B // 2
        acc = None
        for j in range(CONV_B):
            b, a = (base + j) % SUBLANES, (base + j) // SUBLANES * SUBLANES
            term = wb[j:j + 1] * zb_s[b, a:a + rc]
            acc = term if acc is None else acc + term
        yb = _layer_norm(acc + bb_ref[...], blg_ref[...], blb_ref[...])
        yb = yb * jax.nn.sigmoid(yb)
        o_ref[0, r0:r0 + rc, GROUP_W:2 * GROUP_W] = _rms(yb, gg_ref[:, GROUP_W:2 * GROUP_W]).astype(o_ref.dtype)

    for r0 in range(0, ts, CHUNK):
        z = jax.nn.gelu(cur_ref[0, r0:r0 + CHUNK, OFF_C:OFF_D])
        u = z[:, :GROUP_W]
        v = _layer_norm(z[:, GROUP_W:], slg_ref[...], slb_ref[...]).astype(BF16)
        mixed = sbias_ref[...]
        for h in range(N_HEADS):
            vh = v * (lane_head == h).astype(BF16)
            mixed = mixed + jnp.dot(sw_ref[h], vh, preferred_element_type=F32)
        yc = u * mixed
        o_ref[0, r0:r0 + CHUNK, 2 * GROUP_W:3 * GROUP_W] = _rms(yc, gg_ref[:, 2 * GROUP_W:3 * GROUP_W]).astype(o_ref.dtype)


def _conv_mixer(p, wa, wb, bb, blg, blb, slg, slb, sw_bf16, sbias, gg, ts):
    b, s, _ = p.shape
    rc = min(64, ts)
    hb = ts // HALO
    n_halo = s // HALO
    row = lambda a: a.reshape(1, -1)
    const = lambda *shape: pl.BlockSpec(shape, lambda bi, i: (0,) * len(shape))
    return pl.pallas_call(
        functools.partial(_mixer_body, ts=ts, rc=rc),
        grid=(b, s // ts),
        in_specs=[pl.BlockSpec((1, HALO, OFF_D), lambda bi, i: (bi, jnp.maximum(i * hb - 1, 0), 0)),
                  pl.BlockSpec((1, ts, OFF_D), lambda bi, i: (bi, i, 0)),
                  pl.BlockSpec((1, HALO, OFF_D), lambda bi, i: (bi, jnp.minimum((i + 1) * hb, n_halo - 1), 0)),
                  const(CONV_A, GROUP_W), const(CONV_B, GROUP_W), const(1, GROUP_W),
                  const(1, GROUP_W), const(1, GROUP_W), const(1, GROUP_W), const(1, GROUP_W),
                  const(N_HEADS, CHUNK, CHUNK), const(CHUNK, GROUP_W), const(1, 3 * GROUP_W)],
        out_specs=pl.BlockSpec((1, ts, 3 * GROUP_W), lambda bi, i: (bi, i, 0)),
        out_shape=jax.ShapeDtypeStruct((b, s, 3 * GROUP_W), BF16),
        scratch_shapes=[pltpu.VMEM((ts + 2 * HALO, GROUP_W), F32),
                        pltpu.VMEM((SUBLANES, ts + 2 * HALO, GROUP_W), F32)],
        compiler_params=_params("parallel", "parallel"),
        name="conv_mixer",
    )(p, p, p, wa, wb, row(bb), row(blg), row(blb), row(slg), row(slb), sw_bf16, sbias, row(gg))


def _head_masks():
    lane_head = lax.broadcasted_iota(jnp.int32, (1, GROUP_W), 1) // HEAD_W
    return [(lane_head == h).astype(BF16) for h in range(N_HEADS)]


_NT = (((1,), (1,)), ((), ()))


def _attend(q, masks, keys, values, biases, acc):
    for h in range(N_HEADS):
        qh = q * masks[h]
        scores = []
        for k, bias in zip(keys, biases):
            s = lax.dot_general(qh, k, _NT, preferred_element_type=F32)
            scores.append(s if bias is None else s + bias(h))
        m = scores[0].max(axis=-1, keepdims=True)
        for s in scores[1:]:
            m = jnp.maximum(m, s.max(axis=-1, keepdims=True))
        probs = [jnp.exp(s - m) for s in scores]
        denom = probs[0].sum(axis=-1, keepdims=True)
        for p in probs[1:]:
            denom = denom + p.sum(axis=-1, keepdims=True)
        o = jnp.dot(probs[0].astype(BF16), values[0](h), preferred_element_type=F32)
        for p, v in zip(probs[1:], values[1:]):
            o = o + jnp.dot(p.astype(BF16), v(h), preferred_element_type=F32)
        acc = acc + o / denom
    return acc


def _nbr_attn_body(q_ref, kp_ref, kc_ref, kn_ref, kx_ref, bias_ref, gg_ref, o_ref,
                   k_s, v_s, kx_s, vx_s, *, starts):
    masks = _head_masks()
    n_half = kp_ref.shape[1]
    n_cur = kc_ref.shape[1]
    pieces = ((kp_ref, 0, n_half), (kc_ref, n_half, n_cur), (kn_ref, n_half + n_cur, n_half))
    for ref, off, n in pieces:
        k_s[off:off + n] = ref[0, :, 0:GROUP_W]
        for h in range(N_HEADS):
            v_s[h, off:off + n] = ref[0, :, GROUP_W:] * masks[h]
    kx_s[...] = kx_ref[0, :, 0:GROUP_W]
    for h in range(N_HEADS):
        vx_s[h] = kx_ref[0, :, GROUP_W:] * masks[h]

    j = pl.program_id(1)
    nb = pl.num_programs(1)
    n_win = bias_ref.shape[3]
    tq = q_ref.shape[1] // len(starts[0])
    for sb, (first, mid, last) in enumerate(zip(*starts)):
        st = jnp.where(j == 0, first, jnp.where(j == nb - 1, last, mid)) * GRID_W
        st = pl.multiple_of(st, GRID_W)
        r0 = sb * tq
        acc = _attend(
            q_ref[0, r0:r0 + tq, :], masks,
            keys=[k_s[pl.ds(st, n_win)], kx_s[...]],
            values=[lambda h: v_s[h, pl.ds(st, n_win)], lambda h: vx_s[h]],
            biases=[lambda h: bias_ref[0, h, r0:r0 + tq, :], None],
            acc=jnp.zeros((tq, GROUP_W), F32))
        o_ref[0, r0:r0 + tq, :] = _rms(acc, gg_ref[...]).astype(o_ref.dtype)


SUB_Q_ROWS = 4
SUB_KEY_ROWS = 12


def _block_kinds(rows):
    nb = rows // Q_BLOCK_ROWS
    return (0, min(1, nb - 1), nb - 1)


def _sub_window_starts(rows):
    n_key_rows = KEY_ROWS_BEFORE + Q_BLOCK_ROWS + KEY_ROWS_AFTER
    starts = []
    for jv in _block_kinds(rows):
        per_sub = []
        for sb in range(Q_BLOCK_ROWS // SUB_Q_ROWS):
            r = Q_BLOCK_ROWS * jv + SUB_Q_ROWS * sb + np.arange(SUB_Q_ROWS)
            ks = np.clip(r - WIN_R // 2, 0, rows - WIN_R) - Q_BLOCK_ROWS * jv + KEY_ROWS_BEFORE
            start = int(min(ks.min(), n_key_rows - SUB_KEY_ROWS))
            assert start <= ks.min() and ks.max() + WIN_R <= start + SUB_KEY_ROWS
            per_sub.append(start)
        starts.append(tuple(per_sub))
    return tuple(starts)


def _window_bias(rpb_l, rows):
    n_key_rows = KEY_ROWS_BEFORE + Q_BLOCK_ROWS + KEY_ROWS_AFTER
    sub_starts = _sub_window_starts(rows)
    i = np.arange(Q_BLOCK_ROWS)[:, None, None, None]
    c = np.arange(GRID_W)[None, :, None, None]
    kr = np.arange(-KEY_ROWS_BEFORE, Q_BLOCK_ROWS + KEY_ROWS_AFTER)[None, None, :, None]
    kc = np.arange(GRID_W)[None, None, None, :]
    shape = (Q_BLOCK_ROWS, GRID_W, n_key_rows, GRID_W)
    edge = GRID_W - WIN_C
    cols = jnp.pad(rpb_l, ((0, 0), (0, 0), (edge, edge)), mode="edge")
    cols = jnp.pad(cols, ((0, 0), (0, 0), (0, 1)))
    skew = jnp.broadcast_to(cols[:, :, None, :], cols.shape[:2] + (GRID_W, 2 * GRID_W))
    skew = skew.reshape(cols.shape[:2] + (2 * GRID_W * GRID_W,))[:, :, :GRID_W * (2 * GRID_W - 1)]
    by_col = skew.reshape(cols.shape[:2] + (GRID_W, 2 * GRID_W - 1))[..., GRID_W - 1:]
    lo = KEY_ROWS_BEFORE
    hi = Q_BLOCK_ROWS + KEY_ROWS_AFTER - WIN_R
    by_col = jnp.pad(by_col, ((0, 0), (lo, hi), (0, 0), (0, 0)))
    tables = []
    for jv, starts in zip(_block_kinds(rows), sub_starts):
        r = Q_BLOCK_ROWS * jv + i
        ks = np.clip(r - WIN_R // 2, 0, rows - WIN_R)
        krow = Q_BLOCK_ROWS * jv + kr
        c_start = np.clip(c - WIN_C // 2, 0, GRID_W - WIN_C)
        valid = np.broadcast_to((krow >= ks) & (krow < ks + WIN_R) & (krow >= 0) & (krow < rows)
                                & (kc >= c_start) & (kc < c_start + WIN_C), shape)
        n_win = SUB_KEY_ROWS * GRID_W
        per_row, per_row_valid = [], []
        for ii in range(Q_BLOCK_ROWS):
            st = starts[ii // SUB_Q_ROWS]
            first = WIN_R - 1 - ii + st
            piece = by_col[:, first:first + SUB_KEY_ROWS].transpose(0, 2, 1, 3)
            per_row.append(piece.reshape(N_HEADS, GRID_W, n_win))
            per_row_valid.append(valid[ii, :, st:st + SUB_KEY_ROWS, :].reshape(GRID_W, n_win))
        values = jnp.concatenate(per_row, axis=1)
        tables.append(jnp.where(np.concatenate(per_row_valid)[None], values, NEG_INF))
    return jnp.stack(tables)


def _nbr_attention(q, kv, kv_ctx, bias, gg):
    b, s, _ = q.shape
    n_ctx = kv_ctx.shape[1]
    tb = Q_BLOCK_ROWS * GRID_W
    half = KEY_ROWS_BEFORE * GRID_W
    nb = s // tb
    assert nb >= 2 and KEY_ROWS_BEFORE == KEY_ROWS_AFTER and tb == 2 * half
    n_loc = tb + 2 * half
    n_win = SUB_KEY_ROWS * GRID_W
    variant = lambda j: jnp.where(j == 0, 0, jnp.where(j == nb - 1, 2, 1))
    return pl.pallas_call(
        functools.partial(_nbr_attn_body, starts=_sub_window_starts(s // GRID_W)),
        grid=(b, nb),
        in_specs=[pl.BlockSpec((1, tb, GROUP_W), lambda bi, j: (bi, j, 0)),
                  pl.BlockSpec((1, half, 2 * GROUP_W), lambda bi, j: (bi, jnp.maximum(2 * j - 1, 0), 0)),
                  pl.BlockSpec((1, tb, 2 * GROUP_W), lambda bi, j: (bi, j, 0)),
                  pl.BlockSpec((1, half, 2 * GROUP_W), lambda bi, j: (bi, jnp.minimum(2 * j + 2, 2 * nb - 1), 0)),
                  pl.BlockSpec((1, n_ctx, 2 * GROUP_W), lambda bi, j: (bi, 0, 0)),
                  pl.BlockSpec((1, N_HEADS, tb, n_win), lambda bi, j: (variant(j), 0, 0, 0)),
                  pl.BlockSpec((1, GROUP_W), lambda bi, j: (0, 0))],
        out_specs=pl.BlockSpec((1, tb, GROUP_W), lambda bi, j: (bi, j, 0)),
        out_shape=jax.ShapeDtypeStruct((b, s, GROUP_W), BF16),
        scratch_shapes=[pltpu.VMEM((n_loc, GROUP_W), BF16),
                        pltpu.VMEM((N_HEADS, n_loc, GROUP_W), BF16),
                        pltpu.VMEM((n_ctx, GROUP_W), BF16),
                        pltpu.VMEM((N_HEADS, n_ctx, GROUP_W), BF16)],
        compiler_params=_params("parallel", "arbitrary"),
        name="nbr_attention",
    )(q, kv, kv, kv, kv_ctx, bias, gg.reshape(1, GROUP_W))


def _ctx_attn_body(q_ref, kx_ref, gg_ref, o_ref, vx_s):
    masks = _head_masks()
    for h in range(N_HEADS):
        vx_s[h] = kx_ref[0, :, GROUP_W:] * masks[h]
    tq = q_ref.shape[1]
    acc = _attend(q_ref[0], masks, keys=[kx_ref[0, :, 0:GROUP_W]], values=[lambda h: vx_s[h]],
                  biases=[None], acc=jnp.zeros((tq, GROUP_W), F32))
    o_ref[0] = _rms(acc, gg_ref[...]).astype(o_ref.dtype)


def _ctx_attention(q, kv_ctx, gg):
    b, n_ctx, _ = q.shape
    return pl.pallas_call(
        _ctx_attn_body,
        grid=(b,),
        in_specs=[pl.BlockSpec((1, n_ctx, GROUP_W), lambda bi: (bi, 0, 0)),
                  pl.BlockSpec((1, n_ctx, 2 * GROUP_W), lambda bi: (bi, 0, 0)),
                  pl.BlockSpec((1, GROUP_W), lambda bi: (0, 0))],
        out_specs=pl.BlockSpec((1, n_ctx, GROUP_W), lambda bi: (bi, 0, 0)),
        out_shape=jax.ShapeDtypeStruct((b, n_ctx, GROUP_W), BF16),
        scratch_shapes=[pltpu.VMEM((N_HEADS, n_ctx, GROUP_W), BF16)],
        compiler_params=_params("parallel"),
        name="ctx_attention",
    )(q, kv_ctx, gg.reshape(1, GROUP_W))


def _mixer_residual(abc_ref, dn_ref, w_ref, x_ref, m):
    y = (jnp.dot(abc_ref[0], w_ref[0:3 * GROUP_W], preferred_element_type=F32)
         + jnp.dot(dn_ref[0], w_ref[3 * GROUP_W:], preferred_element_type=F32))
    return x_ref[0] + m[M_G1:M_G1 + 1] * y


def _merge_router_body(abc_ref, dn_ref, w_ref, x_ref, mod_ref, g_ref, rw_ref, rb_ref, xo_ref, h_ref, info_ref):
    m = mod_ref[0]
    xn = _mixer_residual(abc_ref, dn_ref, w_ref, x_ref, m)
    xo_ref[0] = xn
    h = _modulated_norm(xn, g_ref[...], m, M_SH2, M_SC2)
    n_sub = h.shape[1] // LANES
    for s in range(n_sub):
        h_ref[0, pl.ds(s, h.shape[0], stride=n_sub), :] = h[:, s * LANES:(s + 1) * LANES]
    logits = jnp.dot(h.astype(BF16), rw_ref[...], preferred_element_type=F32) + rb_ref[...]
    lane = lax.broadcasted_iota(jnp.int32, logits.shape, 1).astype(F32)
    m1 = logits.max(axis=-1, keepdims=True)
    i1 = jnp.where(logits == m1, lane, float(LANES)).min(axis=-1, keepdims=True)
    rest_logits = jnp.where(lane == i1, NEG_INF, logits)
    m2 = rest_logits.max(axis=-1, keepdims=True)
    i2 = jnp.where(rest_logits == m2, lane, float(LANES)).min(axis=-1, keepdims=True)
    e2 = jnp.exp(m2 - m1)
    den = 1.0 + e2
    info = jnp.where(lane == 0, i1, jnp.where(lane == 1, i2,
           jnp.where(lane == 2, 1.0 / den, jnp.where(lane == 3, e2 / den, 0.0))))
    info_ref[0] = info


def _merge_router(abc, dn, w_out_bf16, x, mod, mod_row, g2, router_w, router_b, tm):
    b, s, d = x.shape
    const = lambda *shape: pl.BlockSpec(shape, lambda bi, i: (0,) * len(shape))
    return pl.pallas_call(
        _merge_router_body,
        grid=(b, s // tm),
        in_specs=[pl.BlockSpec((1, tm, 3 * GROUP_W), lambda bi, i: (bi, i, 0)),
                  pl.BlockSpec((1, tm, GROUP_W), lambda bi, i: (bi, i, 0)),
                  const(4 * GROUP_W, d),
                  pl.BlockSpec((1, tm, d), lambda bi, i: (bi, i, 0)),
                  pl.BlockSpec((1, 6, d), lambda bi, i: (mod_row(bi), 0, 0)),
                  const(1, d), const(d, LANES), const(1, LANES)],
        out_specs=[pl.BlockSpec((1, tm, d), lambda bi, i: (bi, i, 0)),
                   pl.BlockSpec((1, tm * d // LANES, LANES), lambda bi, i: (bi, i, 0)),
                   pl.BlockSpec((1, tm, LANES), lambda bi, i: (bi, i, 0))],
        out_shape=[jax.ShapeDtypeStruct((b, s, d), F32),
                   jax.ShapeDtypeStruct((b, s * d // LANES, LANES), F32),
                   jax.ShapeDtypeStruct((b, s, LANES), F32)],
        compiler_params=_params("parallel", "parallel"),
        name="merge_router",
    )(abc, dn, w_out_bf16, x, mod, g2.reshape(1, d), router_w, router_b)


def _swiglu_chunk(xb, w1_ref, w3_ref, w2_ref, c0, cw):
    a = jnp.dot(xb, w1_ref[:, c0:c0 + cw], preferred_element_type=F32)
    g = jnp.dot(xb, w3_ref[:, c0:c0 + cw], preferred_element_type=F32)
    act = (a * jax.nn.sigmoid(a) * g).astype(BF16)
    return jnp.dot(act, w2_ref[c0:c0 + cw, :], preferred_element_type=F32)


def _hidden_chunks(f, tf):
    return [(c0, min(tf, f - c0)) for c0 in range(0, f, tf)]


def _transition_body(abc_ref, dn_ref, wo_ref, x_ref, mod_ref, g2_ref, w1_ref, w3_ref, w2_ref,
                     modn_ref, g1n_ref, win_ref, xo_ref, p_ref, q_ref, kv_ref, acc_ref, *, tf):
    m = mod_ref[0]
    xn = _mixer_residual(abc_ref, dn_ref, wo_ref, x_ref, m)
    h = _modulated_norm(xn, g2_ref[...], m, M_SH2, M_SC2).astype(BF16)
    for n, (c0, cw) in enumerate(_hidden_chunks(w1_ref.shape[1], tf)):
        part = _swiglu_chunk(h, w1_ref, w3_ref, w2_ref, c0, cw)
        if n == 0:
            acc_ref[...] = part
        else:
            acc_ref[...] += part
    x_next = xn + m[M_G2:M_G2 + 1] * acc_ref[...]
    xo_ref[0] = x_next
    _input_projection(x_next, g1n_ref, modn_ref[0], win_ref, p_ref, q_ref, kv_ref)


def _layer_transition(abc, dn, w_out_bf16, x, mod, mod_next, mod_row, g2, ffn_w, g1_next, w_in_next, tm, tf):
    b, s, d = x.shape
    w1, w3, w2 = ffn_w
    f = w1.shape[1]
    resident = lambda *shape: pl.BlockSpec(shape, lambda bi, i: (0,) * len(shape), pipeline_mode=pl.Buffered(1))
    rows = lambda width: pl.BlockSpec((1, tm, width), lambda bi, i: (bi, i, 0))
    mod_spec = pl.BlockSpec((1, 6, d), lambda bi, i: (mod_row(bi), 0, 0))
    return pl.pallas_call(
        functools.partial(_transition_body, tf=tf),
        grid=(b, s // tm),
        in_specs=[rows(3 * GROUP_W), rows(GROUP_W), resident(4 * GROUP_W, d), rows(d), mod_spec, resident(1, d),
                  resident(d, f), resident(d, f), resident(f, d),
                  mod_spec, resident(1, d), resident(d, IN_COLS)],
        out_specs=[rows(d), rows(OFF_D), rows(GROUP_W), rows(2 * GROUP_W)],
        out_shape=[jax.ShapeDtypeStruct((b, s, d), F32),
                   jax.ShapeDtypeStruct((b, s, OFF_D), F32),
                   jax.ShapeDtypeStruct((b, s, GROUP_W), BF16),
                   jax.ShapeDtypeStruct((b, s, 2 * GROUP_W), BF16)],
        scratch_shapes=[pltpu.VMEM((tm, d), F32)],
        compiler_params=_params("parallel", "parallel"),
        name="layer_transition",
    )(abc, dn, w_out_bf16, x, mod, g2.reshape(1, d), w1, w3, w2, mod_next, g1_next.reshape(1, d), w_in_next)


WINDOW_ALIGN = 16


LOC_RADIX = 128


def _route_plan(expert_idx, tm, tt):
    n_assign = expert_idx.size
    experts = jnp.arange(N_EXPERTS, dtype=jnp.int32)[:, None]
    onehot = (expert_idx.reshape(1, n_assign) == experts).astype(jnp.int32)
    csum = jnp.cumsum(onehot, axis=1)
    counts = csum[:, -1]
    rank = jnp.sum((csum - onehot) * onehot, axis=0)
    padded = (counts + tm - 1) // tm * tm
    pad_end = jnp.cumsum(padded)
    pad_start = pad_end - padded
    slot = jnp.sum(onehot * pad_start[:, None], axis=0) + rank
    total = n_assign + N_EXPERTS * tm
    n_blk = total // tm
    slot_tok = jnp.zeros((total,), jnp.int32).at[slot].set(jnp.arange(n_assign, dtype=jnp.int32) // 2)
    n_active = pad_end[-1] // tm
    blk = jnp.arange(n_blk, dtype=jnp.int32)
    blk_exp = jnp.minimum(jnp.searchsorted(pad_end, blk * tm, side="right"), N_EXPERTS - 1).astype(jnp.int32)
    blk_exp = jnp.where(blk < n_active, blk_exp, blk_exp[jnp.maximum(n_active - 1, 0)])

    wa = tt + WINDOW_ALIGN
    assert N_EXPERTS * wa <= LOC_RADIX * LOC_RADIX
    n_tiles = n_assign // (2 * tt)
    before = jnp.concatenate([jnp.zeros((N_EXPERTS, 1), jnp.int32), csum[:, 2 * tt - 1::2 * tt][:, :-1]], axis=1)
    first = pad_start[:, None] + before
    win_base = jnp.minimum(first // WINDOW_ALIGN * WINDOW_ALIGN, total - wa)
    win_row0 = experts * wa - win_base
    win_loc = slot + jnp.sum(onehot.reshape(N_EXPERTS, n_tiles, 2 * tt) * win_row0[:, :, None],
                             axis=0).reshape(n_assign)
    loc = win_loc.reshape(n_tiles, tt, 2)
    digits = [loc[:, :, 0] // LOC_RADIX, loc[:, :, 0] % LOC_RADIX, loc[:, :, 1] // LOC_RADIX, loc[:, :, 1] % LOC_RADIX]
    loc_digits = jnp.stack(digits + [jnp.zeros_like(digits[0])] * 4, axis=1).astype(F32)
    return (slot_tok.reshape(n_blk, tm), blk_exp, n_active.astype(jnp.int32).reshape(1),
            win_base.T.reshape(n_tiles * N_EXPERTS).astype(jnp.int32), loc_digits)


ROW_ISSUE_UNROLL = 8


def _moe_body(bexp_ref, nact_ref, tok_hbm, h_hbm, w1_hbm, w3_hbm, w2_hbm, o_ref,
              tok_s, x_s, xb_s, acc_s, w1_s, w3_s, w2_s, st1_s, st3_s, st2_s,
              sem_idx, sem_row, sem_w, *, tm, tf):
    i = pl.program_id(0)
    n_active = nact_ref[0]
    chunks = _hidden_chunks(w1_s.shape[1], tf)
    rows_per_chunk = -(-tm // len(chunks))

    def idx_copy(step):
        return pltpu.make_async_copy(tok_hbm.at[step], tok_s.at[step % 2], sem_idx.at[step % 2])

    n_sub = h_hbm.shape[1]

    def row_copy(slot, r, tok):
        return pltpu.make_async_copy(h_hbm.at[tok], x_s.at[slot, pl.ds(r * n_sub, n_sub)], sem_row.at[slot])

    @pl.when(i == 0)
    def _():
        idx_copy(0).start()
        idx_copy(0).wait()

        def issue(r, c):
            row_copy(0, r, tok_s[0, r]).start()
            return c

        lax.fori_loop(0, tm, issue, 0, unroll=ROW_ISSUE_UNROLL)
        idx_copy(1).start()

    expert = bexp_ref[i]
    new_expert = (i == 0) | (expert != bexp_ref[jnp.maximum(i - 1, 0)])

    @pl.when(new_expert & (i < n_active))
    def _():
        def weight_copies(n):
            c0, cw = chunks[n]
            slot = n % 2
            return (pltpu.make_async_copy(w1_hbm.at[expert, :, pl.ds(c0, cw)], st1_s.at[slot, :, pl.ds(0, cw)],
                                          sem_w.at[slot]),
                    pltpu.make_async_copy(w3_hbm.at[expert, :, pl.ds(c0, cw)], st3_s.at[slot, :, pl.ds(0, cw)],
                                          sem_w.at[slot]),
                    pltpu.make_async_copy(w2_hbm.at[expert, pl.ds(c0, cw), :], st2_s.at[slot, pl.ds(0, cw), :],
                                          sem_w.at[slot]))

        for cp in weight_copies(0):
            cp.start()
        for n, (c0, cw) in enumerate(chunks):
            if n + 1 < len(chunks):
                for cp in weight_copies(n + 1):
                    cp.start()
            for cp in weight_copies(n):
                cp.wait()
            slot = n % 2
            w1_s[:, c0:c0 + cw] = st1_s[slot, :, 0:cw].astype(BF16)
            w3_s[:, c0:c0 + cw] = st3_s[slot, :, 0:cw].astype(BF16)
            w2_s[c0:c0 + cw, :] = st2_s[slot, 0:cw, :].astype(BF16)

    @pl.when(i <= n_active)
    def _():
        slot = i % 2
        pltpu.make_async_copy(x_s.at[slot], x_s.at[slot], sem_row.at[slot]).wait()
        for s in range(n_sub):
            xb_s[:, s * LANES:(s + 1) * LANES] = x_s[slot, pl.ds(s, tm, stride=n_sub), :].astype(BF16)

    @pl.when(i < n_active)
    def _():
        nxt = (i + 1) % 2
        idx_copy(i + 1).wait()
        xb = xb_s[...]
        for n, (c0, cw) in enumerate(chunks):
            part = _swiglu_chunk(xb, w1_s, w3_s, w2_s, c0, cw)
            if n == 0:
                acc_s[...] = part
            else:
                acc_s[...] += part
            for r in range(n * rows_per_chunk, min((n + 1) * rows_per_chunk, tm)):
                row_copy(nxt, r, tok_s[nxt, r]).start()

        @pl.when(i + 2 <= n_active)
        def _():
            idx_copy(i + 2).start()

    @pl.when(i >= n_active)
    def _():
        acc_s[...] = jnp.zeros(acc_s.shape, acc_s.dtype)

    o_ref[...] = acc_s[...].astype(o_ref.dtype)


def _moe_experts(h, slot_tok, blk_exp, n_active, w1, w3, w2, tm, tf):
    t, n_sub, _ = h.shape
    d = n_sub * LANES
    n_blk = slot_tok.shape[0]
    f = w1.shape[2]
    any_space = pl.BlockSpec(memory_space=pl.ANY)
    grid_spec = pltpu.PrefetchScalarGridSpec(
        num_scalar_prefetch=2,
        grid=(n_blk,),
        in_specs=[any_space, any_space, any_space, any_space, any_space],
        out_specs=pl.BlockSpec((tm, d), lambda i, bexp, nact: (i, 0)),
        scratch_shapes=[pltpu.SMEM((2, tm), jnp.int32),
                        pltpu.VMEM((2, tm * n_sub, LANES), F32),
                        pltpu.VMEM((tm, d), BF16),
                        pltpu.VMEM((tm, d), F32),
                        pltpu.VMEM((d, f), BF16), pltpu.VMEM((d, f), BF16), pltpu.VMEM((f, d), BF16),
                        pltpu.VMEM((2, d, tf), F32), pltpu.VMEM((2, d, tf), F32), pltpu.VMEM((2, tf, d), F32),
                        pltpu.SemaphoreType.DMA((2,)),
                        pltpu.SemaphoreType.DMA((2,)),
                        pltpu.SemaphoreType.DMA((2,))])
    return pl.pallas_call(
        functools.partial(_moe_body, tm=tm, tf=tf),
        grid_spec=grid_spec,
        out_shape=jax.ShapeDtypeStruct((n_blk * tm, d), BF16),
        compiler_params=_params("arbitrary"),
        name="moe_experts",
    )(blk_exp, n_active, slot_tok, h, w1, w3, w2)


COMBINE_SLOTS = 3


def _combine_body(base_ref, y_hbm, loc_ref, info_ref, x_ref, mod_ref, g_ref, o_ref, win_s, sem, *, wa):
    i = pl.program_id(0)
    n = pl.num_programs(0)

    def window_copy(step, e):
        slot = step % COMBINE_SLOTS
        base = pl.multiple_of(base_ref[step * N_EXPERTS + e], WINDOW_ALIGN)
        return pltpu.make_async_copy(y_hbm.at[pl.ds(base, wa)], win_s.at[slot, pl.ds(e * wa, wa)], sem.at[slot])

    def start_windows(step):
        for e in range(N_EXPERTS):
            window_copy(step, e).start()

    @pl.when(i == 0)
    def _():
        for step in range(COMBINE_SLOTS - 1):
            @pl.when(step < n)
            def _():
                start_windows(step)

    @pl.when(i + COMBINE_SLOTS - 1 < n)
    def _():
        start_windows(i + COMBINE_SLOTS - 1)

    for e in range(N_EXPERTS):
        window_copy(i, e).wait()

    info = info_ref[...]
    win = win_s[i % COMBINE_SLOTS]
    tt = info.shape[0]
    eye = (lax.broadcasted_iota(jnp.int32, (tt, tt), 0) == lax.broadcasted_iota(jnp.int32, (tt, tt), 1))
    digits = lax.dot_general(jnp.where(eye, 1.0, 0.0).astype(BF16), loc_ref[0].astype(BF16), _NT,
                             preferred_element_type=F32)
    loc0 = digits[:, 0:1] * LOC_RADIX + digits[:, 1:2]
    loc1 = digits[:, 2:3] * LOC_RADIX + digits[:, 3:4]
    row = lax.broadcasted_iota(jnp.int32, (tt, win.shape[0]), 1).astype(F32)
    pick0 = jnp.where(row == loc0, 1.0, 0.0).astype(BF16)
    pick1 = jnp.where(row == loc1, 1.0, 0.0).astype(BF16)
    f = (info[:, 2:3] * jnp.dot(pick0, win, preferred_element_type=F32)
         + info[:, 3:4] * jnp.dot(pick1, win, preferred_element_type=F32))
    xn = x_ref[...] + mod_ref[0, M_G2:M_G2 + 1] * f
    o_ref[...] = _rms(xn, g_ref[...])


def _combine_final(win_base, y, loc_digits, info, x, mod, mod_row, final_g, tt):
    t, d = x.shape
    wa = tt + WINDOW_ALIGN
    grid_spec = pltpu.PrefetchScalarGridSpec(
        num_scalar_prefetch=1,
        grid=(t // tt,),
        in_specs=[pl.BlockSpec(memory_space=pl.ANY),
                  pl.BlockSpec((1, SUBLANES, tt), lambda i, base: (i, 0, 0)),
                  pl.BlockSpec((tt, LANES), lambda i, base: (i, 0)),
                  pl.BlockSpec((tt, d), lambda i, base: (i, 0)),
                  pl.BlockSpec((1, 6, d), lambda i, base: (mod_row(i), 0, 0)),
                  pl.BlockSpec((1, d), lambda i, base: (0, 0))],
        out_specs=pl.BlockSpec((tt, d), lambda i, base: (i, 0)),
        scratch_shapes=[pltpu.VMEM((COMBINE_SLOTS, N_EXPERTS * wa, d), y.dtype),
                        pltpu.SemaphoreType.DMA((COMBINE_SLOTS,))])
    return pl.pallas_call(
        functools.partial(_combine_body, wa=wa),
        grid_spec=grid_spec,
        out_shape=jax.ShapeDtypeStruct((t, d), F32),
        compiler_params=_params("arbitrary"),
        name="moe_combine_final",
    )(win_base, y, loc_digits, info, x, mod, final_g.reshape(1, d))


class _Tiles(NamedTuple):
    rows: int
    hidden: int
    expert_rows: int
    combine_rows: int


def _choose_tiles(seq):
    return _Tiles(rows=min(512, seq), hidden=512, expert_rows=512, combine_rows=min(128, seq))


def kernel(x, c, ctx, c_ctx, w_mod, b_mod, norm1_g, norm2_g, w_in, conv_a_w, conv_b_w, conv_b_b, conv_ln_g, conv_ln_b, sgu_ln_g, sgu_ln_b, sgu_w, sgu_b, rpb, group_g, w_out, ffn_w1, ffn_w3, ffn_w2, router_w, router_b, moe_w1, moe_w3, moe_w2, final_g):
    bsz, seq, d = x.shape
    n_ctx = ctx.shape[1]
    depth = w_mod.shape[0]
    rows = seq // GRID_W
    assert d == D_MODEL and seq % (Q_BLOCK_ROWS * GRID_W) == 0 and n_ctx % CHUNK == 0 and bsz + 1 <= 8
    assert depth == 2, "layer 0 dense with context, layer 1 (last) MoE without context"

    tiles = _choose_tiles(seq)
    tm = tiles.rows
    lat_row = lambda bi: bi
    ctx_row = lambda bi: bsz

    c8 = jnp.zeros((8, d), F32).at[:bsz].set(c).at[bsz].set(c_ctx)
    mod_all = _modulation(c8, w_mod, b_mod).reshape(depth, 8, 6, d)

    xl, xc = x, ctx
    proj = _norm_proj(xl, norm1_g[0], mod_all[0], lat_row, w_in[0].astype(BF16), tm)
    proj_c = _norm_proj(xc, norm1_g[0], mod_all[0], ctx_row, w_in[0].astype(BF16), n_ctx)
    out = None
    for l in range(depth):
        last = l == depth - 1
        mod = mod_all[l]
        w_out_b = w_out[l].astype(BF16)
        sw_b = sgu_w[l].astype(BF16)
        sbias = jnp.repeat(sgu_b[l].T, HEAD_W, axis=1)
        gg = group_g[l]
        conv_args = (conv_a_w[l], conv_b_w[l], conv_b_b[l], conv_ln_g[l], conv_ln_b[l],
                     sgu_ln_g[l], sgu_ln_b[l], sw_b, sbias, gg[:3 * GROUP_W])
        gg_d = gg[3 * GROUP_W:]

        (p, q, kv), (pc, qc, kvc) = proj, proj_c
        abc = _conv_mixer(p, *conv_args, ts=tm)
        dn = _nbr_attention(q, kv, kvc, _window_bias(rpb[l], rows), gg_d)

        if not last:
            abc_c = _conv_mixer(pc, *conv_args, ts=n_ctx)
            dn_c = _ctx_attention(qc, kvc, gg_d)
            ffn_w = tuple(w[l // 2].astype(BF16) for w in (ffn_w1, ffn_w3, ffn_w2))
            nxt = (mod_all[l + 1], norm1_g[l + 1], w_in[l + 1].astype(BF16))
            xl, *proj = _layer_transition(abc, dn, w_out_b, xl, mod, nxt[0], lat_row, norm2_g[l], ffn_w,
                                          nxt[1], nxt[2], tm, tiles.hidden)
            xc, *proj_c = _layer_transition(abc_c, dn_c, w_out_b, xc, mod, nxt[0], ctx_row, norm2_g[l], ffn_w,
                                            nxt[1], nxt[2], n_ctx, tiles.hidden)
        else:
            rw = jnp.zeros((d, LANES), F32).at[:, :N_EXPERTS].set(router_w[l // 2]).astype(BF16)
            rb = jnp.full((1, LANES), NEG_INF, F32).at[0, :N_EXPERTS].set(router_b[l // 2])
            xl, h, info = _merge_router(abc, dn, w_out_b, xl, mod, lat_row, norm2_g[l], rw, rb, tm)
            t = bsz * seq
            info = info.reshape(t, LANES)
            tt = tiles.combine_rows
            slot_tok, blk_exp, n_active, win_base, loc_digits = _route_plan(
                info[:, :2].astype(jnp.int32), tiles.expert_rows, tt)
            w1, w3, w2 = moe_w1[l // 2], moe_w3[l // 2], moe_w2[l // 2]
            y = _moe_experts(h.reshape(t, d // LANES, LANES), slot_tok, blk_exp, n_active, w1, w3, w2,
                             tiles.expert_rows, tiles.hidden)
            out = _combine_final(win_base, y, loc_digits, info, xl.reshape(t, d), mod, lambda i: i * tt // seq,
                                 final_g, tt).reshape(bsz, seq, d)
    return out
```

```python
import functools
from typing import NamedTuple

import numpy as np
import jax
import jax.numpy as jnp
from jax import lax
from jax.experimental import pallas as pl
from jax.experimental.pallas import tpu as pltpu

F32 = jnp.float32
BF16 = jnp.bfloat16

D_MODEL = 1024
GRID_W = 64
EPS = 1e-6
NEG_INF = -1e30
GROUP_W = 256
N_HEADS = 4
HEAD_W = GROUP_W // N_HEADS
CONV_A = 3
CONV_B = 31
CHUNK = 128
WIN_R = 8
WIN_C = 16
OFF_B = 3 * GROUP_W
OFF_C = OFF_B + 2 * GROUP_W
OFF_D = OFF_C + 2 * GROUP_W
OFF_KV = OFF_D + GROUP_W
IN_COLS = OFF_KV + 2 * GROUP_W
N_EXPERTS = 8
LANES = 128
SUBLANES = 8
HALO = 16
Q_BLOCK_ROWS = 8
KEY_ROWS_BEFORE = 4
KEY_ROWS_AFTER = 4
VMEM_LIMIT = 56 * 1024 * 1024

M_SH1, M_SC1, M_G1, M_SH2, M_SC2, M_G2 = range(6)


def _params(*dims):
    return pltpu.CompilerParams(dimension_semantics=dims, vmem_limit_bytes=VMEM_LIMIT)


def _rms(x, g):
    return x * lax.rsqrt(jnp.mean(x * x, axis=-1, keepdims=True) + EPS) * g


def _layer_norm(x, g, b):
    mu = jnp.mean(x, axis=-1, keepdims=True)
    xc = x - mu
    var = jnp.mean(xc * xc, axis=-1, keepdims=True)
    return xc * lax.rsqrt(var + EPS) * g + b


def _mod_body(c_ref, w_ref, b_ref, o_ref):
    s = c_ref[...]
    s = s * jax.nn.sigmoid(s)
    o_ref[0] = jnp.dot(s, w_ref[0], preferred_element_type=F32,
                       precision=lax.Precision.HIGHEST) + b_ref[0]


def _modulation(c8, w_mod, b_mod):
    n_layers, d, six_d = w_mod.shape
    return pl.pallas_call(
        _mod_body,
        grid=(n_layers, six_d // d),
        in_specs=[pl.BlockSpec((8, d), lambda l, j: (0, 0)),
                  pl.BlockSpec((1, d, d), lambda l, j: (l, 0, j)),
                  pl.BlockSpec((1, 1, d), lambda l, j: (l, 0, j))],
        out_specs=pl.BlockSpec((1, 8, d), lambda l, j: (l, 0, j)),
        out_shape=jax.ShapeDtypeStruct((n_layers, 8, six_d), F32),
        compiler_params=_params("arbitrary", "arbitrary"),
        name="modulation",
    )(c8, w_mod, b_mod.reshape(n_layers, 1, six_d))


def _modulated_norm(x, g, m, shift, scale):
    return _rms(x, g) * (1.0 + m[scale:scale + 1]) + m[shift:shift + 1]


def _input_projection(x, g_ref, m, w_ref, p_ref, q_ref, kv_ref):
    hb = _modulated_norm(x, g_ref[...], m, M_SH1, M_SC1).astype(BF16)
    p_ref[0] = jnp.dot(hb, w_ref[:, :OFF_D], preferred_element_type=F32)
    q = jnp.dot(hb, w_ref[:, OFF_D:OFF_KV], preferred_element_type=F32)
    q_ref[0] = (q * (HEAD_W ** -0.5)).astype(BF16)
    kv_ref[0] = jnp.dot(hb, w_ref[:, OFF_KV:], preferred_element_type=F32).astype(BF16)


def _norm_proj_body(x_ref, g_ref, mod_ref, w_ref, p_ref, q_ref, kv_ref):
    _input_projection(x_ref[0], g_ref, mod_ref[0], w_ref, p_ref, q_ref, kv_ref)


def _norm_proj(x, g, mod, mod_row, w_in_bf16, tm):
    b, s, d = x.shape
    return pl.pallas_call(
        _norm_proj_body,
        grid=(b, s // tm),
        in_specs=[pl.BlockSpec((1, tm, d), lambda bi, i: (bi, i, 0)),
                  pl.BlockSpec((1, d), lambda bi, i: (0, 0)),
                  pl.BlockSpec((1, 6, d), lambda bi, i: (mod_row(bi), 0, 0)),
                  pl.BlockSpec((d, IN_COLS), lambda bi, i: (0, 0))],
        out_specs=[pl.BlockSpec((1, tm, OFF_D), lambda bi, i: (bi, i, 0)),
                   pl.BlockSpec((1, tm, GROUP_W), lambda bi, i: (bi, i, 0)),
                   pl.BlockSpec((1, tm, 2 * GROUP_W), lambda bi, i: (bi, i, 0))],
        out_shape=[jax.ShapeDtypeStruct((b, s, OFF_D), F32),
                   jax.ShapeDtypeStruct((b, s, GROUP_W), BF16),
                   jax.ShapeDtypeStruct((b, s, 2 * GROUP_W), BF16)],
        compiler_params=_params("parallel", "parallel"),
        name="norm_proj",
    )(x, g.reshape(1, d), mod, w_in_bf16)


def _conv_inputs(blk):
    za = blk[:, GROUP_W:2 * GROUP_W] * blk[:, 2 * GROUP_W:3 * GROUP_W]
    zb = blk[:, OFF_B:OFF_B + GROUP_W] * jax.nn.sigmoid(blk[:, OFF_B + GROUP_W:OFF_C])
    return za, zb


def _mixer_body(prev_ref, cur_ref, next_ref, wa_ref, wb_ref, bb_ref, blg_ref, blb_ref,
                slg_ref, slb_ref, sw_ref, sbias_ref, gg_ref, o_ref, za_s, zb_s, *, ts, rc):
    i = pl.program_id(1)
    n = pl.num_programs(1)
    has_prev = (i > 0).astype(F32)
    has_next = (i < n - 1).astype(F32)

    pa, pb = _conv_inputs(prev_ref[0])
    za_s[0:HALO] = pa * has_prev
    zb_s[0, 0:HALO] = pb * has_prev
    ca, cb = _conv_inputs(cur_ref[0])
    za_s[HALO:HALO + ts] = ca
    zb_s[0, HALO:HALO + ts] = cb
    na, nb = _conv_inputs(next_ref[0])
    za_s[HALO + ts:] = na * has_next
    zb_s[0, HALO + ts:] = nb * has_next
    n_shifted = ts + 2 * HALO - SUBLANES
    for b in range(1, SUBLANES):
        zb_s[b, 0:n_shifted] = zb_s[0, b:b + n_shifted]

    wa = wa_ref[...]
    wb = wb_ref[...]
    lane_head = lax.broadcasted_iota(jnp.int32, (1, GROUP_W), 1) // HEAD_W
    for r0 in range(0, ts, rc):
        acc = wa[0:1] * za_s[HALO + r0 - 1:HALO + r0 - 1 + rc]
        for j in range(1, CONV_A):
            acc = acc + wa[j:j + 1] * za_s[HALO + r0 - 1 + j:HALO + r0 - 1 + j + rc]
        ya = cur_ref[0, r0:r0 + rc, 0:GROUP_W] * acc
        o_ref[0, r0:r0 + rc, 0:GROUP_W] = _rms(ya, gg_ref[:, 0:GROUP_W]).astype(o_ref.dtype)

        base = HALO + r0 - CONV_B // 2
        acc = None
        for j in range(CONV_B):
            b, a = (base + j) % SUBLANES, (base + j) // SUBLANES * SUBLANES
            term = wb[j:j + 1] * zb_s[b, a:a + rc]
            acc = term if acc is None else acc + term
        yb = _layer_norm(acc + bb_ref[...], blg_ref[...], blb_ref[...])
        yb = yb * jax.nn.sigmoid(yb)
        o_ref[0, r0:r0 + rc, GROUP_W:2 * GROUP_W] = _rms(yb, gg_ref[:, GROUP_W:2 * GROUP_W]).astype(o_ref.dtype)

    for r0 in range(0, ts, CHUNK):
        z = jax.nn.gelu(cur_ref[0, r0:r0 + CHUNK, OFF_C:OFF_D])
        u = z[:, :GROUP_W]
        v = _layer_norm(z[:, GROUP_W:], slg_ref[...], slb_ref[...]).astype(BF16)
        mixed = sbias_ref[...]
        for h in range(N_HEADS):
            vh = v * (lane_head == h).astype(BF16)
            mixed = mixed + jnp.dot(sw_ref[h], vh, preferred_element_type=F32)
        yc = u * mixed
        o_ref[0, r0:r0 + CHUNK, 2 * GROUP_W:3 * GROUP_W] = _rms(yc, gg_ref[:, 2 * GROUP_W:3 * GROUP_W]).astype(o_ref.dtype)


def _conv_mixer(p, wa, wb, bb, blg, blb, slg, slb, sw_bf16, sbias, gg, ts):
    b, s, _ = p.shape
    rc = min(64, ts)
    hb = ts // HALO
    n_halo = s // HALO
    row = lambda a: a.reshape(1, -1)
    const = lambda *shape: pl.BlockSpec(shape, lambda bi, i: (0,) * len(shape))
    return pl.pallas_call(
        functools.partial(_mixer_body, ts=ts, rc=rc),
        grid=(b, s // ts),
        in_specs=[pl.BlockSpec((1, HALO, OFF_D), lambda bi, i: (bi, jnp.maximum(i * hb - 1, 0), 0)),
                  pl.BlockSpec((1, ts, OFF_D), lambda bi, i: (bi, i, 0)),
                  pl.BlockSpec((1, HALO, OFF_D), lambda bi, i: (bi, jnp.minimum((i + 1) * hb, n_halo - 1), 0)),
                  const(CONV_A, GROUP_W), const(CONV_B, GROUP_W), const(1, GROUP_W),
                  const(1, GROUP_W), const(1, GROUP_W), const(1, GROUP_W), const(1, GROUP_W),
                  const(N_HEADS, CHUNK, CHUNK), const(CHUNK, GROUP_W), const(1, 3 * GROUP_W)],
        out_specs=pl.BlockSpec((1, ts, 3 * GROUP_W), lambda bi, i: (bi, i, 0)),
        out_shape=jax.ShapeDtypeStruct((b, s, 3 * GROUP_W), BF16),
        scratch_shapes=[pltpu.VMEM((ts + 2 * HALO, GROUP_W), F32),
                        pltpu.VMEM((SUBLANES, ts + 2 * HALO, GROUP_W), F32)],
        compiler_params=_params("parallel", "parallel"),
        name="conv_mixer",
    )(p, p, p, wa, wb, row(bb), row(blg), row(blb), row(slg), row(slb), sw_bf16, sbias, row(gg))


def _head_masks():
    lane_head = lax.broadcasted_iota(jnp.int32, (1, GROUP_W), 1) // HEAD_W
    return [(lane_head == h).astype(BF16) for h in range(N_HEADS)]


_NT = (((1,), (1,)), ((), ()))


def _attend(q, masks, keys, values, biases, acc):
    for h in range(N_HEADS):
        qh = q * masks[h]
        scores = []
        for k, bias in zip(keys, biases):
            s = lax.dot_general(qh, k, _NT, preferred_element_type=F32)
            scores.append(s if bias is None else s + bias(h))
        m = scores[0].max(axis=-1, keepdims=True)
        for s in scores[1:]:
            m = jnp.maximum(m, s.max(axis=-1, keepdims=True))
        probs = [jnp.exp(s - m) for s in scores]
        denom = probs[0].sum(axis=-1, keepdims=True)
        for p in probs[1:]:
            denom = denom + p.sum(axis=-1, keepdims=True)
        o = jnp.dot(probs[0].astype(BF16), values[0](h), preferred_element_type=F32)
        for p, v in zip(probs[1:], values[1:]):
            o = o + jnp.dot(p.astype(BF16), v(h), preferred_element_type=F32)
        acc = acc + o / denom
    return acc


def _nbr_attn_body(q_ref, kp_ref, kc_ref, kn_ref, kx_ref, bias_ref, gg_ref, o_ref,
                   k_s, v_s, kx_s, vx_s, *, starts):
    masks = _head_masks()
    n_half = kp_ref.shape[1]
    n_cur = kc_ref.shape[1]
    pieces = ((kp_ref, 0, n_half), (kc_ref, n_half, n_cur), (kn_ref, n_half + n_cur, n_half))
    for ref, off, n in pieces:
        k_s[off:off + n] = ref[0, :, 0:GROUP_W]
        for h in range(N_HEADS):
            v_s[h, off:off + n] = ref[0, :, GROUP_W:] * masks[h]
    kx_s[...] = kx_ref[0, :, 0:GROUP_W]
    for h in range(N_HEADS):
        vx_s[h] = kx_ref[0, :, GROUP_W:] * masks[h]

    j = pl.program_id(1)
    nb = pl.num_programs(1)
    n_win = bias_ref.shape[3]
    tq = q_ref.shape[1] // len(starts[0])
    for sb, (first, mid, last) in enumerate(zip(*starts)):
        st = jnp.where(j == 0, first, jnp.where(j == nb - 1, last, mid)) * GRID_W
        st = pl.multiple_of(st, GRID_W)
        r0 = sb * tq
        acc = _attend(
            q_ref[0, r0:r0 + tq, :], masks,
            keys=[k_s[pl.ds(st, n_win)], kx_s[...]],
            values=[lambda h: v_s[h, pl.ds(st, n_win)], lambda h: vx_s[h]],
            biases=[lambda h: bias_ref[0, h, r0:r0 + tq, :], None],
            acc=jnp.zeros((tq, GROUP_W), F32))
        o_ref[0, r0:r0 + tq, :] = _rms(acc, gg_ref[...]).astype(o_ref.dtype)


SUB_Q_ROWS = 4
SUB_KEY_ROWS = 12


def _block_kinds(rows):
    nb = rows // Q_BLOCK_ROWS
    return (0, min(1, nb - 1), nb - 1)


def _sub_window_starts(rows):
    n_key_rows = KEY_ROWS_BEFORE + Q_BLOCK_ROWS + KEY_ROWS_AFTER
    starts = []
    for jv in _block_kinds(rows):
        per_sub = []
        for sb in range(Q_BLOCK_ROWS // SUB_Q_ROWS):
            r = Q_BLOCK_ROWS * jv + SUB_Q_ROWS * sb + np.arange(SUB_Q_ROWS)
            ks = np.clip(r - WIN_R // 2, 0, rows - WIN_R) - Q_BLOCK_ROWS * jv + KEY_ROWS_BEFORE
            start = int(min(ks.min(), n_key_rows - SUB_KEY_ROWS))
            assert start <= ks.min() and ks.max() + WIN_R <= start + SUB_KEY_ROWS
            per_sub.append(start)
        starts.append(tuple(per_sub))
    return tuple(starts)


def _window_bias(rpb_l, rows):
    n_key_rows = KEY_ROWS_BEFORE + Q_BLOCK_ROWS + KEY_ROWS_AFTER
    sub_starts = _sub_window_starts(rows)
    i = np.arange(Q_BLOCK_ROWS)[:, None, None, None]
    c = np.arange(GRID_W)[None, :, None, None]
    kr = np.arange(-KEY_ROWS_BEFORE, Q_BLOCK_ROWS + KEY_ROWS_AFTER)[None, None, :, None]
    kc = np.arange(GRID_W)[None, None, None, :]
    shape = (Q_BLOCK_ROWS, GRID_W, n_key_rows, GRID_W)
    edge = GRID_W - WIN_C
    cols = jnp.pad(rpb_l, ((0, 0), (0, 0), (edge, edge)), mode="edge")
    cols = jnp.pad(cols, ((0, 0), (0, 0), (0, 1)))
    skew = jnp.broadcast_to(cols[:, :, None, :], cols.shape[:2] + (GRID_W, 2 * GRID_W))
    skew = skew.reshape(cols.shape[:2] + (2 * GRID_W * GRID_W,))[:, :, :GRID_W * (2 * GRID_W - 1)]
    by_col = skew.reshape(cols.shape[:2] + (GRID_W, 2 * GRID_W - 1))[..., GRID_W - 1:]
    lo = KEY_ROWS_BEFORE
    hi = Q_BLOCK_ROWS + KEY_ROWS_AFTER - WIN_R
    by_col = jnp.pad(by_col, ((0, 0), (lo, hi), (0, 0), (0, 0)))
    tables = []
    for jv, starts in zip(_block_kinds(rows), sub_starts):
        r = Q_BLOCK_ROWS * jv + i
        ks = np.clip(r - WIN_R // 2, 0, rows - WIN_R)
        krow = Q_BLOCK_ROWS * jv + kr
        c_start = np.clip(c - WIN_C // 2, 0, GRID_W - WIN_C)
        valid = np.broadcast_to((krow >= ks) & (krow < ks + WIN_R) & (krow >= 0) & (krow < rows)
                                & (kc >= c_start) & (kc < c_start + WIN_C), shape)
        n_win = SUB_KEY_ROWS * GRID_W
        per_row, per_row_valid = [], []
        for ii in range(Q_BLOCK_ROWS):
            st = starts[ii // SUB_Q_ROWS]
            first = WIN_R - 1 - ii + st
            piece = by_col[:, first:first + SUB_KEY_ROWS].transpose(0, 2, 1, 3)
            per_row.append(piece.reshape(N_HEADS, GRID_W, n_win))
            per_row_valid.append(valid[ii, :, st:st + SUB_KEY_ROWS, :].reshape(GRID_W, n_win))
        values = jnp.concatenate(per_row, axis=1)
        tables.append(jnp.where(np.concatenate(per_row_valid)[None], values, NEG_INF))
    return jnp.stack(tables)


def _nbr_attention(q, kv, kv_ctx, bias, gg):
    b, s, _ = q.shape
    n_ctx = kv_ctx.shape[1]
    tb = Q_BLOCK_ROWS * GRID_W
    half = KEY_ROWS_BEFORE * GRID_W
    nb = s // tb
    assert nb >= 2 and KEY_ROWS_BEFORE == KEY_ROWS_AFTER and tb == 2 * half
    n_loc = tb + 2 * half
    n_win = SUB_KEY_ROWS * GRID_W
    variant = lambda j: jnp.where(j == 0, 0, jnp.where(j == nb - 1, 2, 1))
    return pl.pallas_call(
        functools.partial(_nbr_attn_body, starts=_sub_window_starts(s // GRID_W)),
        grid=(b, nb),
        in_specs=[pl.BlockSpec((1, tb, GROUP_W), lambda bi, j: (bi, j, 0)),
                  pl.BlockSpec((1, half, 2 * GROUP_W), lambda bi, j: (bi, jnp.maximum(2 * j - 1, 0), 0)),
                  pl.BlockSpec((1, tb, 2 * GROUP_W), lambda bi, j: (bi, j, 0)),
                  pl.BlockSpec((1, half, 2 * GROUP_W), lambda bi, j: (bi, jnp.minimum(2 * j + 2, 2 * nb - 1), 0)),
                  pl.BlockSpec((1, n_ctx, 2 * GROUP_W), lambda bi, j: (bi, 0, 0)),
                  pl.BlockSpec((1, N_HEADS, tb, n_win), lambda bi, j: (variant(j), 0, 0, 0)),
                  pl.BlockSpec((1, GROUP_W), lambda bi, j: (0, 0))],
        out_specs=pl.BlockSpec((1, tb, GROUP_W), lambda bi, j: (bi, j, 0)),
        out_shape=jax.ShapeDtypeStruct((b, s, GROUP_W), BF16),
        scratch_shapes=[pltpu.VMEM((n_loc, GROUP_W), BF16),
                        pltpu.VMEM((N_HEADS, n_loc, GROUP_W), BF16),
                        pltpu.VMEM((n_ctx, GROUP_W), BF16),
                        pltpu.VMEM((N_HEADS, n_ctx, GROUP_W), BF16)],
        compiler_params=_params("parallel", "arbitrary"),
        name="nbr_attention",
    )(q, kv, kv, kv, kv_ctx, bias, gg.reshape(1, GROUP_W))


def _ctx_attn_body(q_ref, kx_ref, gg_ref, o_ref, vx_s):
    masks = _head_masks()
    for h in range(N_HEADS):
        vx_s[h] = kx_ref[0, :, GROUP_W:] * masks[h]
    tq = q_ref.shape[1]
    acc = _attend(q_ref[0], masks, keys=[kx_ref[0, :, 0:GROUP_W]], values=[lambda h: vx_s[h]],
                  biases=[None], acc=jnp.zeros((tq, GROUP_W), F32))
    o_ref[0] = _rms(acc, gg_ref[...]).astype(o_ref.dtype)


def _ctx_attention(q, kv_ctx, gg):
    b, n_ctx, _ = q.shape
    return pl.pallas_call(
        _ctx_attn_body,
        grid=(b,),
        in_specs=[pl.BlockSpec((1, n_ctx, GROUP_W), lambda bi: (bi, 0, 0)),
                  pl.BlockSpec((1, n_ctx, 2 * GROUP_W), lambda bi: (bi, 0, 0)),
                  pl.BlockSpec((1, GROUP_W), lambda bi: (0, 0))],
        out_specs=pl.BlockSpec((1, n_ctx, GROUP_W), lambda bi: (bi, 0, 0)),
        out_shape=jax.ShapeDtypeStruct((b, n_ctx, GROUP_W), BF16),
        scratch_shapes=[pltpu.VMEM((N_HEADS, n_ctx, GROUP_W), BF16)],
        compiler_params=_params("parallel"),
        name="ctx_attention",
    )(q, kv_ctx, gg.reshape(1, GROUP_W))


def _mixer_residual(abc_ref, dn_ref, w_ref, x_ref, m):
    y = (jnp.dot(abc_ref[0], w_ref[0:3 * GROUP_W], preferred_element_type=F32)
         + jnp.dot(dn_ref[0], w_ref[3 * GROUP_W:], preferred_element_type=F32))
    return x_ref[0] + m[M_G1:M_G1 + 1] * y


def _merge_router_body(abc_ref, dn_ref, w_ref, x_ref, mod_ref, g_ref, rw_ref, rb_ref,
                       xo_ref, h_ref, info_ref, ids_ref):
    m = mod_ref[0]
    xn = _mixer_residual(abc_ref, dn_ref, w_ref, x_ref, m)
    xo_ref[0] = xn
    h = _modulated_norm(xn, g_ref[...], m, M_SH2, M_SC2)
    n_sub = h.shape[1] // LANES
    for s in range(n_sub):
        h_ref[0, pl.ds(s, h.shape[0], stride=n_sub), :] = h[:, s * LANES:(s + 1) * LANES]
    logits = jnp.dot(h.astype(BF16), rw_ref[...], preferred_element_type=F32) + rb_ref[...]
    lane = lax.broadcasted_iota(jnp.int32, logits.shape, 1).astype(F32)
    m1 = logits.max(axis=-1, keepdims=True)
    i1 = jnp.where(logits == m1, lane, float(LANES)).min(axis=-1, keepdims=True)
    rest_logits = jnp.where(lane == i1, NEG_INF, logits)
    m2 = rest_logits.max(axis=-1, keepdims=True)
    i2 = jnp.where(rest_logits == m2, lane, float(LANES)).min(axis=-1, keepdims=True)
    e2 = jnp.exp(m2 - m1)
    den = 1.0 + e2
    info = jnp.where(lane == 0, i1, jnp.where(lane == 1, i2,
           jnp.where(lane == 2, 1.0 / den, jnp.where(lane == 3, e2 / den, 0.0))))
    info_ref[0] = info
    ids_ref[0, 0] = info.T[0:SUBLANES, :]


def _merge_router(abc, dn, w_out_bf16, x, mod, mod_row, g2, router_w, router_b, tm):
    b, s, d = x.shape
    const = lambda *shape: pl.BlockSpec(shape, lambda bi, i: (0,) * len(shape))
    return pl.pallas_call(
        _merge_router_body,
        grid=(b, s // tm),
        in_specs=[pl.BlockSpec((1, tm, 3 * GROUP_W), lambda bi, i: (bi, i, 0)),
                  pl.BlockSpec((1, tm, GROUP_W), lambda bi, i: (bi, i, 0)),
                  const(4 * GROUP_W, d),
                  pl.BlockSpec((1, tm, d), lambda bi, i: (bi, i, 0)),
                  pl.BlockSpec((1, 6, d), lambda bi, i: (mod_row(bi), 0, 0)),
                  const(1, d), const(d, LANES), const(1, LANES)],
        out_specs=[pl.BlockSpec((1, tm, d), lambda bi, i: (bi, i, 0)),
                   pl.BlockSpec((1, tm * d // LANES, LANES), lambda bi, i: (bi, i, 0)),
                   pl.BlockSpec((1, tm, LANES), lambda bi, i: (bi, i, 0)),
                   pl.BlockSpec((1, 1, SUBLANES, tm), lambda bi, i: (bi, i, 0, 0))],
        out_shape=[jax.ShapeDtypeStruct((b, s, d), F32),
                   jax.ShapeDtypeStruct((b, s * d // LANES, LANES), F32),
                   jax.ShapeDtypeStruct((b, s, LANES), F32),
                   jax.ShapeDtypeStruct((b, s // tm, SUBLANES, tm), F32)],
        compiler_params=_params("parallel", "parallel"),
        name="merge_router",
    )(abc, dn, w_out_bf16, x, mod, g2.reshape(1, d), router_w, router_b)


def _swiglu_chunk(xb, w1_ref, w3_ref, w2_ref, c0, cw):
    a = jnp.dot(xb, w1_ref[:, c0:c0 + cw], preferred_element_type=F32)
    g = jnp.dot(xb, w3_ref[:, c0:c0 + cw], preferred_element_type=F32)
    act = (a * jax.nn.sigmoid(a) * g).astype(BF16)
    return jnp.dot(act, w2_ref[c0:c0 + cw, :], preferred_element_type=F32)


def _hidden_chunks(f, tf):
    return [(c0, min(tf, f - c0)) for c0 in range(0, f, tf)]


def _transition_body(abc_ref, dn_ref, wo_ref, x_ref, mod_ref, g2_ref, w1_ref, w3_ref, w2_ref,
                     modn_ref, g1n_ref, win_ref, xo_ref, p_ref, q_ref, kv_ref, acc_ref, *, tf):
    m = mod_ref[0]
    xn = _mixer_residual(abc_ref, dn_ref, wo_ref, x_ref, m)
    h = _modulated_norm(xn, g2_ref[...], m, M_SH2, M_SC2).astype(BF16)
    for n, (c0, cw) in enumerate(_hidden_chunks(w1_ref.shape[1], tf)):
        part = _swiglu_chunk(h, w1_ref, w3_ref, w2_ref, c0, cw)
        if n == 0:
            acc_ref[...] = part
        else:
            acc_ref[...] += part
    x_next = xn + m[M_G2:M_G2 + 1] * acc_ref[...]
    xo_ref[0] = x_next
    _input_projection(x_next, g1n_ref, modn_ref[0], win_ref, p_ref, q_ref, kv_ref)


def _layer_transition(abc, dn, w_out_bf16, x, mod, mod_next, mod_row, g2, ffn_w, g1_next, w_in_next, tm, tf):
    b, s, d = x.shape
    w1, w3, w2 = ffn_w
    f = w1.shape[1]
    resident = lambda *shape: pl.BlockSpec(shape, lambda bi, i: (0,) * len(shape), pipeline_mode=pl.Buffered(1))
    rows = lambda width: pl.BlockSpec((1, tm, width), lambda bi, i: (bi, i, 0))
    mod_spec = pl.BlockSpec((1, 6, d), lambda bi, i: (mod_row(bi), 0, 0))
    return pl.pallas_call(
        functools.partial(_transition_body, tf=tf),
        grid=(b, s // tm),
        in_specs=[rows(3 * GROUP_W), rows(GROUP_W), resident(4 * GROUP_W, d), rows(d), mod_spec, resident(1, d),
                  resident(d, f), resident(d, f), resident(f, d),
                  mod_spec, resident(1, d), resident(d, IN_COLS)],
        out_specs=[rows(d), rows(OFF_D), rows(GROUP_W), rows(2 * GROUP_W)],
        out_shape=[jax.ShapeDtypeStruct((b, s, d), F32),
                   jax.ShapeDtypeStruct((b, s, OFF_D), F32),
                   jax.ShapeDtypeStruct((b, s, GROUP_W), BF16),
                   jax.ShapeDtypeStruct((b, s, 2 * GROUP_W), BF16)],
        scratch_shapes=[pltpu.VMEM((tm, d), F32)],
        compiler_params=_params("parallel", "parallel"),
        name="layer_transition",
    )(abc, dn, w_out_bf16, x, mod, g2.reshape(1, d), w1, w3, w2, mod_next, g1_next.reshape(1, d), w_in_next)


WINDOW_ALIGN = 16


LOC_RADIX = 128


def _route_plan(expert_idx, tm, tt):
    n_tok = expert_idx.shape[1]
    n_assign = 2 * n_tok
    experts = jnp.arange(N_EXPERTS, dtype=jnp.int32)[:, None]
    picks = [(expert_idx[k][None, :] == experts).astype(jnp.int32) for k in range(2)]
    chosen = picks[0] + picks[1]
    csum = jnp.cumsum(chosen, axis=1)
    counts = csum[:, -1]
    earlier = csum - chosen
    padded = (counts + tm - 1) // tm * tm
    pad_end = jnp.cumsum(padded)
    pad_start = pad_end - padded
    slots = [jnp.sum(pick * (earlier + pad_start[:, None]), axis=0) for pick in picks]
    total = n_assign + N_EXPERTS * tm
    n_blk = total // tm
    tok = jnp.arange(n_tok, dtype=jnp.int32)
    slot_tok = jnp.zeros((total,), jnp.int32).at[jnp.concatenate(slots)].set(jnp.concatenate([tok, tok]))
    n_active = pad_end[-1] // tm
    blk = jnp.arange(n_blk, dtype=jnp.int32)
    blk_exp = jnp.minimum(jnp.searchsorted(pad_end, blk * tm, side="right"), N_EXPERTS - 1).astype(jnp.int32)
    blk_exp = jnp.where(blk < n_active, blk_exp, blk_exp[jnp.maximum(n_active - 1, 0)])

    wa = tt + WINDOW_ALIGN
    assert N_EXPERTS * wa <= LOC_RADIX * LOC_RADIX
    n_tiles = n_tok // tt
    before = jnp.concatenate([jnp.zeros((N_EXPERTS, 1), jnp.int32), csum[:, tt - 1::tt][:, :-1]], axis=1)
    first = pad_start[:, None] + before
    win_base = jnp.minimum(first // WINDOW_ALIGN * WINDOW_ALIGN, total - wa)
    win_row0 = experts * wa - win_base
    digits = []
    for pick, slot in zip(picks, slots):
        loc = slot.reshape(n_tiles, tt) + jnp.sum(pick.reshape(N_EXPERTS, n_tiles, tt) * win_row0[:, :, None], axis=0)
        digits += [loc // LOC_RADIX, loc % LOC_RADIX]
    loc_digits = jnp.stack(digits + [jnp.zeros_like(digits[0])] * 4, axis=1).astype(F32)
    return (slot_tok.reshape(n_blk, tm), blk_exp, n_active.astype(jnp.int32).reshape(1),
            win_base.T.reshape(n_tiles * N_EXPERTS).astype(jnp.int32), loc_digits)


ROW_ISSUE_UNROLL = 8


def _moe_body(bexp_ref, nact_ref, tok_hbm, h_hbm, w1_hbm, w3_hbm, w2_hbm, o_ref,
              tok_s, x_s, xb_s, acc_s, w1_s, w3_s, w2_s, st1_s, st3_s, st2_s,
              sem_idx, sem_row, sem_w, *, tm, tf):
    i = pl.program_id(0)
    n_active = nact_ref[0]
    chunks = _hidden_chunks(w1_s.shape[1], tf)
    rows_per_chunk = -(-tm // len(chunks))

    def idx_copy(step):
        return pltpu.make_async_copy(tok_hbm.at[step], tok_s.at[step % 2], sem_idx.at[step % 2])

    n_sub = h_hbm.shape[1]

    def row_copy(slot, r, tok):
        return pltpu.make_async_copy(h_hbm.at[tok], x_s.at[slot, pl.ds(r * n_sub, n_sub)], sem_row.at[slot])

    @pl.when(i == 0)
    def _():
        idx_copy(0).start()
        idx_copy(0).wait()

        def issue(r, c):
            row_copy(0, r, tok_s[0, r]).start()
            return c

        lax.fori_loop(0, tm, issue, 0, unroll=ROW_ISSUE_UNROLL)
        idx_copy(1).start()

    def weight_copies(expert, n):
        c0, cw = chunks[n]
        slot = n % 2
        return (pltpu.make_async_copy(w1_hbm.at[expert, :, pl.ds(c0, cw)], st1_s.at[slot, :, pl.ds(0, cw)],
                                      sem_w.at[slot]),
                pltpu.make_async_copy(w3_hbm.at[expert, :, pl.ds(c0, cw)], st3_s.at[slot, :, pl.ds(0, cw)],
                                      sem_w.at[slot]),
                pltpu.make_async_copy(w2_hbm.at[expert, pl.ds(c0, cw), :], st2_s.at[slot, pl.ds(0, cw), :],
                                      sem_w.at[slot]))

    def install_chunk(expert, n):
        c0, cw = chunks[n]
        slot = n % 2
        for cp in weight_copies(expert, n):
            cp.wait()
        w1_s[:, c0:c0 + cw] = st1_s[slot, :, 0:cw].astype(BF16)
        w3_s[:, c0:c0 + cw] = st3_s[slot, :, 0:cw].astype(BF16)
        w2_s[c0:c0 + cw, :] = st2_s[slot, 0:cw, :].astype(BF16)
        if n + 2 < len(chunks):
            for cp in weight_copies(expert, n + 2):
                cp.start()

    def start_install(expert):
        for n in range(min(2, len(chunks))):
            for cp in weight_copies(expert, n):
                cp.start()

    @pl.when(i == 0)
    def _():
        start_install(bexp_ref[0])
        for n in range(len(chunks)):
            install_chunk(bexp_ref[0], n)

    @pl.when(i <= n_active)
    def _():
        slot = i % 2
        pltpu.make_async_copy(x_s.at[slot], x_s.at[slot], sem_row.at[slot]).wait()
        for s in range(n_sub):
            xb_s[:, s * LANES:(s + 1) * LANES] = x_s[slot, pl.ds(s, tm, stride=n_sub), :].astype(BF16)

    next_expert = bexp_ref[jnp.minimum(i + 1, pl.num_programs(0) - 1)]
    expert_ends = (i + 1 < n_active) & (next_expert != bexp_ref[i])

    def multiply_block(install_next):
        nxt = (i + 1) % 2
        idx_copy(i + 1).wait()
        if install_next:
            start_install(next_expert)
        xb = xb_s[...]
        for n, (c0, cw) in enumerate(chunks):
            part = _swiglu_chunk(xb, w1_s, w3_s, w2_s, c0, cw)
            if n == 0:
                acc_s[...] = part
            else:
                acc_s[...] += part
            for r in range(n * rows_per_chunk, min((n + 1) * rows_per_chunk, tm)):
                row_copy(nxt, r, tok_s[nxt, r]).start()
            if install_next:
                install_chunk(next_expert, n)

        @pl.when(i + 2 <= n_active)
        def _():
            idx_copy(i + 2).start()

    @pl.when((i < n_active) & jnp.logical_not(expert_ends))
    def _():
        multiply_block(install_next=False)

    @pl.when((i < n_active) & expert_ends)
    def _():
        multiply_block(install_next=True)

    @pl.when(i >= n_active)
    def _():
        acc_s[...] = jnp.zeros(acc_s.shape, acc_s.dtype)

    o_ref[...] = acc_s[...].astype(o_ref.dtype)


def _moe_experts(h, slot_tok, blk_exp, n_active, w1, w3, w2, tm, tf):
    t, n_sub, _ = h.shape
    d = n_sub * LANES
    n_blk = slot_tok.shape[0]
    f = w1.shape[2]
    any_space = pl.BlockSpec(memory_space=pl.ANY)
    grid_spec = pltpu.PrefetchScalarGridSpec(
        num_scalar_prefetch=2,
        grid=(n_blk,),
        in_specs=[any_space, any_space, any_space, any_space, any_space],
        out_specs=pl.BlockSpec((tm, d), lambda i, bexp, nact: (i, 0)),
        scratch_shapes=[pltpu.SMEM((2, tm), jnp.int32),
                        pltpu.VMEM((2, tm * n_sub, LANES), F32),
                        pltpu.VMEM((tm, d), BF16),
                        pltpu.VMEM((tm, d), F32),
                        pltpu.VMEM((d, f), BF16), pltpu.VMEM((d, f), BF16), pltpu.VMEM((f, d), BF16),
                        pltpu.VMEM((2, d, tf), F32), pltpu.VMEM((2, d, tf), F32), pltpu.VMEM((2, tf, d), F32),
                        pltpu.SemaphoreType.DMA((2,)),
                        pltpu.SemaphoreType.DMA((2,)),
                        pltpu.SemaphoreType.DMA((2,))])
    return pl.pallas_call(
        functools.partial(_moe_body, tm=tm, tf=tf),
        grid_spec=grid_spec,
        out_shape=jax.ShapeDtypeStruct((n_blk * tm, d), BF16),
        compiler_params=_params("arbitrary"),
        name="moe_experts",
    )(blk_exp, n_active, slot_tok, h, w1, w3, w2)


COMBINE_SLOTS = 3


def _combine_body(base_ref, y_hbm, loc_ref, info_ref, x_ref, mod_ref, g_ref, o_ref, win_s, sem, *, wa):
    i = pl.program_id(0)
    n = pl.num_programs(0)

    def window_copy(step, e):
        slot = step % COMBINE_SLOTS
        base = pl.multiple_of(base_ref[step * N_EXPERTS + e], WINDOW_ALIGN)
        return pltpu.make_async_copy(y_hbm.at[pl.ds(base, wa)], win_s.at[slot, pl.ds(e * wa, wa)], sem.at[slot])

    def start_windows(step):
        for e in range(N_EXPERTS):
            window_copy(step, e).start()

    @pl.when(i == 0)
    def _():
        for step in range(COMBINE_SLOTS - 1):
            @pl.when(step < n)
            def _():
                start_windows(step)

    @pl.when(i + COMBINE_SLOTS - 1 < n)
    def _():
        start_windows(i + COMBINE_SLOTS - 1)

    for e in range(N_EXPERTS):
        window_copy(i, e).wait()

    info = info_ref[...]
    win = win_s[i % COMBINE_SLOTS]
    tt = info.shape[0]
    eye = (lax.broadcasted_iota(jnp.int32, (tt, tt), 0) == lax.broadcasted_iota(jnp.int32, (tt, tt), 1))
    digits = lax.dot_general(jnp.where(eye, 1.0, 0.0).astype(BF16), loc_ref[0].astype(BF16), _NT,
                             preferred_element_type=F32)
    loc0 = digits[:, 0:1] * LOC_RADIX + digits[:, 1:2]
    loc1 = digits[:, 2:3] * LOC_RADIX + digits[:, 3:4]
    row = lax.broadcasted_iota(jnp.int32, (tt, win.shape[0]), 1).astype(F32)
    pick0 = jnp.where(row == loc0, 1.0, 0.0).astype(BF16)
    pick1 = jnp.where(row == loc1, 1.0, 0.0).astype(BF16)
    f = (info[:, 2:3] * jnp.dot(pick0, win, preferred_element_type=F32)
         + info[:, 3:4] * jnp.dot(pick1, win, preferred_element_type=F32))
    xn = x_ref[...] + mod_ref[0, M_G2:M_G2 + 1] * f
    o_ref[...] = _rms(xn, g_ref[...])


def _combine_final(win_base, y, loc_digits, info, x, mod, mod_row, final_g, tt):
    t, d = x.shape
    wa = tt + WINDOW_ALIGN
    grid_spec = pltpu.PrefetchScalarGridSpec(
        num_scalar_prefetch=1,
        grid=(t // tt,),
        in_specs=[pl.BlockSpec(memory_space=pl.ANY),
                  pl.BlockSpec((1, SUBLANES, tt), lambda i, base: (i, 0, 0)),
                  pl.BlockSpec((tt, LANES), lambda i, base: (i, 0)),
                  pl.BlockSpec((tt, d), lambda i, base: (i, 0)),
                  pl.BlockSpec((1, 6, d), lambda i, base: (mod_row(i), 0, 0)),
                  pl.BlockSpec((1, d), lambda i, base: (0, 0))],
        out_specs=pl.BlockSpec((tt, d), lambda i, base: (i, 0)),
        scratch_shapes=[pltpu.VMEM((COMBINE_SLOTS, N_EXPERTS * wa, d), y.dtype),
                        pltpu.SemaphoreType.DMA((COMBINE_SLOTS,))])
    return pl.pallas_call(
        functools.partial(_combine_body, wa=wa),
        grid_spec=grid_spec,
        out_shape=jax.ShapeDtypeStruct((t, d), F32),
        compiler_params=_params("arbitrary"),
        name="moe_combine_final",
    )(win_base, y, loc_digits, info, x, mod, final_g.reshape(1, d))


class _Tiles(NamedTuple):
    rows: int
    hidden: int
    expert_rows: int
    combine_rows: int


def _choose_tiles(seq):
    return _Tiles(rows=min(512, seq), hidden=512, expert_rows=512, combine_rows=min(128, seq))


def kernel(x, c, ctx, c_ctx, w_mod, b_mod, norm1_g, norm2_g, w_in, conv_a_w, conv_b_w, conv_b_b, conv_ln_g, conv_ln_b, sgu_ln_g, sgu_ln_b, sgu_w, sgu_b, rpb, group_g, w_out, ffn_w1, ffn_w3, ffn_w2, router_w, router_b, moe_w1, moe_w3, moe_w2, final_g):
    bsz, seq, d = x.shape
    n_ctx = ctx.shape[1]
    depth = w_mod.shape[0]
    rows = seq // GRID_W
    assert d == D_MODEL and seq % (Q_BLOCK_ROWS * GRID_W) == 0 and n_ctx % CHUNK == 0 and bsz + 1 <= 8
    assert depth == 2, "layer 0 dense with context, layer 1 (last) MoE without context"

    tiles = _choose_tiles(seq)
    tm = tiles.rows
    lat_row = lambda bi: bi
    ctx_row = lambda bi: bsz

    c8 = jnp.zeros((8, d), F32).at[:bsz].set(c).at[bsz].set(c_ctx)
    mod_all = _modulation(c8, w_mod, b_mod).reshape(depth, 8, 6, d)

    xl, xc = x, ctx
    proj = _norm_proj(xl, norm1_g[0], mod_all[0], lat_row, w_in[0].astype(BF16), tm)
    proj_c = _norm_proj(xc, norm1_g[0], mod_all[0], ctx_row, w_in[0].astype(BF16), n_ctx)
    out = None
    for l in range(depth):
        last = l == depth - 1
        mod = mod_all[l]
        w_out_b = w_out[l].astype(BF16)
        sw_b = sgu_w[l].astype(BF16)
        sbias = jnp.repeat(sgu_b[l].T, HEAD_W, axis=1)
        gg = group_g[l]
        conv_args = (conv_a_w[l], conv_b_w[l], conv_b_b[l], conv_ln_g[l], conv_ln_b[l],
                     sgu_ln_g[l], sgu_ln_b[l], sw_b, sbias, gg[:3 * GROUP_W])
        gg_d = gg[3 * GROUP_W:]

        (p, q, kv), (pc, qc, kvc) = proj, proj_c
        abc = _conv_mixer(p, *conv_args, ts=tm)
        dn = _nbr_attention(q, kv, kvc, _window_bias(rpb[l], rows), gg_d)

        if not last:
            abc_c = _conv_mixer(pc, *conv_args, ts=n_ctx)
            dn_c = _ctx_attention(qc, kvc, gg_d)
            ffn_w = tuple(w[l // 2].astype(BF16) for w in (ffn_w1, ffn_w3, ffn_w2))
            nxt = (mod_all[l + 1], norm1_g[l + 1], w_in[l + 1].astype(BF16))
            xl, *proj = _layer_transition(abc, dn, w_out_b, xl, mod, nxt[0], lat_row, norm2_g[l], ffn_w,
                                          nxt[1], nxt[2], tm, tiles.hidden)
            xc, *proj_c = _layer_transition(abc_c, dn_c, w_out_b, xc, mod, nxt[0], ctx_row, norm2_g[l], ffn_w,
                                            nxt[1], nxt[2], n_ctx, tiles.hidden)
        else:
            rw = jnp.zeros((d, LANES), F32).at[:, :N_EXPERTS].set(router_w[l // 2]).astype(BF16)
            rb = jnp.full((1, LANES), NEG_INF, F32).at[0, :N_EXPERTS].set(router_b[l // 2])
            xl, h, info, info_t = _merge_router(abc, dn, w_out_b, xl, mod, lat_row, norm2_g[l], rw, rb, tm)
            t = bsz * seq
            info = info.reshape(t, LANES)
            tt = tiles.combine_rows
            expert_idx = jnp.stack([info_t[:, :, k, :].reshape(t) for k in range(2)]).astype(jnp.int32)
            slot_tok, blk_exp, n_active, win_base, loc_digits = _route_plan(expert_idx, tiles.expert_rows, tt)
            w1, w3, w2 = moe_w1[l // 2], moe_w3[l // 2], moe_w2[l // 2]
            y = _moe_experts(h.reshape(t, d // LANES, LANES), slot_tok, blk_exp, n_active, w1, w3, w2,
                             tiles.expert_rows, tiles.hidden)
            out = _combine_final(win_base, y, loc_digits, info, xl.reshape(t, d), mod, lambda i: i * tt // seq,
                                 final_g, tt).reshape(bsz, seq, d)
    return out
```

```python
import functools
from typing import NamedTuple

import numpy as np
import jax
import jax.numpy as jnp
from jax import lax
from jax.experimental import pallas as pl
from jax.experimental.pallas import tpu as pltpu

F32 = jnp.float32
BF16 = jnp.bfloat16

D_MODEL = 1024
GRID_W = 64
EPS = 1e-6
NEG_INF = -1e30
GROUP_W = 256
N_HEADS = 4
HEAD_W = GROUP_W // N_HEADS
CONV_A = 3
CONV_B = 31
CHUNK = 128
WIN_R = 8
WIN_C = 16
OFF_B = 3 * GROUP_W
OFF_C = OFF_B + 2 * GROUP_W
OFF_D = OFF_C + 2 * GROUP_W
OFF_KV = OFF_D + GROUP_W
IN_COLS = OFF_KV + 2 * GROUP_W
N_EXPERTS = 8
LANES = 128
SUBLANES = 8
HALO = 16
Q_BLOCK_ROWS = 8
KEY_ROWS_BEFORE = 4
KEY_ROWS_AFTER = 4
VMEM_LIMIT = 56 * 1024 * 1024

M_SH1, M_SC1, M_G1, M_SH2, M_SC2, M_G2 = range(6)


def _params(*dims):
    return pltpu.CompilerParams(dimension_semantics=dims, vmem_limit_bytes=VMEM_LIMIT)


def _rms(x, g):
    return x * lax.rsqrt(jnp.mean(x * x, axis=-1, keepdims=True) + EPS) * g


def _layer_norm(x, g, b):
    mu = jnp.mean(x, axis=-1, keepdims=True)
    xc = x - mu
    var = jnp.mean(xc * xc, axis=-1, keepdims=True)
    return xc * lax.rsqrt(var + EPS) * g + b


def _mod_body(c_ref, w_ref, b_ref, o_ref):
    s = c_ref[...]
    s = s * jax.nn.sigmoid(s)
    o_ref[0] = jnp.dot(s, w_ref[0], preferred_element_type=F32,
                       precision=lax.Precision.HIGHEST) + b_ref[0]


def _modulation(c8, w_mod, b_mod):
    n_layers, d, six_d = w_mod.shape
    return pl.pallas_call(
        _mod_body,
        grid=(n_layers, six_d // d),
        in_specs=[pl.BlockSpec((8, d), lambda l, j: (0, 0)),
                  pl.BlockSpec((1, d, d), lambda l, j: (l, 0, j)),
                  pl.BlockSpec((1, 1, d), lambda l, j: (l, 0, j))],
        out_specs=pl.BlockSpec((1, 8, d), lambda l, j: (l, 0, j)),
        out_shape=jax.ShapeDtypeStruct((n_layers, 8, six_d), F32),
        compiler_params=_params("arbitrary", "arbitrary"),
        name="modulation",
    )(c8, w_mod, b_mod.reshape(n_layers, 1, six_d))


def _modulated_norm(x, g, m, shift, scale):
    return _rms(x, g) * (1.0 + m[scale:scale + 1]) + m[shift:shift + 1]


def _input_projection(x, g_ref, m, w_ref, p_ref, q_ref, kv_ref):
    hb = _modulated_norm(x, g_ref[...], m, M_SH1, M_SC1).astype(BF16)
    p_ref[0] = jnp.dot(hb, w_ref[:, :OFF_D], preferred_element_type=F32)
    q = jnp.dot(hb, w_ref[:, OFF_D:OFF_KV], preferred_element_type=F32)
    q_ref[0] = (q * (HEAD_W ** -0.5)).astype(BF16)
    kv_ref[0] = jnp.dot(hb, w_ref[:, OFF_KV:], preferred_element_type=F32).astype(BF16)


def _norm_proj_body(x_ref, g_ref, mod_ref, w_ref, p_ref, q_ref, kv_ref):
    _input_projection(x_ref[0], g_ref, mod_ref[0], w_ref, p_ref, q_ref, kv_ref)


def _norm_proj(x, g, mod, mod_row, w_in_bf16, tm):
    b, s, d = x.shape
    return pl.pallas_call(
        _norm_proj_body,
        grid=(b, s // tm),
        in_specs=[pl.BlockSpec((1, tm, d), lambda bi, i: (bi, i, 0)),
                  pl.BlockSpec((1, d), lambda bi, i: (0, 0)),
                  pl.BlockSpec((1, 6, d), lambda bi, i: (mod_row(bi), 0, 0)),
                  pl.BlockSpec((d, IN_COLS), lambda bi, i: (0, 0))],
        out_specs=[pl.BlockSpec((1, tm, OFF_D), lambda bi, i: (bi, i, 0)),
                   pl.BlockSpec((1, tm, GROUP_W), lambda bi, i: (bi, i, 0)),
                   pl.BlockSpec((1, tm, 2 * GROUP_W), lambda bi, i: (bi, i, 0))],
        out_shape=[jax.ShapeDtypeStruct((b, s, OFF_D), F32),
                   jax.ShapeDtypeStruct((b, s, GROUP_W), BF16),
                   jax.ShapeDtypeStruct((b, s, 2 * GROUP_W), BF16)],
        compiler_params=_params("parallel", "parallel"),
        name="norm_proj",
    )(x, g.reshape(1, d), mod, w_in_bf16)


def _conv_inputs(blk):
    za = blk[:, GROUP_W:2 * GROUP_W] * blk[:, 2 * GROUP_W:3 * GROUP_W]
    zb = blk[:, OFF_B:OFF_B + GROUP_W] * jax.nn.sigmoid(blk[:, OFF_B + GROUP_W:OFF_C])
    return za, zb


CP_WA = 0
CP_WB = CP_WA + CONV_A
CP_BB = CP_WB + CONV_B
CP_BLG, CP_BLB, CP_SLG, CP_SLB = (CP_BB + k for k in range(1, 5))
CP_GG = CP_SLB + 1
CP_ROWS = -(-(CP_GG + 3) // SUBLANES) * SUBLANES


def _mixer_body(prev_ref, cur_ref, next_ref, cp_ref, sw_ref, sbias_ref, o_ref, za_s, zb_s, *, ts, rc):
    i = pl.program_id(1)
    n = pl.num_programs(1)
    has_prev = (i > 0).astype(F32)
    has_next = (i < n - 1).astype(F32)

    pa, pb = _conv_inputs(prev_ref[0])
    za_s[0:HALO] = pa * has_prev
    zb_s[0, 0:HALO] = pb * has_prev
    ca, cb = _conv_inputs(cur_ref[0])
    za_s[HALO:HALO + ts] = ca
    zb_s[0, HALO:HALO + ts] = cb
    na, nb = _conv_inputs(next_ref[0])
    za_s[HALO + ts:] = na * has_next
    zb_s[0, HALO + ts:] = nb * has_next
    n_shifted = ts + 2 * HALO - SUBLANES
    for b in range(1, SUBLANES):
        zb_s[b, 0:n_shifted] = zb_s[0, b:b + n_shifted]

    cp = cp_ref[0]
    row = lambda r: cp[r:r + 1]
    wa = cp[CP_WA:CP_WA + CONV_A]
    wb = cp[CP_WB:CP_WB + CONV_B]
    lane_head = lax.broadcasted_iota(jnp.int32, (1, GROUP_W), 1) // HEAD_W
    for r0 in range(0, ts, rc):
        acc = wa[0:1] * za_s[HALO + r0 - 1:HALO + r0 - 1 + rc]
        for j in range(1, CONV_A):
            acc = acc + wa[j:j + 1] * za_s[HALO + r0 - 1 + j:HALO + r0 - 1 + j + rc]
        ya = cur_ref[0, r0:r0 + rc, 0:GROUP_W] * acc
        o_ref[0, r0:r0 + rc, 0:GROUP_W] = _rms(ya, row(CP_GG)).astype(o_ref.dtype)

        base = HALO + r0 - CONV_B // 2
        acc = None
        for j in range(CONV_B):
            b, a = (base + j) % SUBLANES, (base + j) // SUBLANES * SUBLANES
            term = wb[j:j + 1] * zb_s[b, a:a + rc]
            acc = term if acc is None else acc + term
        yb = _layer_norm(acc + row(CP_BB), row(CP_BLG), row(CP_BLB))
        yb = yb * jax.nn.sigmoid(yb)
        o_ref[0, r0:r0 + rc, GROUP_W:2 * GROUP_W] = _rms(yb, row(CP_GG + 1)).astype(o_ref.dtype)

    for r0 in range(0, ts, CHUNK):
        z = jax.nn.gelu(cur_ref[0, r0:r0 + CHUNK, OFF_C:OFF_D])
        u = z[:, :GROUP_W]
        v = _layer_norm(z[:, GROUP_W:], row(CP_SLG), row(CP_SLB)).astype(BF16)
        mixed = sbias_ref[0]
        for h in range(N_HEADS):
            vh = v * (lane_head == h).astype(BF16)
            mixed = mixed + jnp.dot(sw_ref[0, h], vh, preferred_element_type=F32)
        yc = u * mixed
        o_ref[0, r0:r0 + CHUNK, 2 * GROUP_W:3 * GROUP_W] = _rms(yc, row(CP_GG + 2)).astype(o_ref.dtype)


def _conv_params(conv_a_w, conv_b_w, conv_b_b, conv_ln_g, conv_ln_b, sgu_ln_g, sgu_ln_b, group_g):
    n_layers = conv_a_w.shape[0]
    vec = lambda a: a[:, None, :]
    pad = jnp.zeros((n_layers, CP_ROWS - CP_GG - 3, GROUP_W), F32)
    return jnp.concatenate([conv_a_w, conv_b_w, vec(conv_b_b), vec(conv_ln_g), vec(conv_ln_b), vec(sgu_ln_g),
                            vec(sgu_ln_b), group_g[:, :3 * GROUP_W].reshape(n_layers, 3, GROUP_W), pad], axis=1)


def _conv_mixer(p, conv_params, sw_bf16, sbias, layer, ts):
    b, s, _ = p.shape
    rc = min(64, ts)
    hb = ts // HALO
    n_halo = s // HALO
    of_layer = lambda *shape: pl.BlockSpec((1,) + shape, lambda bi, i: (layer,) + (0,) * len(shape))
    return pl.pallas_call(
        functools.partial(_mixer_body, ts=ts, rc=rc),
        grid=(b, s // ts),
        in_specs=[pl.BlockSpec((1, HALO, OFF_D), lambda bi, i: (bi, jnp.maximum(i * hb - 1, 0), 0)),
                  pl.BlockSpec((1, ts, OFF_D), lambda bi, i: (bi, i, 0)),
                  pl.BlockSpec((1, HALO, OFF_D), lambda bi, i: (bi, jnp.minimum((i + 1) * hb, n_halo - 1), 0)),
                  of_layer(CP_ROWS, GROUP_W), of_layer(N_HEADS, CHUNK, CHUNK), of_layer(CHUNK, GROUP_W)],
        out_specs=pl.BlockSpec((1, ts, 3 * GROUP_W), lambda bi, i: (bi, i, 0)),
        out_shape=jax.ShapeDtypeStruct((b, s, 3 * GROUP_W), BF16),
        scratch_shapes=[pltpu.VMEM((ts + 2 * HALO, GROUP_W), F32),
                        pltpu.VMEM((SUBLANES, ts + 2 * HALO, GROUP_W), F32)],
        compiler_params=_params("parallel", "parallel"),
        name="conv_mixer",
    )(p, p, p, conv_params, sw_bf16, sbias)


def _head_masks():
    lane_head = lax.broadcasted_iota(jnp.int32, (1, GROUP_W), 1) // HEAD_W
    return [(lane_head == h).astype(BF16) for h in range(N_HEADS)]


_NT = (((1,), (1,)), ((), ()))


def _attend(q, masks, keys, values, biases, acc):
    for h in range(N_HEADS):
        qh = q * masks[h]
        scores = []
        for k, bias in zip(keys, biases):
            s = lax.dot_general(qh, k, _NT, preferred_element_type=F32)
            scores.append(s if bias is None else s + bias(h))
        m = scores[0].max(axis=-1, keepdims=True)
        for s in scores[1:]:
            m = jnp.maximum(m, s.max(axis=-1, keepdims=True))
        probs = [jnp.exp(s - m) for s in scores]
        denom = probs[0].sum(axis=-1, keepdims=True)
        for p in probs[1:]:
            denom = denom + p.sum(axis=-1, keepdims=True)
        o = jnp.dot(probs[0].astype(BF16), values[0](h), preferred_element_type=F32)
        for p, v in zip(probs[1:], values[1:]):
            o = o + jnp.dot(p.astype(BF16), v(h), preferred_element_type=F32)
        acc = acc + o / denom
    return acc


def _nbr_attn_body(q_ref, kp_ref, kc_ref, kn_ref, kx_ref, bias_ref, gg_ref, o_ref,
                   k_s, v_s, kx_s, vx_s, *, starts):
    masks = _head_masks()
    n_half = kp_ref.shape[1]
    n_cur = kc_ref.shape[1]
    pieces = ((kp_ref, 0, n_half), (kc_ref, n_half, n_cur), (kn_ref, n_half + n_cur, n_half))
    for ref, off, n in pieces:
        k_s[off:off + n] = ref[0, :, 0:GROUP_W]
        for h in range(N_HEADS):
            v_s[h, off:off + n] = ref[0, :, GROUP_W:] * masks[h]
    kx_s[...] = kx_ref[0, :, 0:GROUP_W]
    for h in range(N_HEADS):
        vx_s[h] = kx_ref[0, :, GROUP_W:] * masks[h]

    j = pl.program_id(1)
    nb = pl.num_programs(1)
    n_win = bias_ref.shape[4]
    tq = q_ref.shape[1] // len(starts[0])
    for sb, (first, mid, last) in enumerate(zip(*starts)):
        st = jnp.where(j == 0, first, jnp.where(j == nb - 1, last, mid)) * GRID_W
        st = pl.multiple_of(st, GRID_W)
        r0 = sb * tq
        acc = _attend(
            q_ref[0, r0:r0 + tq, :], masks,
            keys=[k_s[pl.ds(st, n_win)], kx_s[...]],
            values=[lambda h: v_s[h, pl.ds(st, n_win)], lambda h: vx_s[h]],
            biases=[lambda h: bias_ref[0, 0, h, r0:r0 + tq, :], None],
            acc=jnp.zeros((tq, GROUP_W), F32))
        o_ref[0, r0:r0 + tq, :] = _rms(acc, gg_ref[...]).astype(o_ref.dtype)


SUB_Q_ROWS = 4
SUB_KEY_ROWS = 12


def _block_kinds(rows):
    nb = rows // Q_BLOCK_ROWS
    return (0, min(1, nb - 1), nb - 1)


def _sub_window_starts(rows):
    n_key_rows = KEY_ROWS_BEFORE + Q_BLOCK_ROWS + KEY_ROWS_AFTER
    starts = []
    for jv in _block_kinds(rows):
        per_sub = []
        for sb in range(Q_BLOCK_ROWS // SUB_Q_ROWS):
            r = Q_BLOCK_ROWS * jv + SUB_Q_ROWS * sb + np.arange(SUB_Q_ROWS)
            ks = np.clip(r - WIN_R // 2, 0, rows - WIN_R) - Q_BLOCK_ROWS * jv + KEY_ROWS_BEFORE
            start = int(min(ks.min(), n_key_rows - SUB_KEY_ROWS))
            assert start <= ks.min() and ks.max() + WIN_R <= start + SUB_KEY_ROWS
            per_sub.append(start)
        starts.append(tuple(per_sub))
    return tuple(starts)


def _window_bias(rpb_l, rows):
    n_key_rows = KEY_ROWS_BEFORE + Q_BLOCK_ROWS + KEY_ROWS_AFTER
    sub_starts = _sub_window_starts(rows)
    i = np.arange(Q_BLOCK_ROWS)[:, None, None, None]
    c = np.arange(GRID_W)[None, :, None, None]
    kr = np.arange(-KEY_ROWS_BEFORE, Q_BLOCK_ROWS + KEY_ROWS_AFTER)[None, None, :, None]
    kc = np.arange(GRID_W)[None, None, None, :]
    shape = (Q_BLOCK_ROWS, GRID_W, n_key_rows, GRID_W)
    edge = GRID_W - WIN_C
    cols = jnp.pad(rpb_l, ((0, 0), (0, 0), (edge, edge)), mode="edge")
    cols = jnp.pad(cols, ((0, 0), (0, 0), (0, 1)))
    skew = jnp.broadcast_to(cols[:, :, None, :], cols.shape[:2] + (GRID_W, 2 * GRID_W))
    skew = skew.reshape(cols.shape[:2] + (2 * GRID_W * GRID_W,))[:, :, :GRID_W * (2 * GRID_W - 1)]
    by_col = skew.reshape(cols.shape[:2] + (GRID_W, 2 * GRID_W - 1))[..., GRID_W - 1:]
    lo = KEY_ROWS_BEFORE
    hi = Q_BLOCK_ROWS + KEY_ROWS_AFTER - WIN_R
    by_col = jnp.pad(by_col, ((0, 0), (lo, hi), (0, 0), (0, 0)))
    tables = []
    for jv, starts in zip(_block_kinds(rows), sub_starts):
        r = Q_BLOCK_ROWS * jv + i
        ks = np.clip(r - WIN_R // 2, 0, rows - WIN_R)
        krow = Q_BLOCK_ROWS * jv + kr
        c_start = np.clip(c - WIN_C // 2, 0, GRID_W - WIN_C)
        valid = np.broadcast_to((krow >= ks) & (krow < ks + WIN_R) & (krow >= 0) & (krow < rows)
                                & (kc >= c_start) & (kc < c_start + WIN_C), shape)
        n_win = SUB_KEY_ROWS * GRID_W
        per_row, per_row_valid = [], []
        for ii in range(Q_BLOCK_ROWS):
            st = starts[ii // SUB_Q_ROWS]
            first = WIN_R - 1 - ii + st
            piece = by_col[:, first:first + SUB_KEY_ROWS].transpose(0, 2, 1, 3)
            per_row.append(piece.reshape(N_HEADS, GRID_W, n_win))
            per_row_valid.append(valid[ii, :, st:st + SUB_KEY_ROWS, :].reshape(GRID_W, n_win))
        values = jnp.concatenate(per_row, axis=1)
        tables.append(jnp.where(np.concatenate(per_row_valid)[None], values, NEG_INF))
    return jnp.stack(tables)


def _nbr_attention(q, kv, kv_ctx, bias, layer, gg):
    b, s, _ = q.shape
    n_ctx = kv_ctx.shape[1]
    tb = Q_BLOCK_ROWS * GRID_W
    half = KEY_ROWS_BEFORE * GRID_W
    nb = s // tb
    assert nb >= 2 and KEY_ROWS_BEFORE == KEY_ROWS_AFTER and tb == 2 * half
    n_loc = tb + 2 * half
    n_win = SUB_KEY_ROWS * GRID_W
    variant = lambda j: jnp.where(j == 0, 0, jnp.where(j == nb - 1, 2, 1))
    return pl.pallas_call(
        functools.partial(_nbr_attn_body, starts=_sub_window_starts(s // GRID_W)),
        grid=(b, nb),
        in_specs=[pl.BlockSpec((1, tb, GROUP_W), lambda bi, j: (bi, j, 0)),
                  pl.BlockSpec((1, half, 2 * GROUP_W), lambda bi, j: (bi, jnp.maximum(2 * j - 1, 0), 0)),
                  pl.BlockSpec((1, tb, 2 * GROUP_W), lambda bi, j: (bi, j, 0)),
                  pl.BlockSpec((1, half, 2 * GROUP_W), lambda bi, j: (bi, jnp.minimum(2 * j + 2, 2 * nb - 1), 0)),
                  pl.BlockSpec((1, n_ctx, 2 * GROUP_W), lambda bi, j: (bi, 0, 0)),
                  pl.BlockSpec((1, 1, N_HEADS, tb, n_win), lambda bi, j: (layer, variant(j), 0, 0, 0)),
                  pl.BlockSpec((1, GROUP_W), lambda bi, j: (0, 0))],
        out_specs=pl.BlockSpec((1, tb, GROUP_W), lambda bi, j: (bi, j, 0)),
        out_shape=jax.ShapeDtypeStruct((b, s, GROUP_W), BF16),
        scratch_shapes=[pltpu.VMEM((n_loc, GROUP_W), BF16),
                        pltpu.VMEM((N_HEADS, n_loc, GROUP_W), BF16),
                        pltpu.VMEM((n_ctx, GROUP_W), BF16),
                        pltpu.VMEM((N_HEADS, n_ctx, GROUP_W), BF16)],
        compiler_params=_params("parallel", "arbitrary"),
        name="nbr_attention",
    )(q, kv, kv, kv, kv_ctx, bias, gg.reshape(1, GROUP_W))


def _ctx_attn_body(q_ref, kx_ref, gg_ref, o_ref, vx_s):
    masks = _head_masks()
    for h in range(N_HEADS):
        vx_s[h] = kx_ref[0, :, GROUP_W:] * masks[h]
    tq = q_ref.shape[1]
    acc = _attend(q_ref[0], masks, keys=[kx_ref[0, :, 0:GROUP_W]], values=[lambda h: vx_s[h]],
                  biases=[None], acc=jnp.zeros((tq, GROUP_W), F32))
    o_ref[0] = _rms(acc, gg_ref[...]).astype(o_ref.dtype)


def _ctx_attention(q, kv_ctx, gg):
    b, n_ctx, _ = q.shape
    return pl.pallas_call(
        _ctx_attn_body,
        grid=(b,),
        in_specs=[pl.BlockSpec((1, n_ctx, GROUP_W), lambda bi: (bi, 0, 0)),
                  pl.BlockSpec((1, n_ctx, 2 * GROUP_W), lambda bi: (bi, 0, 0)),
                  pl.BlockSpec((1, GROUP_W), lambda bi: (0, 0))],
        out_specs=pl.BlockSpec((1, n_ctx, GROUP_W), lambda bi: (bi, 0, 0)),
        out_shape=jax.ShapeDtypeStruct((b, n_ctx, GROUP_W), BF16),
        scratch_shapes=[pltpu.VMEM((N_HEADS, n_ctx, GROUP_W), BF16)],
        compiler_params=_params("parallel"),
        name="ctx_attention",
    )(q, kv_ctx, gg.reshape(1, GROUP_W))


def _mixer_residual(abc_ref, dn_ref, w_ref, x_ref, m):
    y = (jnp.dot(abc_ref[0], w_ref[0:3 * GROUP_W], preferred_element_type=F32)
         + jnp.dot(dn_ref[0], w_ref[3 * GROUP_W:], preferred_element_type=F32))
    return x_ref[0] + m[M_G1:M_G1 + 1] * y


def _merge_router_body(abc_ref, dn_ref, w_ref, x_ref, mod_ref, g_ref, rw_ref, rb_ref,
                       xo_ref, h_ref, info_ref, ids_ref):
    m = mod_ref[0]
    xn = _mixer_residual(abc_ref, dn_ref, w_ref, x_ref, m)
    xo_ref[0] = xn
    h = _modulated_norm(xn, g_ref[...], m, M_SH2, M_SC2)
    n_sub = h.shape[1] // LANES
    for s in range(n_sub):
        h_ref[0, pl.ds(s, h.shape[0], stride=n_sub), :] = h[:, s * LANES:(s + 1) * LANES]
    logits = jnp.dot(h.astype(BF16), rw_ref[...], preferred_element_type=F32) + rb_ref[...]
    lane = lax.broadcasted_iota(jnp.int32, logits.shape, 1).astype(F32)
    m1 = logits.max(axis=-1, keepdims=True)
    i1 = jnp.where(logits == m1, lane, float(LANES)).min(axis=-1, keepdims=True)
    rest_logits = jnp.where(lane == i1, NEG_INF, logits)
    m2 = rest_logits.max(axis=-1, keepdims=True)
    i2 = jnp.where(rest_logits == m2, lane, float(LANES)).min(axis=-1, keepdims=True)
    e2 = jnp.exp(m2 - m1)
    den = 1.0 + e2
    info = jnp.where(lane == 0, i1, jnp.where(lane == 1, i2,
           jnp.where(lane == 2, 1.0 / den, jnp.where(lane == 3, e2 / den, 0.0))))
    info_ref[0] = info
    ids_ref[0, 0] = info.T[0:SUBLANES, :]


def _merge_router(abc, dn, w_out_bf16, x, mod, mod_row, g2, router_w, router_b, tm):
    b, s, d = x.shape
    const = lambda *shape: pl.BlockSpec(shape, lambda bi, i: (0,) * len(shape))
    return pl.pallas_call(
        _merge_router_body,
        grid=(b, s // tm),
        in_specs=[pl.BlockSpec((1, tm, 3 * GROUP_W), lambda bi, i: (bi, i, 0)),
                  pl.BlockSpec((1, tm, GROUP_W), lambda bi, i: (bi, i, 0)),
                  const(4 * GROUP_W, d),
                  pl.BlockSpec((1, tm, d), lambda bi, i: (bi, i, 0)),
                  pl.BlockSpec((1, 6, d), lambda bi, i: (mod_row(bi), 0, 0)),
                  const(1, d), const(d, LANES), const(1, LANES)],
        out_specs=[pl.BlockSpec((1, tm, d), lambda bi, i: (bi, i, 0)),
                   pl.BlockSpec((1, tm * d // LANES, LANES), lambda bi, i: (bi, i, 0)),
                   pl.BlockSpec((1, tm, LANES), lambda bi, i: (bi, i, 0)),
                   pl.BlockSpec((1, 1, SUBLANES, tm), lambda bi, i: (bi, i, 0, 0))],
        out_shape=[jax.ShapeDtypeStruct((b, s, d), F32),
                   jax.ShapeDtypeStruct((b, s * d // LANES, LANES), F32),
                   jax.ShapeDtypeStruct((b, s, LANES), F32),
                   jax.ShapeDtypeStruct((b, s // tm, SUBLANES, tm), F32)],
        compiler_params=_params("parallel", "parallel"),
        name="merge_router",
    )(abc, dn, w_out_bf16, x, mod, g2.reshape(1, d), router_w, router_b)


def _swiglu_chunk(xb, w1_ref, w3_ref, w2_ref, c0, cw):
    a = jnp.dot(xb, w1_ref[:, c0:c0 + cw], preferred_element_type=F32)
    g = jnp.dot(xb, w3_ref[:, c0:c0 + cw], preferred_element_type=F32)
    act = (a * jax.nn.sigmoid(a) * g).astype(BF16)
    return jnp.dot(act, w2_ref[c0:c0 + cw, :], preferred_element_type=F32)


def _hidden_chunks(f, tf):
    return [(c0, min(tf, f - c0)) for c0 in range(0, f, tf)]


def _transition_body(abc_ref, dn_ref, wo_ref, x_ref, mod_ref, g2_ref, w1_ref, w3_ref, w2_ref,
                     modn_ref, g1n_ref, win_ref, xo_ref, p_ref, q_ref, kv_ref, acc_ref, *, tf):
    m = mod_ref[0]
    xn = _mixer_residual(abc_ref, dn_ref, wo_ref, x_ref, m)
    h = _modulated_norm(xn, g2_ref[...], m, M_SH2, M_SC2).astype(BF16)
    for n, (c0, cw) in enumerate(_hidden_chunks(w1_ref.shape[1], tf)):
        part = _swiglu_chunk(h, w1_ref, w3_ref, w2_ref, c0, cw)
        if n == 0:
            acc_ref[...] = part
        else:
            acc_ref[...] += part
    x_next = xn + m[M_G2:M_G2 + 1] * acc_ref[...]
    xo_ref[0] = x_next
    _input_projection(x_next, g1n_ref, modn_ref[0], win_ref, p_ref, q_ref, kv_ref)


def _layer_transition(abc, dn, w_out_bf16, x, mod, mod_next, mod_row, g2, ffn_w, g1_next, w_in_next, tm, tf):
    b, s, d = x.shape
    w1, w3, w2 = ffn_w
    f = w1.shape[1]
    resident = lambda *shape: pl.BlockSpec(shape, lambda bi, i: (0,) * len(shape), pipeline_mode=pl.Buffered(1))
    rows = lambda width: pl.BlockSpec((1, tm, width), lambda bi, i: (bi, i, 0))
    mod_spec = pl.BlockSpec((1, 6, d), lambda bi, i: (mod_row(bi), 0, 0))
    return pl.pallas_call(
        functools.partial(_transition_body, tf=tf),
        grid=(b, s // tm),
        in_specs=[rows(3 * GROUP_W), rows(GROUP_W), resident(4 * GROUP_W, d), rows(d), mod_spec, resident(1, d),
                  resident(d, f), resident(d, f), resident(f, d),
                  mod_spec, resident(1, d), resident(d, IN_COLS)],
        out_specs=[rows(d), rows(OFF_D), rows(GROUP_W), rows(2 * GROUP_W)],
        out_shape=[jax.ShapeDtypeStruct((b, s, d), F32),
                   jax.ShapeDtypeStruct((b, s, OFF_D), F32),
                   jax.ShapeDtypeStruct((b, s, GROUP_W), BF16),
                   jax.ShapeDtypeStruct((b, s, 2 * GROUP_W), BF16)],
        scratch_shapes=[pltpu.VMEM((tm, d), F32)],
        compiler_params=_params("parallel", "parallel"),
        name="layer_transition",
    )(abc, dn, w_out_bf16, x, mod, g2.reshape(1, d), w1, w3, w2, mod_next, g1_next.reshape(1, d), w_in_next)


WINDOW_ALIGN = 16


LOC_RADIX = 128


def _route_plan(expert_idx, tm, tt):
    n_tok = expert_idx.shape[1]
    n_assign = 2 * n_tok
    experts = jnp.arange(N_EXPERTS, dtype=jnp.int32)[:, None]
    picks = [(expert_idx[k][None, :] == experts).astype(jnp.int32) for k in range(2)]
    chosen = picks[0] + picks[1]
    csum = jnp.cumsum(chosen, axis=1)
    counts = csum[:, -1]
    earlier = csum - chosen
    padded = (counts + tm - 1) // tm * tm
    pad_end = jnp.cumsum(padded)
    pad_start = pad_end - padded
    slots = [jnp.sum(pick * (earlier + pad_start[:, None]), axis=0) for pick in picks]
    total = n_assign + N_EXPERTS * tm
    n_blk = total // tm
    tok = jnp.arange(n_tok, dtype=jnp.int32)
    slot_tok = jnp.zeros((total,), jnp.int32).at[jnp.concatenate(slots)].set(jnp.concatenate([tok, tok]))
    n_active = pad_end[-1] // tm
    blk = jnp.arange(n_blk, dtype=jnp.int32)
    blk_exp = jnp.minimum(jnp.searchsorted(pad_end, blk * tm, side="right"), N_EXPERTS - 1).astype(jnp.int32)
    blk_exp = jnp.where(blk < n_active, blk_exp, blk_exp[jnp.maximum(n_active - 1, 0)])

    wa = tt + WINDOW_ALIGN
    assert N_EXPERTS * wa <= LOC_RADIX * LOC_RADIX
    n_tiles = n_tok // tt
    before = jnp.concatenate([jnp.zeros((N_EXPERTS, 1), jnp.int32), csum[:, tt - 1::tt][:, :-1]], axis=1)
    first = pad_start[:, None] + before
    win_base = jnp.minimum(first // WINDOW_ALIGN * WINDOW_ALIGN, total - wa)
    win_row0 = experts * wa - win_base
    digits = []
    for pick, slot in zip(picks, slots):
        loc = slot.reshape(n_tiles, tt) + jnp.sum(pick.reshape(N_EXPERTS, n_tiles, tt) * win_row0[:, :, None], axis=0)
        digits += [loc // LOC_RADIX, loc % LOC_RADIX]
    loc_digits = jnp.stack(digits + [jnp.zeros_like(digits[0])] * 4, axis=1).astype(F32)
    return (slot_tok.reshape(n_blk, tm), blk_exp, n_active.astype(jnp.int32).reshape(1),
            win_base.T.reshape(n_tiles * N_EXPERTS).astype(jnp.int32), loc_digits)


ROW_ISSUE_UNROLL = 8


def _moe_body(bexp_ref, nact_ref, tok_hbm, h_hbm, w1_hbm, w3_hbm, w2_hbm, o_ref,
              tok_s, x_s, xb_s, acc_s, w1_s, w3_s, w2_s, st1_s, st3_s, st2_s,
              sem_idx, sem_row, sem_w, *, tm, tf):
    i = pl.program_id(0)
    n_active = nact_ref[0]
    chunks = _hidden_chunks(w1_s.shape[1], tf)
    rows_per_chunk = -(-tm // len(chunks))

    def idx_copy(step):
        return pltpu.make_async_copy(tok_hbm.at[step], tok_s.at[step % 2], sem_idx.at[step % 2])

    n_sub = h_hbm.shape[1]

    def row_copy(slot, r, tok):
        return pltpu.make_async_copy(h_hbm.at[tok], x_s.at[slot, pl.ds(r * n_sub, n_sub)], sem_row.at[slot])

    @pl.when(i == 0)
    def _():
        idx_copy(0).start()
        idx_copy(0).wait()

        def issue(r, c):
            row_copy(0, r, tok_s[0, r]).start()
            return c

        lax.fori_loop(0, tm, issue, 0, unroll=ROW_ISSUE_UNROLL)
        idx_copy(1).start()

    def weight_copies(expert, n):
        c0, cw = chunks[n]
        slot = n % 2
        return (pltpu.make_async_copy(w1_hbm.at[expert, :, pl.ds(c0, cw)], st1_s.at[slot, :, pl.ds(0, cw)],
                                      sem_w.at[slot]),
                pltpu.make_async_copy(w3_hbm.at[expert, :, pl.ds(c0, cw)], st3_s.at[slot, :, pl.ds(0, cw)],
                                      sem_w.at[slot]),
                pltpu.make_async_copy(w2_hbm.at[expert, pl.ds(c0, cw), :], st2_s.at[slot, pl.ds(0, cw), :],
                                      sem_w.at[slot]))

    def install_chunk(expert, n):
        c0, cw = chunks[n]
        slot = n % 2
        for cp in weight_copies(expert, n):
            cp.wait()
        w1_s[:, c0:c0 + cw] = st1_s[slot, :, 0:cw].astype(BF16)
        w3_s[:, c0:c0 + cw] = st3_s[slot, :, 0:cw].astype(BF16)
        w2_s[c0:c0 + cw, :] = st2_s[slot, 0:cw, :].astype(BF16)
        if n + 2 < len(chunks):
            for cp in weight_copies(expert, n + 2):
                cp.start()

    def start_install(expert):
        for n in range(min(2, len(chunks))):
            for cp in weight_copies(expert, n):
                cp.start()

    @pl.when(i == 0)
    def _():
        start_install(bexp_ref[0])
        for n in range(len(chunks)):
            install_chunk(bexp_ref[0], n)

    @pl.when(i <= n_active)
    def _():
        slot = i % 2
        pltpu.make_async_copy(x_s.at[slot], x_s.at[slot], sem_row.at[slot]).wait()
        for s in range(n_sub):
            xb_s[:, s * LANES:(s + 1) * LANES] = x_s[slot, pl.ds(s, tm, stride=n_sub), :].astype(BF16)

    next_expert = bexp_ref[jnp.minimum(i + 1, pl.num_programs(0) - 1)]
    expert_ends = (i + 1 < n_active) & (next_expert != bexp_ref[i])

    def multiply_block(install_next):
        nxt = (i + 1) % 2
        idx_copy(i + 1).wait()
        if install_next:
            start_install(next_expert)
        xb = xb_s[...]
        for n, (c0, cw) in enumerate(chunks):
            part = _swiglu_chunk(xb, w1_s, w3_s, w2_s, c0, cw)
            if n == 0:
                acc_s[...] = part
            else:
                acc_s[...] += part
            for r in range(n * rows_per_chunk, min((n + 1) * rows_per_chunk, tm)):
                row_copy(nxt, r, tok_s[nxt, r]).start()
            if install_next:
                install_chunk(next_expert, n)

        @pl.when(i + 2 <= n_active)
        def _():
            idx_copy(i + 2).start()

    @pl.when((i < n_active) & jnp.logical_not(expert_ends))
    def _():
        multiply_block(install_next=False)

    @pl.when((i < n_active) & expert_ends)
    def _():
        multiply_block(install_next=True)

    @pl.when(i >= n_active)
    def _():
        acc_s[...] = jnp.zeros(acc_s.shape, acc_s.dtype)

    o_ref[...] = acc_s[...].astype(o_ref.dtype)


def _moe_experts(h, slot_tok, blk_exp, n_active, w1, w3, w2, tm, tf):
    t, n_sub, _ = h.shape
    d = n_sub * LANES
    n_blk = slot_tok.shape[0]
    f = w1.shape[2]
    any_space = pl.BlockSpec(memory_space=pl.ANY)
    grid_spec = pltpu.PrefetchScalarGridSpec(
        num_scalar_prefetch=2,
        grid=(n_blk,),
        in_specs=[any_space, any_space, any_space, any_space, any_space],
        out_specs=pl.BlockSpec((tm, d), lambda i, bexp, nact: (i, 0)),
        scratch_shapes=[pltpu.SMEM((2, tm), jnp.int32),
                        pltpu.VMEM((2, tm * n_sub, LANES), F32),
                        pltpu.VMEM((tm, d), BF16),
                        pltpu.VMEM((tm, d), F32),
                        pltpu.VMEM((d, f), BF16), pltpu.VMEM((d, f), BF16), pltpu.VMEM((f, d), BF16),
                        pltpu.VMEM((2, d, tf), F32), pltpu.VMEM((2, d, tf), F32), pltpu.VMEM((2, tf, d), F32),
                        pltpu.SemaphoreType.DMA((2,)),
                        pltpu.SemaphoreType.DMA((2,)),
                        pltpu.SemaphoreType.DMA((2,))])
    return pl.pallas_call(
        functools.partial(_moe_body, tm=tm, tf=tf),
        grid_spec=grid_spec,
        out_shape=jax.ShapeDtypeStruct((n_blk * tm, d), BF16),
        compiler_params=_params("arbitrary"),
        name="moe_experts",
    )(blk_exp, n_active, slot_tok, h, w1, w3, w2)


COMBINE_SLOTS = 3


def _combine_body(base_ref, y_hbm, loc_ref, info_ref, x_ref, mod_ref, g_ref, o_ref, win_s, sem, *, wa):
    i = pl.program_id(0)
    n = pl.num_programs(0)

    def window_copy(step, e):
        slot = step % COMBINE_SLOTS
        base = pl.multiple_of(base_ref[step * N_EXPERTS + e], WINDOW_ALIGN)
        return pltpu.make_async_copy(y_hbm.at[pl.ds(base, wa)], win_s.at[slot, pl.ds(e * wa, wa)], sem.at[slot])

    def start_windows(step):
        for e in range(N_EXPERTS):
            window_copy(step, e).start()

    @pl.when(i == 0)
    def _():
        for step in range(COMBINE_SLOTS - 1):
            @pl.when(step < n)
            def _():
                start_windows(step)

    @pl.when(i + COMBINE_SLOTS - 1 < n)
    def _():
        start_windows(i + COMBINE_SLOTS - 1)

    for e in range(N_EXPERTS):
        window_copy(i, e).wait()

    info = info_ref[...]
    win = win_s[i % COMBINE_SLOTS]
    tt = info.shape[0]
    eye = (lax.broadcasted_iota(jnp.int32, (tt, tt), 0) == lax.broadcasted_iota(jnp.int32, (tt, tt), 1))
    digits = lax.dot_general(jnp.where(eye, 1.0, 0.0).astype(BF16), loc_ref[0].astype(BF16), _NT,
                             preferred_element_type=F32)
    loc0 = digits[:, 0:1] * LOC_RADIX + digits[:, 1:2]
    loc1 = digits[:, 2:3] * LOC_RADIX + digits[:, 3:4]
    row = lax.broadcasted_iota(jnp.int32, (tt, win.shape[0]), 1).astype(F32)
    pick0 = jnp.where(row == loc0, 1.0, 0.0).astype(BF16)
    pick1 = jnp.where(row == loc1, 1.0, 0.0).astype(BF16)
    f = (info[:, 2:3] * jnp.dot(pick0, win, preferred_element_type=F32)
         + info[:, 3:4] * jnp.dot(pick1, win, preferred_element_type=F32))
    xn = x_ref[...] + mod_ref[0, M_G2:M_G2 + 1] * f
    o_ref[...] = _rms(xn, g_ref[...])


def _combine_final(win_base, y, loc_digits, info, x, mod, mod_row, final_g, tt):
    t, d = x.shape
    wa = tt + WINDOW_ALIGN
    grid_spec = pltpu.PrefetchScalarGridSpec(
        num_scalar_prefetch=1,
        grid=(t // tt,),
        in_specs=[pl.BlockSpec(memory_space=pl.ANY),
                  pl.BlockSpec((1, SUBLANES, tt), lambda i, base: (i, 0, 0)),
                  pl.BlockSpec((tt, LANES), lambda i, base: (i, 0)),
                  pl.BlockSpec((tt, d), lambda i, base: (i, 0)),
                  pl.BlockSpec((1, 6, d), lambda i, base: (mod_row(i), 0, 0)),
                  pl.BlockSpec((1, d), lambda i, base: (0, 0))],
        out_specs=pl.BlockSpec((tt, d), lambda i, base: (i, 0)),
        scratch_shapes=[pltpu.VMEM((COMBINE_SLOTS, N_EXPERTS * wa, d), y.dtype),
                        pltpu.SemaphoreType.DMA((COMBINE_SLOTS,))])
    return pl.pallas_call(
        functools.partial(_combine_body, wa=wa),
        grid_spec=grid_spec,
        out_shape=jax.ShapeDtypeStruct((t, d), F32),
        compiler_params=_params("arbitrary"),
        name="moe_combine_final",
    )(win_base, y, loc_digits, info, x, mod, final_g.reshape(1, d))


class _Tiles(NamedTuple):
    rows: int
    hidden: int
    expert_rows: int
    combine_rows: int


def _choose_tiles(seq):
    return _Tiles(rows=min(512, seq), hidden=512, expert_rows=512, combine_rows=min(128, seq))


def kernel(x, c, ctx, c_ctx, w_mod, b_mod, norm1_g, norm2_g, w_in, conv_a_w, conv_b_w, conv_b_b, conv_ln_g, conv_ln_b, sgu_ln_g, sgu_ln_b, sgu_w, sgu_b, rpb, group_g, w_out, ffn_w1, ffn_w3, ffn_w2, router_w, router_b, moe_w1, moe_w3, moe_w2, final_g):
    bsz, seq, d = x.shape
    n_ctx = ctx.shape[1]
    depth = w_mod.shape[0]
    rows = seq // GRID_W
    assert d == D_MODEL and seq % (Q_BLOCK_ROWS * GRID_W) == 0 and n_ctx % CHUNK == 0 and bsz + 1 <= 8
    assert depth == 2, "layer 0 dense with context, layer 1 (last) MoE without context"

    tiles = _choose_tiles(seq)
    tm = tiles.rows
    lat_row = lambda bi: bi
    ctx_row = lambda bi: bsz

    c8 = jnp.concatenate([c, c_ctx[None], jnp.zeros((8 - bsz - 1, d), F32)], axis=0)
    mod_all = _modulation(c8, w_mod, b_mod).reshape(depth, 8, 6, d)

    conv_params = _conv_params(conv_a_w, conv_b_w, conv_b_b, conv_ln_g, conv_ln_b, sgu_ln_g, sgu_ln_b, group_g)
    sw_b = sgu_w.astype(BF16)
    sbias = jnp.repeat(jnp.swapaxes(sgu_b, 1, 2), HEAD_W, axis=2)
    bias_all = jax.vmap(lambda r: _window_bias(r, rows))(rpb)

    xl, xc = x, ctx
    proj = _norm_proj(xl, norm1_g[0], mod_all[0], lat_row, w_in[0].astype(BF16), tm)
    proj_c = _norm_proj(xc, norm1_g[0], mod_all[0], ctx_row, w_in[0].astype(BF16), n_ctx)
    out = None
    for l in range(depth):
        last = l == depth - 1
        mod = mod_all[l]
        w_out_b = w_out[l].astype(BF16)
        gg_d = group_g[l, 3 * GROUP_W:]

        (p, q, kv), (pc, qc, kvc) = proj, proj_c
        abc = _conv_mixer(p, conv_params, sw_b, sbias, l, ts=tm)
        dn = _nbr_attention(q, kv, kvc, bias_all, l, gg_d)

        if not last:
            abc_c = _conv_mixer(pc, conv_params, sw_b, sbias, l, ts=n_ctx)
            dn_c = _ctx_attention(qc, kvc, gg_d)
            ffn_w = tuple(w[l // 2].astype(BF16) for w in (ffn_w1, ffn_w3, ffn_w2))
            nxt = (mod_all[l + 1], norm1_g[l + 1], w_in[l + 1].astype(BF16))
            xl, *proj = _layer_transition(abc, dn, w_out_b, xl, mod, nxt[0], lat_row, norm2_g[l], ffn_w,
                                          nxt[1], nxt[2], tm, tiles.hidden)
            xc, *proj_c = _layer_transition(abc_c, dn_c, w_out_b, xc, mod, nxt[0], ctx_row, norm2_g[l], ffn_w,
                                            nxt[1], nxt[2], n_ctx, tiles.hidden)
        else:
            rw = jnp.zeros((d, LANES), F32).at[:, :N_EXPERTS].set(router_w[l // 2]).astype(BF16)
            rb = jnp.full((1, LANES), NEG_INF, F32).at[0, :N_EXPERTS].set(router_b[l // 2])
            xl, h, info, info_t = _merge_router(abc, dn, w_out_b, xl, mod, lat_row, norm2_g[l], rw, rb, tm)
            t = bsz * seq
            info = info.reshape(t, LANES)
            tt = tiles.combine_rows
            expert_idx = jnp.stack([info_t[:, :, k, :].reshape(t) for k in range(2)]).astype(jnp.int32)
            slot_tok, blk_exp, n_active, win_base, loc_digits = _route_plan(expert_idx, tiles.expert_rows, tt)
            w1, w3, w2 = moe_w1[l // 2], moe_w3[l // 2], moe_w2[l // 2]
            y = _moe_experts(h.reshape(t, d // LANES, LANES), slot_tok, blk_exp, n_active, w1, w3, w2,
                             tiles.expert_rows, tiles.hidden)
            out = _combine_final(win_base, y, loc_digits, info, xl.reshape(t, d), mod, lambda i: i * tt // seq,
                                 final_g, tt).reshape(bsz, seq, d)
    return out
```

```python
import functools
from typing import NamedTuple

import numpy as np
import jax
import jax.numpy as jnp
from jax import lax
from jax.experimental import pallas as pl
from jax.experimental.pallas import tpu as pltpu

F32 = jnp.float32
BF16 = jnp.bfloat16

D_MODEL = 1024
GRID_W = 64
EPS = 1e-6
NEG_INF = -1e30
GROUP_W = 256
N_HEADS = 4
HEAD_W = GROUP_W // N_HEADS
CONV_A = 3
CONV_B = 31
CHUNK = 128
WIN_R = 8
WIN_C = 16
OFF_B = 3 * GROUP_W
OFF_C = OFF_B + 2 * GROUP_W
OFF_D = OFF_C + 2 * GROUP_W
OFF_KV = OFF_D + GROUP_W
IN_COLS = OFF_KV + 2 * GROUP_W
N_EXPERTS = 8
LANES = 128
SUBLANES = 8
HALO = 16
Q_BLOCK_ROWS = 8
KEY_ROWS_BEFORE = 4
KEY_ROWS_AFTER = 4
VMEM_LIMIT = 56 * 1024 * 1024

M_SH1, M_SC1, M_G1, M_SH2, M_SC2, M_G2 = range(6)


def _params(*dims):
    return pltpu.CompilerParams(dimension_semantics=dims, vmem_limit_bytes=VMEM_LIMIT)


def _rms(x, g):
    return x * lax.rsqrt(jnp.mean(x * x, axis=-1, keepdims=True) + EPS) * g


def _layer_norm(x, g, b):
    mu = jnp.mean(x, axis=-1, keepdims=True)
    xc = x - mu
    var = jnp.mean(xc * xc, axis=-1, keepdims=True)
    return xc * lax.rsqrt(var + EPS) * g + b


def _mod_body(c_ref, w_ref, b_ref, o_ref):
    s = c_ref[...]
    s = s * jax.nn.sigmoid(s)
    o_ref[0] = jnp.dot(s, w_ref[0], preferred_element_type=F32,
                       precision=lax.Precision.HIGHEST) + b_ref[0]


def _modulation(c8, w_mod, b_mod):
    n_layers, d, six_d = w_mod.shape
    return pl.pallas_call(
        _mod_body,
        grid=(n_layers, six_d // d),
        in_specs=[pl.BlockSpec((8, d), lambda l, j: (0, 0)),
                  pl.BlockSpec((1, d, d), lambda l, j: (l, 0, j)),
                  pl.BlockSpec((1, 1, d), lambda l, j: (l, 0, j))],
        out_specs=pl.BlockSpec((1, 8, d), lambda l, j: (l, 0, j)),
        out_shape=jax.ShapeDtypeStruct((n_layers, 8, six_d), F32),
        compiler_params=_params("arbitrary", "arbitrary"),
        name="modulation",
    )(c8, w_mod, b_mod.reshape(n_layers, 1, six_d))


def _modulated_norm(x, g, m, shift, scale):
    return _rms(x, g) * (1.0 + m[scale:scale + 1]) + m[shift:shift + 1]


def _input_projection(x, g_ref, m, w_ref, q_ref, kv_ref):
    hb = _modulated_norm(x, g_ref[...], m, M_SH1, M_SC1).astype(BF16)
    q = jnp.dot(hb, w_ref[:, OFF_D:OFF_KV], preferred_element_type=F32)
    q_ref[0] = (q * (HEAD_W ** -0.5)).astype(BF16)
    kv_ref[0] = jnp.dot(hb, w_ref[:, OFF_KV:], preferred_element_type=F32).astype(BF16)
    return jnp.dot(hb, w_ref[:, :OFF_D], preferred_element_type=F32)


def _norm_proj_body(x_ref, g_ref, mod_ref, w_ref, p_ref, q_ref, kv_ref):
    p_ref[0] = _input_projection(x_ref[0], g_ref, mod_ref[0], w_ref, q_ref, kv_ref)


def _norm_proj(x, g, mod, mod_row, w_in_bf16, tm):
    b, s, d = x.shape
    return pl.pallas_call(
        _norm_proj_body,
        grid=(b, s // tm),
        in_specs=[pl.BlockSpec((1, tm, d), lambda bi, i: (bi, i, 0)),
                  pl.BlockSpec((1, d), lambda bi, i: (0, 0)),
                  pl.BlockSpec((1, 6, d), lambda bi, i: (mod_row(bi), 0, 0)),
                  pl.BlockSpec((d, IN_COLS), lambda bi, i: (0, 0))],
        out_specs=[pl.BlockSpec((1, tm, OFF_D), lambda bi, i: (bi, i, 0)),
                   pl.BlockSpec((1, tm, GROUP_W), lambda bi, i: (bi, i, 0)),
                   pl.BlockSpec((1, tm, 2 * GROUP_W), lambda bi, i: (bi, i, 0))],
        out_shape=[jax.ShapeDtypeStruct((b, s, OFF_D), F32),
                   jax.ShapeDtypeStruct((b, s, GROUP_W), BF16),
                   jax.ShapeDtypeStruct((b, s, 2 * GROUP_W), BF16)],
        compiler_params=_params("parallel", "parallel"),
        name="norm_proj",
    )(x, g.reshape(1, d), mod, w_in_bf16)


def _conv_inputs(blk):
    za = blk[:, GROUP_W:2 * GROUP_W] * blk[:, 2 * GROUP_W:3 * GROUP_W]
    zb = blk[:, OFF_B:OFF_B + GROUP_W] * jax.nn.sigmoid(blk[:, OFF_B + GROUP_W:OFF_C])
    return za, zb


CP_WA = 0
CP_WB = CP_WA + CONV_A
CP_BB = CP_WB + CONV_B
CP_BLG, CP_BLB, CP_SLG, CP_SLB = (CP_BB + k for k in range(1, 5))
CP_GG = CP_SLB + 1
CP_ROWS = -(-(CP_GG + 3) // SUBLANES) * SUBLANES


def _conv_groups(cur, prev_blk, next_blk, has_prev, has_next, cp, sw_ref, sbias_ref, o_ref, za_s, zb_s, *, ts, rc):
    pa, pb = _conv_inputs(prev_blk)
    za_s[0:HALO] = pa * has_prev
    zb_s[0, 0:HALO] = pb * has_prev
    ca, cb = _conv_inputs(cur[...])
    za_s[HALO:HALO + ts] = ca
    zb_s[0, HALO:HALO + ts] = cb
    na, nb = _conv_inputs(next_blk)
    za_s[HALO + ts:] = na * has_next
    zb_s[0, HALO + ts:] = nb * has_next
    n_shifted = ts + 2 * HALO - SUBLANES
    for b in range(1, SUBLANES):
        zb_s[b, 0:n_shifted] = zb_s[0, b:b + n_shifted]

    row = lambda r: cp[r:r + 1]
    wa = cp[CP_WA:CP_WA + CONV_A]
    wb = cp[CP_WB:CP_WB + CONV_B]
    lane_head = lax.broadcasted_iota(jnp.int32, (1, GROUP_W), 1) // HEAD_W
    for r0 in range(0, ts, rc):
        acc = wa[0:1] * za_s[HALO + r0 - 1:HALO + r0 - 1 + rc]
        for j in range(1, CONV_A):
            acc = acc + wa[j:j + 1] * za_s[HALO + r0 - 1 + j:HALO + r0 - 1 + j + rc]
        ya = cur[r0:r0 + rc, 0:GROUP_W] * acc
        o_ref[0, r0:r0 + rc, 0:GROUP_W] = _rms(ya, row(CP_GG)).astype(o_ref.dtype)

        base = HALO + r0 - CONV_B // 2
        acc = None
        for j in range(CONV_B):
            b, a = (base + j) % SUBLANES, (base + j) // SUBLANES * SUBLANES
            term = wb[j:j + 1] * zb_s[b, a:a + rc]
            acc = term if acc is None else acc + term
        yb = _layer_norm(acc + row(CP_BB), row(CP_BLG), row(CP_BLB))
        yb = yb * jax.nn.sigmoid(yb)
        o_ref[0, r0:r0 + rc, GROUP_W:2 * GROUP_W] = _rms(yb, row(CP_GG + 1)).astype(o_ref.dtype)

    for r0 in range(0, ts, CHUNK):
        z = jax.nn.gelu(cur[r0:r0 + CHUNK, OFF_C:OFF_D])
        u = z[:, :GROUP_W]
        v = _layer_norm(z[:, GROUP_W:], row(CP_SLG), row(CP_SLB)).astype(BF16)
        mixed = sbias_ref[0]
        for h in range(N_HEADS):
            vh = v * (lane_head == h).astype(BF16)
            mixed = mixed + jnp.dot(sw_ref[0, h], vh, preferred_element_type=F32)
        yc = u * mixed
        o_ref[0, r0:r0 + CHUNK, 2 * GROUP_W:3 * GROUP_W] = _rms(yc, row(CP_GG + 2)).astype(o_ref.dtype)


def _mixer_body(prev_ref, cur_ref, next_ref, cp_ref, sw_ref, sbias_ref, o_ref, za_s, zb_s, *, ts, rc):
    i = pl.program_id(1)
    n = pl.num_programs(1)
    _conv_groups(cur_ref.at[0], prev_ref[0], next_ref[0], (i > 0).astype(F32), (i < n - 1).astype(F32),
                 cp_ref[0], sw_ref, sbias_ref, o_ref, za_s, zb_s, ts=ts, rc=rc)


def _conv_params(conv_a_w, conv_b_w, conv_b_b, conv_ln_g, conv_ln_b, sgu_ln_g, sgu_ln_b, group_g):
    n_layers = conv_a_w.shape[0]
    vec = lambda a: a[:, None, :]
    pad = jnp.zeros((n_layers, CP_ROWS - CP_GG - 3, GROUP_W), F32)
    return jnp.concatenate([conv_a_w, conv_b_w, vec(conv_b_b), vec(conv_ln_g), vec(conv_ln_b), vec(sgu_ln_g),
                            vec(sgu_ln_b), group_g[:, :3 * GROUP_W].reshape(n_layers, 3, GROUP_W), pad], axis=1)


def _conv_mixer(p, conv_params, sw_bf16, sbias, layer, ts):
    b, s, _ = p.shape
    rc = min(64, ts)
    hb = ts // HALO
    n_halo = s // HALO
    of_layer = lambda *shape: pl.BlockSpec((1,) + shape, lambda bi, i: (layer,) + (0,) * len(shape))
    return pl.pallas_call(
        functools.partial(_mixer_body, ts=ts, rc=rc),
        grid=(b, s // ts),
        in_specs=[pl.BlockSpec((1, HALO, OFF_D), lambda bi, i: (bi, jnp.maximum(i * hb - 1, 0), 0)),
                  pl.BlockSpec((1, ts, OFF_D), lambda bi, i: (bi, i, 0)),
                  pl.BlockSpec((1, HALO, OFF_D), lambda bi, i: (bi, jnp.minimum((i + 1) * hb, n_halo - 1), 0)),
                  of_layer(CP_ROWS, GROUP_W), of_layer(N_HEADS, CHUNK, CHUNK), of_layer(CHUNK, GROUP_W)],
        out_specs=pl.BlockSpec((1, ts, 3 * GROUP_W), lambda bi, i: (bi, i, 0)),
        out_shape=jax.ShapeDtypeStruct((b, s, 3 * GROUP_W), BF16),
        scratch_shapes=[pltpu.VMEM((ts + 2 * HALO, GROUP_W), F32),
                        pltpu.VMEM((SUBLANES, ts + 2 * HALO, GROUP_W), F32)],
        compiler_params=_params("parallel", "parallel"),
        name="conv_mixer",
    )(p, p, p, conv_params, sw_bf16, sbias)


def _head_masks():
    lane_head = lax.broadcasted_iota(jnp.int32, (1, GROUP_W), 1) // HEAD_W
    return [(lane_head == h).astype(BF16) for h in range(N_HEADS)]


_NT = (((1,), (1,)), ((), ()))


def _attend(q, masks, keys, values, biases, acc):
    for h in range(N_HEADS):
        qh = q * masks[h]
        scores = []
        for k, bias in zip(keys, biases):
            s = lax.dot_general(qh, k, _NT, preferred_element_type=F32)
            scores.append(s if bias is None else s + bias(h))
        m = scores[0].max(axis=-1, keepdims=True)
        for s in scores[1:]:
            m = jnp.maximum(m, s.max(axis=-1, keepdims=True))
        probs = [jnp.exp(s - m) for s in scores]
        denom = probs[0].sum(axis=-1, keepdims=True)
        for p in probs[1:]:
            denom = denom + p.sum(axis=-1, keepdims=True)
        o = jnp.dot(probs[0].astype(BF16), values[0](h), preferred_element_type=F32)
        for p, v in zip(probs[1:], values[1:]):
            o = o + jnp.dot(p.astype(BF16), v(h), preferred_element_type=F32)
        acc = acc + o / denom
    return acc


def _nbr_attn_body(q_ref, kp_ref, kc_ref, kn_ref, kx_ref, bias_ref, gg_ref, o_ref,
                   k_s, v_s, kx_s, vx_s, *, starts):
    masks = _head_masks()
    n_half = kp_ref.shape[1]
    n_cur = kc_ref.shape[1]
    pieces = ((kp_ref, 0, n_half), (kc_ref, n_half, n_cur), (kn_ref, n_half + n_cur, n_half))
    for ref, off, n in pieces:
        k_s[off:off + n] = ref[0, :, 0:GROUP_W]
        for h in range(N_HEADS):
            v_s[h, off:off + n] = ref[0, :, GROUP_W:] * masks[h]
    kx_s[...] = kx_ref[0, :, 0:GROUP_W]
    for h in range(N_HEADS):
        vx_s[h] = kx_ref[0, :, GROUP_W:] * masks[h]

    j = pl.program_id(1)
    nb = pl.num_programs(1)
    n_win = bias_ref.shape[4]
    tq = q_ref.shape[1] // len(starts[0])
    for sb, (first, mid, last) in enumerate(zip(*starts)):
        st = jnp.where(j == 0, first, jnp.where(j == nb - 1, last, mid)) * GRID_W
        st = pl.multiple_of(st, GRID_W)
        r0 = sb * tq
        acc = _attend(
            q_ref[0, r0:r0 + tq, :], masks,
            keys=[k_s[pl.ds(st, n_win)], kx_s[...]],
            values=[lambda h: v_s[h, pl.ds(st, n_win)], lambda h: vx_s[h]],
            biases=[lambda h: bias_ref[0, 0, h, r0:r0 + tq, :], None],
            acc=jnp.zeros((tq, GROUP_W), F32))
        o_ref[0, r0:r0 + tq, :] = _rms(acc, gg_ref[...]).astype(o_ref.dtype)


SUB_Q_ROWS = 4
SUB_KEY_ROWS = 12


def _block_kinds(rows):
    nb = rows // Q_BLOCK_ROWS
    return (0, min(1, nb - 1), nb - 1)


def _sub_window_starts(rows):
    n_key_rows = KEY_ROWS_BEFORE + Q_BLOCK_ROWS + KEY_ROWS_AFTER
    starts = []
    for jv in _block_kinds(rows):
        per_sub = []
        for sb in range(Q_BLOCK_ROWS // SUB_Q_ROWS):
            r = Q_BLOCK_ROWS * jv + SUB_Q_ROWS * sb + np.arange(SUB_Q_ROWS)
            ks = np.clip(r - WIN_R // 2, 0, rows - WIN_R) - Q_BLOCK_ROWS * jv + KEY_ROWS_BEFORE
            start = int(min(ks.min(), n_key_rows - SUB_KEY_ROWS))
            assert start <= ks.min() and ks.max() + WIN_R <= start + SUB_KEY_ROWS
            per_sub.append(start)
        starts.append(tuple(per_sub))
    return tuple(starts)


def _window_bias(rpb_l, rows):
    n_key_rows = KEY_ROWS_BEFORE + Q_BLOCK_ROWS + KEY_ROWS_AFTER
    sub_starts = _sub_window_starts(rows)
    i = np.arange(Q_BLOCK_ROWS)[:, None, None, None]
    c = np.arange(GRID_W)[None, :, None, None]
    kr = np.arange(-KEY_ROWS_BEFORE, Q_BLOCK_ROWS + KEY_ROWS_AFTER)[None, None, :, None]
    kc = np.arange(GRID_W)[None, None, None, :]
    shape = (Q_BLOCK_ROWS, GRID_W, n_key_rows, GRID_W)
    edge = GRID_W - WIN_C
    cols = jnp.pad(rpb_l, ((0, 0), (0, 0), (edge, edge)), mode="edge")
    cols = jnp.pad(cols, ((0, 0), (0, 0), (0, 1)))
    skew = jnp.broadcast_to(cols[:, :, None, :], cols.shape[:2] + (GRID_W, 2 * GRID_W))
    skew = skew.reshape(cols.shape[:2] + (2 * GRID_W * GRID_W,))[:, :, :GRID_W * (2 * GRID_W - 1)]
    by_col = skew.reshape(cols.shape[:2] + (GRID_W, 2 * GRID_W - 1))[..., GRID_W - 1:]
    lo = KEY_ROWS_BEFORE
    hi = Q_BLOCK_ROWS + KEY_ROWS_AFTER - WIN_R
    by_col = jnp.pad(by_col, ((0, 0), (lo, hi), (0, 0), (0, 0)))
    tables = []
    for jv, starts in zip(_block_kinds(rows), sub_starts):
        r = Q_BLOCK_ROWS * jv + i
        ks = np.clip(r - WIN_R // 2, 0, rows - WIN_R)
        krow = Q_BLOCK_ROWS * jv + kr
        c_start = np.clip(c - WIN_C // 2, 0, GRID_W - WIN_C)
        valid = np.broadcast_to((krow >= ks) & (krow < ks + WIN_R) & (krow >= 0) & (krow < rows)
                                & (kc >= c_start) & (kc < c_start + WIN_C), shape)
        n_win = SUB_KEY_ROWS * GRID_W
        per_row, per_row_valid = [], []
        for ii in range(Q_BLOCK_ROWS):
            st = starts[ii // SUB_Q_ROWS]
            first = WIN_R - 1 - ii + st
            piece = by_col[:, first:first + SUB_KEY_ROWS].transpose(0, 2, 1, 3)
            per_row.append(piece.reshape(N_HEADS, GRID_W, n_win))
            per_row_valid.append(valid[ii, :, st:st + SUB_KEY_ROWS, :].reshape(GRID_W, n_win))
        values = jnp.concatenate(per_row, axis=1)
        tables.append(jnp.where(np.concatenate(per_row_valid)[None], values, NEG_INF))
    return jnp.stack(tables)


def _nbr_attention(q, kv, kv_ctx, bias, layer, gg):
    b, s, _ = q.shape
    n_ctx = kv_ctx.shape[1]
    tb = Q_BLOCK_ROWS * GRID_W
    half = KEY_ROWS_BEFORE * GRID_W
    nb = s // tb
    assert nb >= 2 and KEY_ROWS_BEFORE == KEY_ROWS_AFTER and tb == 2 * half
    n_loc = tb + 2 * half
    n_win = SUB_KEY_ROWS * GRID_W
    variant = lambda j: jnp.where(j == 0, 0, jnp.where(j == nb - 1, 2, 1))
    return pl.pallas_call(
        functools.partial(_nbr_attn_body, starts=_sub_window_starts(s // GRID_W)),
        grid=(b, nb),
        in_specs=[pl.BlockSpec((1, tb, GROUP_W), lambda bi, j: (bi, j, 0)),
                  pl.BlockSpec((1, half, 2 * GROUP_W), lambda bi, j: (bi, jnp.maximum(2 * j - 1, 0), 0)),
                  pl.BlockSpec((1, tb, 2 * GROUP_W), lambda bi, j: (bi, j, 0)),
                  pl.BlockSpec((1, half, 2 * GROUP_W), lambda bi, j: (bi, jnp.minimum(2 * j + 2, 2 * nb - 1), 0)),
                  pl.BlockSpec((1, n_ctx, 2 * GROUP_W), lambda bi, j: (bi, 0, 0)),
                  pl.BlockSpec((1, 1, N_HEADS, tb, n_win), lambda bi, j: (layer, variant(j), 0, 0, 0)),
                  pl.BlockSpec((1, GROUP_W), lambda bi, j: (0, 0))],
        out_specs=pl.BlockSpec((1, tb, GROUP_W), lambda bi, j: (bi, j, 0)),
        out_shape=jax.ShapeDtypeStruct((b, s, GROUP_W), BF16),
        scratch_shapes=[pltpu.VMEM((n_loc, GROUP_W), BF16),
                        pltpu.VMEM((N_HEADS, n_loc, GROUP_W), BF16),
                        pltpu.VMEM((n_ctx, GROUP_W), BF16),
                        pltpu.VMEM((N_HEADS, n_ctx, GROUP_W), BF16)],
        compiler_params=_params("parallel", "arbitrary"),
        name="nbr_attention",
    )(q, kv, kv, kv, kv_ctx, bias, gg.reshape(1, GROUP_W))


def _ctx_attn_body(q_ref, kx_ref, gg_ref, o_ref, vx_s):
    masks = _head_masks()
    for h in range(N_HEADS):
        vx_s[h] = kx_ref[0, :, GROUP_W:] * masks[h]
    tq = q_ref.shape[1]
    acc = _attend(q_ref[0], masks, keys=[kx_ref[0, :, 0:GROUP_W]], values=[lambda h: vx_s[h]],
                  biases=[None], acc=jnp.zeros((tq, GROUP_W), F32))
    o_ref[0] = _rms(acc, gg_ref[...]).astype(o_ref.dtype)


def _ctx_attention(q, kv_ctx, gg):
    b, n_ctx, _ = q.shape
    return pl.pallas_call(
        _ctx_attn_body,
        grid=(b,),
        in_specs=[pl.BlockSpec((1, n_ctx, GROUP_W), lambda bi: (bi, 0, 0)),
                  pl.BlockSpec((1, n_ctx, 2 * GROUP_W), lambda bi: (bi, 0, 0)),
                  pl.BlockSpec((1, GROUP_W), lambda bi: (0, 0))],
        out_specs=pl.BlockSpec((1, n_ctx, GROUP_W), lambda bi: (bi, 0, 0)),
        out_shape=jax.ShapeDtypeStruct((b, n_ctx, GROUP_W), BF16),
        scratch_shapes=[pltpu.VMEM((N_HEADS, n_ctx, GROUP_W), BF16)],
        compiler_params=_params("parallel"),
        name="ctx_attention",
    )(q, kv_ctx, gg.reshape(1, GROUP_W))


def _mixer_residual(abc_ref, dn_ref, w_ref, x_ref, m):
    y = (jnp.dot(abc_ref[0], w_ref[0:3 * GROUP_W], preferred_element_type=F32)
         + jnp.dot(dn_ref[0], w_ref[3 * GROUP_W:], preferred_element_type=F32))
    return x_ref[0] + m[M_G1:M_G1 + 1] * y


def _merge_router_body(abc_ref, dn_ref, w_ref, x_ref, mod_ref, g_ref, rw_ref, rb_ref,
                       xo_ref, h_ref, info_ref, ids_ref):
    m = mod_ref[0]
    xn = _mixer_residual(abc_ref, dn_ref, w_ref, x_ref, m)
    xo_ref[0] = xn
    h = _modulated_norm(xn, g_ref[...], m, M_SH2, M_SC2)
    n_sub = h.shape[1] // LANES
    for s in range(n_sub):
        h_ref[0, pl.ds(s, h.shape[0], stride=n_sub), :] = h[:, s * LANES:(s + 1) * LANES]
    logits = jnp.dot(h.astype(BF16), rw_ref[...], preferred_element_type=F32) + rb_ref[...]
    lane = lax.broadcasted_iota(jnp.int32, logits.shape, 1).astype(F32)
    m1 = logits.max(axis=-1, keepdims=True)
    i1 = jnp.where(logits == m1, lane, float(LANES)).min(axis=-1, keepdims=True)
    rest_logits = jnp.where(lane == i1, NEG_INF, logits)
    m2 = rest_logits.max(axis=-1, keepdims=True)
    i2 = jnp.where(rest_logits == m2, lane, float(LANES)).min(axis=-1, keepdims=True)
    e2 = jnp.exp(m2 - m1)
    den = 1.0 + e2
    info = jnp.where(lane == 0, i1, jnp.where(lane == 1, i2,
           jnp.where(lane == 2, 1.0 / den, jnp.where(lane == 3, e2 / den, 0.0))))
    info_ref[0] = info
    ids_ref[0, 0] = info.T[0:SUBLANES, :]


def _merge_router(abc, dn, w_out_bf16, x, mod, mod_row, g2, router_w, router_b, tm):
    b, s, d = x.shape
    const = lambda *shape: pl.BlockSpec(shape, lambda bi, i: (0,) * len(shape))
    return pl.pallas_call(
        _merge_router_body,
        grid=(b, s // tm),
        in_specs=[pl.BlockSpec((1, tm, 3 * GROUP_W), lambda bi, i: (bi, i, 0)),
                  pl.BlockSpec((1, tm, GROUP_W), lambda bi, i: (bi, i, 0)),
                  const(4 * GROUP_W, d),
                  pl.BlockSpec((1, tm, d), lambda bi, i: (bi, i, 0)),
                  pl.BlockSpec((1, 6, d), lambda bi, i: (mod_row(bi), 0, 0)),
                  const(1, d), const(d, LANES), const(1, LANES)],
        out_specs=[pl.BlockSpec((1, tm, d), lambda bi, i: (bi, i, 0)),
                   pl.BlockSpec((1, tm * d // LANES, LANES), lambda bi, i: (bi, i, 0)),
                   pl.BlockSpec((1, tm, LANES), lambda bi, i: (bi, i, 0)),
                   pl.BlockSpec((1, 1, SUBLANES, tm), lambda bi, i: (bi, i, 0, 0))],
        out_shape=[jax.ShapeDtypeStruct((b, s, d), F32),
                   jax.ShapeDtypeStruct((b, s * d // LANES, LANES), F32),
                   jax.ShapeDtypeStruct((b, s, LANES), F32),
                   jax.ShapeDtypeStruct((b, s // tm, SUBLANES, tm), F32)],
        compiler_params=_params("parallel", "parallel"),
        name="merge_router",
    )(abc, dn, w_out_bf16, x, mod, g2.reshape(1, d), router_w, router_b)


def _swiglu_chunk(xb, w1_ref, w3_ref, w2_ref, c0, cw):
    a = jnp.dot(xb, w1_ref[:, c0:c0 + cw], preferred_element_type=F32)
    g = jnp.dot(xb, w3_ref[:, c0:c0 + cw], preferred_element_type=F32)
    act = (a * jax.nn.sigmoid(a) * g).astype(BF16)
    return jnp.dot(act, w2_ref[c0:c0 + cw, :], preferred_element_type=F32)


def _hidden_chunks(f, tf):
    return [(c0, min(tf, f - c0)) for c0 in range(0, f, tf)]


def _transition_body(abc_ref, dn_ref, wo_ref, x_ref, mod_ref, g2_ref, w1_ref, w3_ref, w2_ref,
                     modn_ref, g1n_ref, win_ref, cp_ref, sw_ref, sbias_ref,
                     xo_ref, abcn_ref, q_ref, kv_ref, acc_ref, pcur_s, ptail_s, za_s, zb_s, *, tf, rc):
    i = pl.program_id(1)
    n = pl.num_programs(1) - 1
    tm = pcur_s.shape[0]

    @pl.when(i == 0)
    def _():
        pcur_s[...] = jnp.zeros(pcur_s.shape, pcur_s.dtype)
        ptail_s[...] = jnp.zeros(ptail_s.shape, ptail_s.dtype)

    m = mod_ref[0]
    xn = _mixer_residual(abc_ref, dn_ref, wo_ref, x_ref, m)
    h = _modulated_norm(xn, g2_ref[...], m, M_SH2, M_SC2).astype(BF16)
    for k, (c0, cw) in enumerate(_hidden_chunks(w1_ref.shape[1], tf)):
        part = _swiglu_chunk(h, w1_ref, w3_ref, w2_ref, c0, cw)
        if k == 0:
            acc_ref[...] = part
        else:
            acc_ref[...] += part
    x_next = xn + m[M_G2:M_G2 + 1] * acc_ref[...]
    xo_ref[0] = x_next
    p_new = _input_projection(x_next, g1n_ref, modn_ref[0], win_ref, q_ref, kv_ref)

    tile = i - 1
    _conv_groups(pcur_s, ptail_s[...], p_new[0:HALO], (tile > 0).astype(F32), (tile < n - 1).astype(F32),
                 cp_ref[0], sw_ref, sbias_ref, abcn_ref, za_s, zb_s, ts=tm, rc=rc)
    ptail_s[...] = pcur_s[tm - HALO:tm]
    pcur_s[...] = p_new


def _layer_transition(abc, dn, w_out_bf16, x, mod, mod_next, mod_row, g2, ffn_w, g1_next, w_in_next,
                      conv_params, sw_bf16, sbias, next_layer, tm, tf):
    b, s, d = x.shape
    w1, w3, w2 = ffn_w
    f = w1.shape[1]
    n = s // tm
    rc = min(64, tm)
    resident = lambda *shape: pl.BlockSpec(shape, lambda bi, i: (0,) * len(shape), pipeline_mode=pl.Buffered(1))
    of_layer = lambda *shape: pl.BlockSpec((1,) + shape, lambda bi, i: (next_layer,) + (0,) * len(shape),
                                           pipeline_mode=pl.Buffered(1))
    rows = lambda width: pl.BlockSpec((1, tm, width), lambda bi, i: (bi, jnp.minimum(i, n - 1), 0))
    mod_spec = pl.BlockSpec((1, 6, d), lambda bi, i: (mod_row(bi), 0, 0))
    return pl.pallas_call(
        functools.partial(_transition_body, tf=tf, rc=rc),
        grid=(b, n + 1),
        in_specs=[rows(3 * GROUP_W), rows(GROUP_W), resident(4 * GROUP_W, d), rows(d), mod_spec, resident(1, d),
                  resident(d, f), resident(d, f), resident(f, d),
                  mod_spec, resident(1, d), resident(d, IN_COLS),
                  of_layer(CP_ROWS, GROUP_W), of_layer(N_HEADS, CHUNK, CHUNK), of_layer(CHUNK, GROUP_W)],
        out_specs=[rows(d),
                   pl.BlockSpec((1, tm, 3 * GROUP_W), lambda bi, i: (bi, jnp.maximum(i - 1, 0), 0)),
                   rows(GROUP_W), rows(2 * GROUP_W)],
        out_shape=[jax.ShapeDtypeStruct((b, s, d), F32),
                   jax.ShapeDtypeStruct((b, s, 3 * GROUP_W), BF16),
                   jax.ShapeDtypeStruct((b, s, GROUP_W), BF16),
                   jax.ShapeDtypeStruct((b, s, 2 * GROUP_W), BF16)],
        scratch_shapes=[pltpu.VMEM((tm, d), F32),
                        pltpu.VMEM((tm, OFF_D), F32),
                        pltpu.VMEM((HALO, OFF_D), F32),
                        pltpu.VMEM((tm + 2 * HALO, GROUP_W), F32),
                        pltpu.VMEM((SUBLANES, tm + 2 * HALO, GROUP_W), F32)],
        compiler_params=_params("parallel", "arbitrary"),
        name="layer_transition",
    )(abc, dn, w_out_bf16, x, mod, g2.reshape(1, d), w1, w3, w2, mod_next, g1_next.reshape(1, d), w_in_next,
      conv_params, sw_bf16, sbias)


WINDOW_ALIGN = 16


LOC_RADIX = 128


def _route_plan(expert_idx, tm, tt):
    n_tok = expert_idx.shape[1]
    n_assign = 2 * n_tok
    experts = jnp.arange(N_EXPERTS, dtype=jnp.int32)[:, None]
    picks = [(expert_idx[k][None, :] == experts).astype(jnp.int32) for k in range(2)]
    chosen = picks[0] + picks[1]
    csum = jnp.cumsum(chosen, axis=1)
    counts = csum[:, -1]
    earlier = csum - chosen
    padded = (counts + tm - 1) // tm * tm
    pad_end = jnp.cumsum(padded)
    pad_start = pad_end - padded
    slots = [jnp.sum(pick * (earlier + pad_start[:, None]), axis=0) for pick in picks]
    total = n_assign + N_EXPERTS * tm
    n_blk = total // tm
    tok = jnp.arange(n_tok, dtype=jnp.int32)
    slot_tok = jnp.zeros((total,), jnp.int32).at[jnp.concatenate(slots)].set(jnp.concatenate([tok, tok]))
    n_active = pad_end[-1] // tm
    blk = jnp.arange(n_blk, dtype=jnp.int32)
    blk_exp = jnp.minimum(jnp.searchsorted(pad_end, blk * tm, side="right"), N_EXPERTS - 1).astype(jnp.int32)
    blk_exp = jnp.where(blk < n_active, blk_exp, blk_exp[jnp.maximum(n_active - 1, 0)])

    wa = tt + WINDOW_ALIGN
    assert N_EXPERTS * wa <= LOC_RADIX * LOC_RADIX
    n_tiles = n_tok // tt
    before = jnp.concatenate([jnp.zeros((N_EXPERTS, 1), jnp.int32), csum[:, tt - 1::tt][:, :-1]], axis=1)
    first = pad_start[:, None] + before
    win_base = jnp.minimum(first // WINDOW_ALIGN * WINDOW_ALIGN, total - wa)
    win_row0 = experts * wa - win_base
    digits = []
    for pick, slot in zip(picks, slots):
        loc = slot.reshape(n_tiles, tt) + jnp.sum(pick.reshape(N_EXPERTS, n_tiles, tt) * win_row0[:, :, None], axis=0)
        digits += [loc // LOC_RADIX, loc % LOC_RADIX]
    loc_digits = jnp.stack(digits + [jnp.zeros_like(digits[0])] * 4, axis=1).astype(F32)
    return (slot_tok.reshape(n_blk, tm), blk_exp, n_active.astype(jnp.int32).reshape(1),
            win_base.T.reshape(n_tiles * N_EXPERTS).astype(jnp.int32), loc_digits)


ROW_ISSUE_UNROLL = 8


def _moe_body(bexp_ref, nact_ref, tok_hbm, h_hbm, w1_hbm, w3_hbm, w2_hbm, o_ref,
              tok_s, x_s, xb_s, acc_s, w1_s, w3_s, w2_s, st1_s, st3_s, st2_s,
              sem_idx, sem_row, sem_w, *, tm, tf):
    i = pl.program_id(0)
    n_active = nact_ref[0]
    chunks = _hidden_chunks(w1_s.shape[1], tf)
    rows_per_chunk = -(-tm // len(chunks))

    def idx_copy(step):
        return pltpu.make_async_copy(tok_hbm.at[step], tok_s.at[step % 2], sem_idx.at[step % 2])

    n_sub = h_hbm.shape[1]

    def row_copy(slot, r, tok):
        return pltpu.make_async_copy(h_hbm.at[tok], x_s.at[slot, pl.ds(r * n_sub, n_sub)], sem_row.at[slot])

    @pl.when(i == 0)
    def _():
        idx_copy(0).start()
        idx_copy(0).wait()

        def issue(r, c):
            row_copy(0, r, tok_s[0, r]).start()
            return c

        lax.fori_loop(0, tm, issue, 0, unroll=ROW_ISSUE_UNROLL)
        idx_copy(1).start()

    def weight_copies(expert, n):
        c0, cw = chunks[n]
        slot = n % 2
        return (pltpu.make_async_copy(w1_hbm.at[expert, :, pl.ds(c0, cw)], st1_s.at[slot, :, pl.ds(0, cw)],
                                      sem_w.at[slot]),
                pltpu.make_async_copy(w3_hbm.at[expert, :, pl.ds(c0, cw)], st3_s.at[slot, :, pl.ds(0, cw)],
                                      sem_w.at[slot]),
                pltpu.make_async_copy(w2_hbm.at[expert, pl.ds(c0, cw), :], st2_s.at[slot, pl.ds(0, cw), :],
                                      sem_w.at[slot]))

    def install_chunk(expert, n):
        c0, cw = chunks[n]
        slot = n % 2
        for cp in weight_copies(expert, n):
            cp.wait()
        w1_s[:, c0:c0 + cw] = st1_s[slot, :, 0:cw].astype(BF16)
        w3_s[:, c0:c0 + cw] = st3_s[slot, :, 0:cw].astype(BF16)
        w2_s[c0:c0 + cw, :] = st2_s[slot, 0:cw, :].astype(BF16)
        if n + 2 < len(chunks):
            for cp in weight_copies(expert, n + 2):
                cp.start()

    def start_install(expert):
        for n in range(min(2, len(chunks))):
            for cp in weight_copies(expert, n):
                cp.start()

    @pl.when(i == 0)
    def _():
        start_install(bexp_ref[0])
        for n in range(len(chunks)):
            install_chunk(bexp_ref[0], n)

    @pl.when(i <= n_active)
    def _():
        slot = i % 2
        pltpu.make_async_copy(x_s.at[slot], x_s.at[slot], sem_row.at[slot]).wait()
        for s in range(n_sub):
            xb_s[:, s * LANES:(s + 1) * LANES] = x_s[slot, pl.ds(s, tm, stride=n_sub), :].astype(BF16)

    next_expert = bexp_ref[jnp.minimum(i + 1, pl.num_programs(0) - 1)]
    expert_ends = (i + 1 < n_active) & (next_expert != bexp_ref[i])

    def multiply_block(install_next):
        nxt = (i + 1) % 2
        idx_copy(i + 1).wait()
        if install_next:
            start_install(next_expert)
        xb = xb_s[...]
        for n, (c0, cw) in enumerate(chunks):
            part = _swiglu_chunk(xb, w1_s, w3_s, w2_s, c0, cw)
            if n == 0:
                acc_s[...] = part
            else:
                acc_s[...] += part
            for r in range(n * rows_per_chunk, min((n + 1) * rows_per_chunk, tm)):
                row_copy(nxt, r, tok_s[nxt, r]).start()
            if install_next:
                install_chunk(next_expert, n)

        @pl.when(i + 2 <= n_active)
        def _():
            idx_copy(i + 2).start()

    @pl.when((i < n_active) & jnp.logical_not(expert_ends))
    def _():
        multiply_block(install_next=False)

    @pl.when((i < n_active) & expert_ends)
    def _():
        multiply_block(install_next=True)

    @pl.when(i >= n_active)
    def _():
        acc_s[...] = jnp.zeros(acc_s.shape, acc_s.dtype)

    o_ref[...] = acc_s[...].astype(o_ref.dtype)


def _moe_experts(h, slot_tok, blk_exp, n_active, w1, w3, w2, tm, tf):
    t, n_sub, _ = h.shape
    d = n_sub * LANES
    n_blk = slot_tok.shape[0]
    f = w1.shape[2]
    any_space = pl.BlockSpec(memory_space=pl.ANY)
    grid_spec = pltpu.PrefetchScalarGridSpec(
        num_scalar_prefetch=2,
        grid=(n_blk,),
        in_specs=[any_space, any_space, any_space, any_space, any_space],
        out_specs=pl.BlockSpec((tm, d), lambda i, bexp, nact: (i, 0)),
        scratch_shapes=[pltpu.SMEM((2, tm), jnp.int32),
                        pltpu.VMEM((2, tm * n_sub, LANES), F32),
                        pltpu.VMEM((tm, d), BF16),
                        pltpu.VMEM((tm, d), F32),
                        pltpu.VMEM((d, f), BF16), pltpu.VMEM((d, f), BF16), pltpu.VMEM((f, d), BF16),
                        pltpu.VMEM((2, d, tf), F32), pltpu.VMEM((2, d, tf), F32), pltpu.VMEM((2, tf, d), F32),
                        pltpu.SemaphoreType.DMA((2,)),
                        pltpu.SemaphoreType.DMA((2,)),
                        pltpu.SemaphoreType.DMA((2,))])
    return pl.pallas_call(
        functools.partial(_moe_body, tm=tm, tf=tf),
        grid_spec=grid_spec,
        out_shape=jax.ShapeDtypeStruct((n_blk * tm, d), BF16),
        compiler_params=_params("arbitrary"),
        name="moe_experts",
    )(blk_exp, n_active, slot_tok, h, w1, w3, w2)


COMBINE_SLOTS = 3


def _combine_body(base_ref, y_hbm, loc_ref, info_ref, x_ref, mod_ref, g_ref, o_ref, win_s, sem, *, wa):
    i = pl.program_id(0)
    n = pl.num_programs(0)

    def window_copy(step, e):
        slot = step % COMBINE_SLOTS
        base = pl.multiple_of(base_ref[step * N_EXPERTS + e], WINDOW_ALIGN)
        return pltpu.make_async_copy(y_hbm.at[pl.ds(base, wa)], win_s.at[slot, pl.ds(e * wa, wa)], sem.at[slot])

    def start_windows(step):
        for e in range(N_EXPERTS):
            window_copy(step, e).start()

    @pl.when(i == 0)
    def _():
        for step in range(COMBINE_SLOTS - 1):
            @pl.when(step < n)
            def _():
                start_windows(step)

    @pl.when(i + COMBINE_SLOTS - 1 < n)
    def _():
        start_windows(i + COMBINE_SLOTS - 1)

    for e in range(N_EXPERTS):
        window_copy(i, e).wait()

    info = info_ref[...]
    win = win_s[i % COMBINE_SLOTS]
    tt = info.shape[0]
    eye = (lax.broadcasted_iota(jnp.int32, (tt, tt), 0) == lax.broadcasted_iota(jnp.int32, (tt, tt), 1))
    digits = lax.dot_general(jnp.where(eye, 1.0, 0.0).astype(BF16), loc_ref[0].astype(BF16), _NT,
                             preferred_element_type=F32)
    loc0 = digits[:, 0:1] * LOC_RADIX + digits[:, 1:2]
    loc1 = digits[:, 2:3] * LOC_RADIX + digits[:, 3:4]
    row = lax.broadcasted_iota(jnp.int32, (tt, win.shape[0]), 1).astype(F32)
    pick0 = jnp.where(row == loc0, 1.0, 0.0).astype(BF16)
    pick1 = jnp.where(row == loc1, 1.0, 0.0).astype(BF16)
    f = (info[:, 2:3] * jnp.dot(pick0, win, preferred_element_type=F32)
         + info[:, 3:4] * jnp.dot(pick1, win, preferred_element_type=F32))
    xn = x_ref[...] + mod_ref[0, M_G2:M_G2 + 1] * f
    o_ref[...] = _rms(xn, g_ref[...])


def _combine_final(win_base, y, loc_digits, info, x, mod, mod_row, final_g, tt):
    t, d = x.shape
    wa = tt + WINDOW_ALIGN
    grid_spec = pltpu.PrefetchScalarGridSpec(
        num_scalar_prefetch=1,
        grid=(t // tt,),
        in_specs=[pl.BlockSpec(memory_space=pl.ANY),
                  pl.BlockSpec((1, SUBLANES, tt), lambda i, base: (i, 0, 0)),
                  pl.BlockSpec((tt, LANES), lambda i, base: (i, 0)),
                  pl.BlockSpec((tt, d), lambda i, base: (i, 0)),
                  pl.BlockSpec((1, 6, d), lambda i, base: (mod_row(i), 0, 0)),
                  pl.BlockSpec((1, d), lambda i, base: (0, 0))],
        out_specs=pl.BlockSpec((tt, d), lambda i, base: (i, 0)),
        scratch_shapes=[pltpu.VMEM((COMBINE_SLOTS, N_EXPERTS * wa, d), y.dtype),
                        pltpu.SemaphoreType.DMA((COMBINE_SLOTS,))])
    return pl.pallas_call(
        functools.partial(_combine_body, wa=wa),
        grid_spec=grid_spec,
        out_shape=jax.ShapeDtypeStruct((t, d), F32),
        compiler_params=_params("arbitrary"),
        name="moe_combine_final",
    )(win_base, y, loc_digits, info, x, mod, final_g.reshape(1, d))


class _Tiles(NamedTuple):
    rows: int
    hidden: int
    expert_rows: int
    combine_rows: int


def _choose_tiles(seq):
    return _Tiles(rows=min(512, seq), hidden=512, expert_rows=512, combine_rows=min(128, seq))


def kernel(x, c, ctx, c_ctx, w_mod, b_mod, norm1_g, norm2_g, w_in, conv_a_w, conv_b_w, conv_b_b, conv_ln_g, conv_ln_b, sgu_ln_g, sgu_ln_b, sgu_w, sgu_b, rpb, group_g, w_out, ffn_w1, ffn_w3, ffn_w2, router_w, router_b, moe_w1, moe_w3, moe_w2, final_g):
    bsz, seq, d = x.shape
    n_ctx = ctx.shape[1]
    depth = w_mod.shape[0]
    rows = seq // GRID_W
    assert d == D_MODEL and seq % (Q_BLOCK_ROWS * GRID_W) == 0 and n_ctx % CHUNK == 0 and bsz + 1 <= 8
    assert depth == 2, "layer 0 dense with context, layer 1 (last) MoE without context"

    tiles = _choose_tiles(seq)
    tm = tiles.rows
    lat_row = lambda bi: bi
    ctx_row = lambda bi: bsz

    c8 = jnp.concatenate([c, c_ctx[None], jnp.zeros((8 - bsz - 1, d), F32)], axis=0)
    mod_all = _modulation(c8, w_mod, b_mod).reshape(depth, 8, 6, d)

    conv_params = _conv_params(conv_a_w, conv_b_w, conv_b_b, conv_ln_g, conv_ln_b, sgu_ln_g, sgu_ln_b, group_g)
    sw_b = sgu_w.astype(BF16)
    sbias = jnp.repeat(jnp.swapaxes(sgu_b, 1, 2), HEAD_W, axis=2)
    bias_all = jax.vmap(lambda r: _window_bias(r, rows))(rpb)

    xl, xc = x, ctx
    p, q, kv = _norm_proj(xl, norm1_g[0], mod_all[0], lat_row, w_in[0].astype(BF16), tm)
    pc, qc, kvc = _norm_proj(xc, norm1_g[0], mod_all[0], ctx_row, w_in[0].astype(BF16), n_ctx)
    abc = _conv_mixer(p, conv_params, sw_b, sbias, 0, ts=tm)
    abc_c = _conv_mixer(pc, conv_params, sw_b, sbias, 0, ts=n_ctx)
    out = None
    for l in range(depth):
        last = l == depth - 1
        mod = mod_all[l]
        w_out_b = w_out[l].astype(BF16)
        gg_d = group_g[l, 3 * GROUP_W:]
        dn = _nbr_attention(q, kv, kvc, bias_all, l, gg_d)

        if not last:
            dn_c = _ctx_attention(qc, kvc, gg_d)
            ffn_w = tuple(w[l // 2].astype(BF16) for w in (ffn_w1, ffn_w3, ffn_w2))
            nxt = (mod_all[l + 1], norm1_g[l + 1], w_in[l + 1].astype(BF16))
            xl, abc, q, kv = _layer_transition(abc, dn, w_out_b, xl, mod, nxt[0], lat_row, norm2_g[l], ffn_w,
                                               nxt[1], nxt[2], conv_params, sw_b, sbias, l + 1, tm, tiles.hidden)
            xc, abc_c, qc, kvc = _layer_transition(abc_c, dn_c, w_out_b, xc, mod, nxt[0], ctx_row, norm2_g[l], ffn_w,
                                                   nxt[1], nxt[2], conv_params, sw_b, sbias, l + 1, n_ctx,
                                                   tiles.hidden)
        else:
            rw = jnp.zeros((d, LANES), F32).at[:, :N_EXPERTS].set(router_w[l // 2]).astype(BF16)
            rb = jnp.full((1, LANES), NEG_INF, F32).at[0, :N_EXPERTS].set(router_b[l // 2])
            xl, h, info, info_t = _merge_router(abc, dn, w_out_b, xl, mod, lat_row, norm2_g[l], rw, rb, tm)
            t = bsz * seq
            info = info.reshape(t, LANES)
            tt = tiles.combine_rows
            expert_idx = jnp.stack([info_t[:, :, k, :].reshape(t) for k in range(2)]).astype(jnp.int32)
            slot_tok, blk_exp, n_active, win_base, loc_digits = _route_plan(expert_idx, tiles.expert_rows, tt)
            w1, w3, w2 = moe_w1[l // 2], moe_w3[l // 2], moe_w2[l // 2]
            y = _moe_experts(h.reshape(t, d // LANES, LANES), slot_tok, blk_exp, n_active, w1, w3, w2,
                             tiles.expert_rows, tiles.hidden)
            out = _combine_final(win_base, y, loc_digits, info, xl.reshape(t, d), mod, lambda i: i * tt // seq,
                                 final_g, tt).reshape(bsz, seq, d)
    return out
```

```python
import functools
from typing import NamedTuple

import numpy as np
import jax
import jax.numpy as jnp
from jax import lax
from jax.experimental import pallas as pl
from jax.experimental.pallas import tpu as pltpu

F32 = jnp.float32
BF16 = jnp.bfloat16

D_MODEL = 1024
GRID_W = 64
EPS = 1e-6
NEG_INF = -1e30
GROUP_W = 256
N_HEADS = 4
HEAD_W = GROUP_W // N_HEADS
CONV_A = 3
CONV_B = 31
CHUNK = 128
WIN_R = 8
WIN_C = 16
OFF_B = 3 * GROUP_W
OFF_C = OFF_B + 2 * GROUP_W
OFF_D = OFF_C + 2 * GROUP_W
OFF_KV = OFF_D + GROUP_W
IN_COLS = OFF_KV + 2 * GROUP_W
N_EXPERTS = 8
LANES = 128
SUBLANES = 8
HALO = 16
Q_BLOCK_ROWS = 8
KEY_ROWS_BEFORE = 4
KEY_ROWS_AFTER = 4
VMEM_LIMIT = 56 * 1024 * 1024

M_SH1, M_SC1, M_G1, M_SH2, M_SC2, M_G2 = range(6)


def _params(*dims):
    return pltpu.CompilerParams(dimension_semantics=dims, vmem_limit_bytes=VMEM_LIMIT)


def _rms(x, g):
    return x * lax.rsqrt(jnp.mean(x * x, axis=-1, keepdims=True) + EPS) * g


def _layer_norm(x, g, b):
    mu = jnp.mean(x, axis=-1, keepdims=True)
    xc = x - mu
    var = jnp.mean(xc * xc, axis=-1, keepdims=True)
    return xc * lax.rsqrt(var + EPS) * g + b


def _mod_body(c_ref, w_ref, b_ref, o_ref):
    s = c_ref[...]
    s = s * jax.nn.sigmoid(s)
    o_ref[0] = jnp.dot(s, w_ref[0], preferred_element_type=F32,
                       precision=lax.Precision.HIGHEST) + b_ref[0]


def _modulation(c8, w_mod, b_mod):
    n_layers, d, six_d = w_mod.shape
    return pl.pallas_call(
        _mod_body,
        grid=(n_layers, six_d // d),
        in_specs=[pl.BlockSpec((8, d), lambda l, j: (0, 0)),
                  pl.BlockSpec((1, d, d), lambda l, j: (l, 0, j)),
                  pl.BlockSpec((1, 1, d), lambda l, j: (l, 0, j))],
        out_specs=pl.BlockSpec((1, 8, d), lambda l, j: (l, 0, j)),
        out_shape=jax.ShapeDtypeStruct((n_layers, 8, six_d), F32),
        compiler_params=_params("arbitrary", "arbitrary"),
        name="modulation",
    )(c8, w_mod, b_mod.reshape(n_layers, 1, six_d))


def _modulated_norm(x, g, m, shift, scale):
    return _rms(x, g) * (1.0 + m[scale:scale + 1]) + m[shift:shift + 1]


def _input_projection(x, g_ref, m, w_ref, q_ref, kv_ref):
    hb = _modulated_norm(x, g_ref[...], m, M_SH1, M_SC1).astype(BF16)
    q = jnp.dot(hb, w_ref[:, OFF_D:OFF_KV], preferred_element_type=F32)
    q_ref[0] = (q * (HEAD_W ** -0.5)).astype(BF16)
    kv_ref[0] = jnp.dot(hb, w_ref[:, OFF_KV:], preferred_element_type=F32).astype(BF16)
    return jnp.dot(hb, w_ref[:, :OFF_D], preferred_element_type=F32)


def _norm_proj_body(x_ref, g_ref, mod_ref, w_ref, p_ref, q_ref, kv_ref):
    p_ref[0] = _input_projection(x_ref[0], g_ref, mod_ref[0], w_ref, q_ref, kv_ref)


def _norm_proj(x, g, mod, mod_row, w_in_bf16, tm):
    b, s, d = x.shape
    return pl.pallas_call(
        _norm_proj_body,
        grid=(b, s // tm),
        in_specs=[pl.BlockSpec((1, tm, d), lambda bi, i: (bi, i, 0)),
                  pl.BlockSpec((1, d), lambda bi, i: (0, 0)),
                  pl.BlockSpec((1, 6, d), lambda bi, i: (mod_row(bi), 0, 0)),
                  pl.BlockSpec((d, IN_COLS), lambda bi, i: (0, 0))],
        out_specs=[pl.BlockSpec((1, tm, OFF_D), lambda bi, i: (bi, i, 0)),
                   pl.BlockSpec((1, tm, GROUP_W), lambda bi, i: (bi, i, 0)),
                   pl.BlockSpec((1, tm, 2 * GROUP_W), lambda bi, i: (bi, i, 0))],
        out_shape=[jax.ShapeDtypeStruct((b, s, OFF_D), F32),
                   jax.ShapeDtypeStruct((b, s, GROUP_W), BF16),
                   jax.ShapeDtypeStruct((b, s, 2 * GROUP_W), BF16)],
        compiler_params=_params("parallel", "parallel"),
        name="norm_proj",
    )(x, g.reshape(1, d), mod, w_in_bf16)


def _conv_inputs(blk):
    za = blk[:, GROUP_W:2 * GROUP_W] * blk[:, 2 * GROUP_W:3 * GROUP_W]
    zb = blk[:, OFF_B:OFF_B + GROUP_W] * jax.nn.sigmoid(blk[:, OFF_B + GROUP_W:OFF_C])
    return za, zb


CP_WA = 0
CP_WB = CP_WA + CONV_A
CP_BB = CP_WB + CONV_B
CP_BLG, CP_BLB, CP_SLG, CP_SLB = (CP_BB + k for k in range(1, 5))
CP_GG = CP_SLB + 1
CP_ROWS = -(-(CP_GG + 3) // SUBLANES) * SUBLANES


def _conv_groups(cur, prev_blk, next_blk, has_prev, has_next, cp, sw_ref, sbias_ref, o_ref, za_s, zb_s, *, ts, rc):
    pa, pb = _conv_inputs(prev_blk)
    za_s[0:HALO] = pa * has_prev
    zb_s[0, 0:HALO] = pb * has_prev
    ca, cb = _conv_inputs(cur[...])
    za_s[HALO:HALO + ts] = ca
    zb_s[0, HALO:HALO + ts] = cb
    na, nb = _conv_inputs(next_blk)
    za_s[HALO + ts:] = na * has_next
    zb_s[0, HALO + ts:] = nb * has_next
    n_shifted = ts + 2 * HALO - SUBLANES
    for b in range(1, SUBLANES):
        zb_s[b, 0:n_shifted] = zb_s[0, b:b + n_shifted]

    row = lambda r: cp[r:r + 1]
    wa = cp[CP_WA:CP_WA + CONV_A]
    wb = cp[CP_WB:CP_WB + CONV_B]
    lane_head = lax.broadcasted_iota(jnp.int32, (1, GROUP_W), 1) // HEAD_W
    for r0 in range(0, ts, rc):
        acc = wa[0:1] * za_s[HALO + r0 - 1:HALO + r0 - 1 + rc]
        for j in range(1, CONV_A):
            acc = acc + wa[j:j + 1] * za_s[HALO + r0 - 1 + j:HALO + r0 - 1 + j + rc]
        ya = cur[r0:r0 + rc, 0:GROUP_W] * acc
        o_ref[0, r0:r0 + rc, 0:GROUP_W] = _rms(ya, row(CP_GG)).astype(o_ref.dtype)

        base = HALO + r0 - CONV_B // 2
        acc = None
        for j in range(CONV_B):
            b, a = (base + j) % SUBLANES, (base + j) // SUBLANES * SUBLANES
            term = wb[j:j + 1] * zb_s[b, a:a + rc]
            acc = term if acc is None else acc + term
        yb = _layer_norm(acc + row(CP_BB), row(CP_BLG), row(CP_BLB))
        yb = yb * jax.nn.sigmoid(yb)
        o_ref[0, r0:r0 + rc, GROUP_W:2 * GROUP_W] = _rms(yb, row(CP_GG + 1)).astype(o_ref.dtype)

    for r0 in range(0, ts, CHUNK):
        z = jax.nn.gelu(cur[r0:r0 + CHUNK, OFF_C:OFF_D])
        u = z[:, :GROUP_W]
        v = _layer_norm(z[:, GROUP_W:], row(CP_SLG), row(CP_SLB)).astype(BF16)
        mixed = sbias_ref[0]
        for h in range(N_HEADS):
            vh = v * (lane_head == h).astype(BF16)
            mixed = mixed + jnp.dot(sw_ref[0, h], vh, preferred_element_type=F32)
        yc = u * mixed
        o_ref[0, r0:r0 + CHUNK, 2 * GROUP_W:3 * GROUP_W] = _rms(yc, row(CP_GG + 2)).astype(o_ref.dtype)


def _mixer_body(prev_ref, cur_ref, next_ref, cp_ref, sw_ref, sbias_ref, o_ref, za_s, zb_s, *, ts, rc):
    i = pl.program_id(1)
    n = pl.num_programs(1)
    _conv_groups(cur_ref.at[0], prev_ref[0], next_ref[0], (i > 0).astype(F32), (i < n - 1).astype(F32),
                 cp_ref[0], sw_ref, sbias_ref, o_ref, za_s, zb_s, ts=ts, rc=rc)


def _conv_params(conv_a_w, conv_b_w, conv_b_b, conv_ln_g, conv_ln_b, sgu_ln_g, sgu_ln_b, group_g):
    n_layers = conv_a_w.shape[0]
    vec = lambda a: a[:, None, :]
    pad = jnp.zeros((n_layers, CP_ROWS - CP_GG - 3, GROUP_W), F32)
    return jnp.concatenate([conv_a_w, conv_b_w, vec(conv_b_b), vec(conv_ln_g), vec(conv_ln_b), vec(sgu_ln_g),
                            vec(sgu_ln_b), group_g[:, :3 * GROUP_W].reshape(n_layers, 3, GROUP_W), pad], axis=1)


def _conv_mixer(p, conv_params, sw_bf16, sbias, layer, ts):
    b, s, _ = p.shape
    rc = min(64, ts)
    hb = ts // HALO
    n_halo = s // HALO
    of_layer = lambda *shape: pl.BlockSpec((1,) + shape, lambda bi, i: (layer,) + (0,) * len(shape))
    return pl.pallas_call(
        functools.partial(_mixer_body, ts=ts, rc=rc),
        grid=(b, s // ts),
        in_specs=[pl.BlockSpec((1, HALO, OFF_D), lambda bi, i: (bi, jnp.maximum(i * hb - 1, 0), 0)),
                  pl.BlockSpec((1, ts, OFF_D), lambda bi, i: (bi, i, 0)),
                  pl.BlockSpec((1, HALO, OFF_D), lambda bi, i: (bi, jnp.minimum((i + 1) * hb, n_halo - 1), 0)),
                  of_layer(CP_ROWS, GROUP_W), of_layer(N_HEADS, CHUNK, CHUNK), of_layer(CHUNK, GROUP_W)],
        out_specs=pl.BlockSpec((1, ts, 3 * GROUP_W), lambda bi, i: (bi, i, 0)),
        out_shape=jax.ShapeDtypeStruct((b, s, 3 * GROUP_W), BF16),
        scratch_shapes=[pltpu.VMEM((ts + 2 * HALO, GROUP_W), F32),
                        pltpu.VMEM((SUBLANES, ts + 2 * HALO, GROUP_W), F32)],
        compiler_params=_params("parallel", "parallel"),
        name="conv_mixer",
    )(p, p, p, conv_params, sw_bf16, sbias)


def _head_masks():
    lane_head = lax.broadcasted_iota(jnp.int32, (1, GROUP_W), 1) // HEAD_W
    return [(lane_head == h).astype(BF16) for h in range(N_HEADS)]


_NT = (((1,), (1,)), ((), ()))


def _attend(q, masks, keys, values, biases, acc):
    lane_head = lax.broadcasted_iota(jnp.int32, (1, GROUP_W), 1) // HEAD_W
    for h in range(N_HEADS):
        qh = q * masks[h]
        scores = []
        for k, bias in zip(keys, biases):
            s = lax.dot_general(qh, k, _NT, preferred_element_type=F32)
            scores.append(s if bias is None else s + bias(h))
        m = scores[0].max(axis=-1, keepdims=True)
        for s in scores[1:]:
            m = jnp.maximum(m, s.max(axis=-1, keepdims=True))
        probs = [jnp.exp(s - m) for s in scores]
        denom = probs[0].sum(axis=-1, keepdims=True)
        for p in probs[1:]:
            denom = denom + p.sum(axis=-1, keepdims=True)
        o = jnp.dot(probs[0].astype(BF16), values[0], preferred_element_type=F32)
        for p, v in zip(probs[1:], values[1:]):
            o = o + jnp.dot(p.astype(BF16), v, preferred_element_type=F32)
        acc = acc + jnp.where(lane_head == h, o / denom, 0.0)
    return acc


def _nbr_attn_body(q_ref, kp_ref, kc_ref, kn_ref, kx_ref, bias_ref, gg_ref, o_ref,
                   k_s, v_s, *, starts):
    masks = _head_masks()
    n_half = kp_ref.shape[1]
    n_cur = kc_ref.shape[1]
    pieces = ((kp_ref, 0, n_half), (kc_ref, n_half, n_cur), (kn_ref, n_half + n_cur, n_half))
    for ref, off, n in pieces:
        k_s[off:off + n] = ref[0, :, 0:GROUP_W]
        v_s[off:off + n] = ref[0, :, GROUP_W:]

    j = pl.program_id(1)
    nb = pl.num_programs(1)
    n_win = bias_ref.shape[4]
    tq = q_ref.shape[1] // len(starts[0])
    for sb, (first, mid, last) in enumerate(zip(*starts)):
        st = jnp.where(j == 0, first, jnp.where(j == nb - 1, last, mid)) * GRID_W
        st = pl.multiple_of(st, GRID_W)
        r0 = sb * tq
        acc = _attend(
            q_ref[0, r0:r0 + tq, :], masks,
            keys=[k_s[pl.ds(st, n_win)], kx_ref[0, :, 0:GROUP_W]],
            values=[v_s[pl.ds(st, n_win)], kx_ref[0, :, GROUP_W:]],
            biases=[lambda h: bias_ref[0, 0, h, r0:r0 + tq, :], None],
            acc=jnp.zeros((tq, GROUP_W), F32))
        o_ref[0, r0:r0 + tq, :] = _rms(acc, gg_ref[...]).astype(o_ref.dtype)


SUB_Q_ROWS = 4
SUB_KEY_ROWS = 12


def _block_kinds(rows):
    nb = rows // Q_BLOCK_ROWS
    return (0, min(1, nb - 1), nb - 1)


def _sub_window_starts(rows):
    n_key_rows = KEY_ROWS_BEFORE + Q_BLOCK_ROWS + KEY_ROWS_AFTER
    starts = []
    for jv in _block_kinds(rows):
        per_sub = []
        for sb in range(Q_BLOCK_ROWS // SUB_Q_ROWS):
            r = Q_BLOCK_ROWS * jv + SUB_Q_ROWS * sb + np.arange(SUB_Q_ROWS)
            ks = np.clip(r - WIN_R // 2, 0, rows - WIN_R) - Q_BLOCK_ROWS * jv + KEY_ROWS_BEFORE
            start = int(min(ks.min(), n_key_rows - SUB_KEY_ROWS))
            assert start <= ks.min() and ks.max() + WIN_R <= start + SUB_KEY_ROWS
            per_sub.append(start)
        starts.append(tuple(per_sub))
    return tuple(starts)


def _window_bias(rpb_l, rows):
    n_key_rows = KEY_ROWS_BEFORE + Q_BLOCK_ROWS + KEY_ROWS_AFTER
    sub_starts = _sub_window_starts(rows)
    i = np.arange(Q_BLOCK_ROWS)[:, None, None, None]
    c = np.arange(GRID_W)[None, :, None, None]
    kr = np.arange(-KEY_ROWS_BEFORE, Q_BLOCK_ROWS + KEY_ROWS_AFTER)[None, None, :, None]
    kc = np.arange(GRID_W)[None, None, None, :]
    shape = (Q_BLOCK_ROWS, GRID_W, n_key_rows, GRID_W)
    edge = GRID_W - WIN_C
    cols = jnp.pad(rpb_l, ((0, 0), (0, 0), (edge, edge)), mode="edge")
    cols = jnp.pad(cols, ((0, 0), (0, 0), (0, 1)))
    skew = jnp.broadcast_to(cols[:, :, None, :], cols.shape[:2] + (GRID_W, 2 * GRID_W))
    skew = skew.reshape(cols.shape[:2] + (2 * GRID_W * GRID_W,))[:, :, :GRID_W * (2 * GRID_W - 1)]
    by_col = skew.reshape(cols.shape[:2] + (GRID_W, 2 * GRID_W - 1))[..., GRID_W - 1:]
    lo = KEY_ROWS_BEFORE
    hi = Q_BLOCK_ROWS + KEY_ROWS_AFTER - WIN_R
    by_col = jnp.pad(by_col, ((0, 0), (lo, hi), (0, 0), (0, 0)))
    tables = []
    for jv, starts in zip(_block_kinds(rows), sub_starts):
        r = Q_BLOCK_ROWS * jv + i
        ks = np.clip(r - WIN_R // 2, 0, rows - WIN_R)
        krow = Q_BLOCK_ROWS * jv + kr
        c_start = np.clip(c - WIN_C // 2, 0, GRID_W - WIN_C)
        valid = np.broadcast_to((krow >= ks) & (krow < ks + WIN_R) & (krow >= 0) & (krow < rows)
                                & (kc >= c_start) & (kc < c_start + WIN_C), shape)
        n_win = SUB_KEY_ROWS * GRID_W
        per_row, per_row_valid = [], []
        for ii in range(Q_BLOCK_ROWS):
            st = starts[ii // SUB_Q_ROWS]
            first = WIN_R - 1 - ii + st
            piece = by_col[:, first:first + SUB_KEY_ROWS].transpose(0, 2, 1, 3)
            per_row.append(piece.reshape(N_HEADS, GRID_W, n_win))
            per_row_valid.append(valid[ii, :, st:st + SUB_KEY_ROWS, :].reshape(GRID_W, n_win))
        values = jnp.concatenate(per_row, axis=1)
        tables.append(jnp.where(np.concatenate(per_row_valid)[None], values, NEG_INF))
    return jnp.stack(tables)


def _nbr_attention(q, kv, kv_ctx, bias, layer, gg):
    b, s, _ = q.shape
    n_ctx = kv_ctx.shape[1]
    tb = Q_BLOCK_ROWS * GRID_W
    half = KEY_ROWS_BEFORE * GRID_W
    nb = s // tb
    assert nb >= 2 and KEY_ROWS_BEFORE == KEY_ROWS_AFTER and tb == 2 * half
    n_loc = tb + 2 * half
    n_win = SUB_KEY_ROWS * GRID_W
    variant = lambda j: jnp.where(j == 0, 0, jnp.where(j == nb - 1, 2, 1))
    return pl.pallas_call(
        functools.partial(_nbr_attn_body, starts=_sub_window_starts(s // GRID_W)),
        grid=(b, nb),
        in_specs=[pl.BlockSpec((1, tb, GROUP_W), lambda bi, j: (bi, j, 0)),
                  pl.BlockSpec((1, half, 2 * GROUP_W), lambda bi, j: (bi, jnp.maximum(2 * j - 1, 0), 0)),
                  pl.BlockSpec((1, tb, 2 * GROUP_W), lambda bi, j: (bi, j, 0)),
                  pl.BlockSpec((1, half, 2 * GROUP_W), lambda bi, j: (bi, jnp.minimum(2 * j + 2, 2 * nb - 1), 0)),
                  pl.BlockSpec((1, n_ctx, 2 * GROUP_W), lambda bi, j: (bi, 0, 0)),
                  pl.BlockSpec((1, 1, N_HEADS, tb, n_win), lambda bi, j: (layer, variant(j), 0, 0, 0)),
                  pl.BlockSpec((1, GROUP_W), lambda bi, j: (0, 0))],
        out_specs=pl.BlockSpec((1, tb, GROUP_W), lambda bi, j: (bi, j, 0)),
        out_shape=jax.ShapeDtypeStruct((b, s, GROUP_W), BF16),
        scratch_shapes=[pltpu.VMEM((n_loc, GROUP_W), BF16),
                        pltpu.VMEM((n_loc, GROUP_W), BF16)],
        compiler_params=_params("parallel", "arbitrary"),
        name="nbr_attention",
    )(q, kv, kv, kv, kv_ctx, bias, gg.reshape(1, GROUP_W))


def _ctx_attn_body(q_ref, kx_ref, gg_ref, o_ref):
    tq = q_ref.shape[1]
    acc = _attend(q_ref[0], _head_masks(), keys=[kx_ref[0, :, 0:GROUP_W]], values=[kx_ref[0, :, GROUP_W:]],
                  biases=[None], acc=jnp.zeros((tq, GROUP_W), F32))
    o_ref[0] = _rms(acc, gg_ref[...]).astype(o_ref.dtype)


def _ctx_attention(q, kv_ctx, gg):
    b, n_ctx, _ = q.shape
    return pl.pallas_call(
        _ctx_attn_body,
        grid=(b,),
        in_specs=[pl.BlockSpec((1, n_ctx, GROUP_W), lambda bi: (bi, 0, 0)),
                  pl.BlockSpec((1, n_ctx, 2 * GROUP_W), lambda bi: (bi, 0, 0)),
                  pl.BlockSpec((1, GROUP_W), lambda bi: (0, 0))],
        out_specs=pl.BlockSpec((1, n_ctx, GROUP_W), lambda bi: (bi, 0, 0)),
        out_shape=jax.ShapeDtypeStruct((b, n_ctx, GROUP_W), BF16),
        compiler_params=_params("parallel"),
        name="ctx_attention",
    )(q, kv_ctx, gg.reshape(1, GROUP_W))


def _mixer_residual(abc_ref, dn_ref, w_ref, x_ref, m):
    y = (jnp.dot(abc_ref[0], w_ref[0:3 * GROUP_W], preferred_element_type=F32)
         + jnp.dot(dn_ref[0], w_ref[3 * GROUP_W:], preferred_element_type=F32))
    return x_ref[0] + m[M_G1:M_G1 + 1] * y


def _merge_router_body(abc_ref, dn_ref, w_ref, x_ref, mod_ref, g_ref, rw_ref, rb_ref,
                       xo_ref, h_ref, info_ref, ids_ref):
    m = mod_ref[0]
    xn = _mixer_residual(abc_ref, dn_ref, w_ref, x_ref, m)
    xo_ref[0] = xn
    h = _modulated_norm(xn, g_ref[...], m, M_SH2, M_SC2)
    n_sub = h.shape[1] // LANES
    for s in range(n_sub):
        h_ref[0, pl.ds(s, h.shape[0], stride=n_sub), :] = h[:, s * LANES:(s + 1) * LANES]
    logits = jnp.dot(h.astype(BF16), rw_ref[...], preferred_element_type=F32) + rb_ref[...]
    lane = lax.broadcasted_iota(jnp.int32, logits.shape, 1).astype(F32)
    m1 = logits.max(axis=-1, keepdims=True)
    i1 = jnp.where(logits == m1, lane, float(LANES)).min(axis=-1, keepdims=True)
    rest_logits = jnp.where(lane == i1, NEG_INF, logits)
    m2 = rest_logits.max(axis=-1, keepdims=True)
    i2 = jnp.where(rest_logits == m2, lane, float(LANES)).min(axis=-1, keepdims=True)
    e2 = jnp.exp(m2 - m1)
    den = 1.0 + e2
    info = jnp.where(lane == 0, i1, jnp.where(lane == 1, i2,
           jnp.where(lane == 2, 1.0 / den, jnp.where(lane == 3, e2 / den, 0.0))))
    info_ref[0] = info
    ids_ref[0, 0] = info.T[0:SUBLANES, :]


def _merge_router(abc, dn, w_out_bf16, x, mod, mod_row, g2, router_w, router_b, tm):
    b, s, d = x.shape
    const = lambda *shape: pl.BlockSpec(shape, lambda bi, i: (0,) * len(shape))
    return pl.pallas_call(
        _merge_router_body,
        grid=(b, s // tm),
        in_specs=[pl.BlockSpec((1, tm, 3 * GROUP_W), lambda bi, i: (bi, i, 0)),
                  pl.BlockSpec((1, tm, GROUP_W), lambda bi, i: (bi, i, 0)),
                  const(4 * GROUP_W, d),
                  pl.BlockSpec((1, tm, d), lambda bi, i: (bi, i, 0)),
                  pl.BlockSpec((1, 6, d), lambda bi, i: (mod_row(bi), 0, 0)),
                  const(1, d), const(d, LANES), const(1, LANES)],
        out_specs=[pl.BlockSpec((1, tm, d), lambda bi, i: (bi, i, 0)),
                   pl.BlockSpec((1, tm * d // LANES, LANES), lambda bi, i: (bi, i, 0)),
                   pl.BlockSpec((1, tm, LANES), lambda bi, i: (bi, i, 0)),
                   pl.BlockSpec((1, 1, SUBLANES, tm), lambda bi, i: (bi, i, 0, 0))],
        out_shape=[jax.ShapeDtypeStruct((b, s, d), F32),
                   jax.ShapeDtypeStruct((b, s * d // LANES, LANES), F32),
                   jax.ShapeDtypeStruct((b, s, LANES), F32),
                   jax.ShapeDtypeStruct((b, s // tm, SUBLANES, tm), F32)],
        compiler_params=_params("parallel", "parallel"),
        name="merge_router",
    )(abc, dn, w_out_bf16, x, mod, g2.reshape(1, d), router_w, router_b)


def _swiglu_chunk(xb, w1_ref, w3_ref, w2_ref, c0, cw):
    a = jnp.dot(xb, w1_ref[:, c0:c0 + cw], preferred_element_type=F32)
    g = jnp.dot(xb, w3_ref[:, c0:c0 + cw], preferred_element_type=F32)
    act = (a * jax.nn.sigmoid(a) * g).astype(BF16)
    return jnp.dot(act, w2_ref[c0:c0 + cw, :], preferred_element_type=F32)


def _hidden_chunks(f, tf):
    return [(c0, min(tf, f - c0)) for c0 in range(0, f, tf)]


def _transition_body(abc_ref, dn_ref, wo_ref, x_ref, mod_ref, g2_ref, w1_ref, w3_ref, w2_ref,
                     modn_ref, g1n_ref, win_ref, cp_ref, sw_ref, sbias_ref,
                     xo_ref, abcn_ref, q_ref, kv_ref, acc_ref, pcur_s, ptail_s, za_s, zb_s, *, tf, rc):
    i = pl.program_id(1)
    n = pl.num_programs(1) - 1
    tm = pcur_s.shape[0]

    @pl.when(i == 0)
    def _():
        pcur_s[...] = jnp.zeros(pcur_s.shape, pcur_s.dtype)
        ptail_s[...] = jnp.zeros(ptail_s.shape, ptail_s.dtype)

    m = mod_ref[0]
    xn = _mixer_residual(abc_ref, dn_ref, wo_ref, x_ref, m)
    h = _modulated_norm(xn, g2_ref[...], m, M_SH2, M_SC2).astype(BF16)
    for k, (c0, cw) in enumerate(_hidden_chunks(w1_ref.shape[1], tf)):
        part = _swiglu_chunk(h, w1_ref, w3_ref, w2_ref, c0, cw)
        if k == 0:
            acc_ref[...] = part
        else:
            acc_ref[...] += part
    x_next = xn + m[M_G2:M_G2 + 1] * acc_ref[...]
    xo_ref[0] = x_next
    p_new = _input_projection(x_next, g1n_ref, modn_ref[0], win_ref, q_ref, kv_ref)

    tile = i - 1
    _conv_groups(pcur_s, ptail_s[...], p_new[0:HALO], (tile > 0).astype(F32), (tile < n - 1).astype(F32),
                 cp_ref[0], sw_ref, sbias_ref, abcn_ref, za_s, zb_s, ts=tm, rc=rc)
    ptail_s[...] = pcur_s[tm - HALO:tm]
    pcur_s[...] = p_new


def _layer_transition(abc, dn, w_out_bf16, x, mod, mod_next, mod_row, g2, ffn_w, g1_next, w_in_next,
                      conv_params, sw_bf16, sbias, next_layer, tm, tf):
    b, s, d = x.shape
    w1, w3, w2 = ffn_w
    f = w1.shape[1]
    n = s // tm
    rc = min(64, tm)
    resident = lambda *shape: pl.BlockSpec(shape, lambda bi, i: (0,) * len(shape), pipeline_mode=pl.Buffered(1))
    of_layer = lambda *shape: pl.BlockSpec((1,) + shape, lambda bi, i: (next_layer,) + (0,) * len(shape),
                                           pipeline_mode=pl.Buffered(1))
    rows = lambda width: pl.BlockSpec((1, tm, width), lambda bi, i: (bi, jnp.minimum(i, n - 1), 0))
    mod_spec = pl.BlockSpec((1, 6, d), lambda bi, i: (mod_row(bi), 0, 0))
    return pl.pallas_call(
        functools.partial(_transition_body, tf=tf, rc=rc),
        grid=(b, n + 1),
        in_specs=[rows(3 * GROUP_W), rows(GROUP_W), resident(4 * GROUP_W, d), rows(d), mod_spec, resident(1, d),
                  resident(d, f), resident(d, f), resident(f, d),
                  mod_spec, resident(1, d), resident(d, IN_COLS),
                  of_layer(CP_ROWS, GROUP_W), of_layer(N_HEADS, CHUNK, CHUNK), of_layer(CHUNK, GROUP_W)],
        out_specs=[rows(d),
                   pl.BlockSpec((1, tm, 3 * GROUP_W), lambda bi, i: (bi, jnp.maximum(i - 1, 0), 0)),
                   rows(GROUP_W), rows(2 * GROUP_W)],
        out_shape=[jax.ShapeDtypeStruct((b, s, d), F32),
                   jax.ShapeDtypeStruct((b, s, 3 * GROUP_W), BF16),
                   jax.ShapeDtypeStruct((b, s, GROUP_W), BF16),
                   jax.ShapeDtypeStruct((b, s, 2 * GROUP_W), BF16)],
        scratch_shapes=[pltpu.VMEM((tm, d), F32),
                        pltpu.VMEM((tm, OFF_D), F32),
                        pltpu.VMEM((HALO, OFF_D), F32),
                        pltpu.VMEM((tm + 2 * HALO, GROUP_W), F32),
                        pltpu.VMEM((SUBLANES, tm + 2 * HALO, GROUP_W), F32)],
        compiler_params=_params("parallel", "arbitrary"),
        name="layer_transition",
    )(abc, dn, w_out_bf16, x, mod, g2.reshape(1, d), w1, w3, w2, mod_next, g1_next.reshape(1, d), w_in_next,
      conv_params, sw_bf16, sbias)


WINDOW_ALIGN = 16


LOC_RADIX = 128


def _route_plan(expert_idx, tm, tt):
    n_tok = expert_idx.shape[1]
    n_assign = 2 * n_tok
    experts = jnp.arange(N_EXPERTS, dtype=jnp.int32)[:, None]
    picks = [(expert_idx[k][None, :] == experts).astype(jnp.int32) for k in range(2)]
    chosen = picks[0] + picks[1]
    csum = jnp.cumsum(chosen, axis=1)
    counts = csum[:, -1]
    earlier = csum - chosen
    padded = (counts + tm - 1) // tm * tm
    pad_end = jnp.cumsum(padded)
    pad_start = pad_end - padded
    slots = [jnp.sum(pick * (earlier + pad_start[:, None]), axis=0) for pick in picks]
    total = n_assign + N_EXPERTS * tm
    n_blk = total // tm
    tok = jnp.arange(n_tok, dtype=jnp.int32)
    slot_tok = jnp.zeros((total,), jnp.int32).at[jnp.concatenate(slots)].set(jnp.concatenate([tok, tok]))
    n_active = pad_end[-1] // tm
    blk = jnp.arange(n_blk, dtype=jnp.int32)
    blk_exp = jnp.minimum(jnp.searchsorted(pad_end, blk * tm, side="right"), N_EXPERTS - 1).astype(jnp.int32)
    blk_exp = jnp.where(blk < n_active, blk_exp, blk_exp[jnp.maximum(n_active - 1, 0)])

    wa = tt + WINDOW_ALIGN
    assert N_EXPERTS * wa <= LOC_RADIX * LOC_RADIX
    n_tiles = n_tok // tt
    before = jnp.concatenate([jnp.zeros((N_EXPERTS, 1), jnp.int32), csum[:, tt - 1::tt][:, :-1]], axis=1)
    first = pad_start[:, None] + before
    win_base = jnp.minimum(first // WINDOW_ALIGN * WINDOW_ALIGN, total - wa)
    win_row0 = experts * wa - win_base
    digits = []
    for pick, slot in zip(picks, slots):
        loc = slot.reshape(n_tiles, tt) + jnp.sum(pick.reshape(N_EXPERTS, n_tiles, tt) * win_row0[:, :, None], axis=0)
        digits += [loc // LOC_RADIX, loc % LOC_RADIX]
    loc_digits = jnp.stack(digits + [jnp.zeros_like(digits[0])] * 4, axis=1).astype(F32)
    return (slot_tok.reshape(n_blk, tm), blk_exp, n_active.astype(jnp.int32).reshape(1),
            win_base.T.reshape(n_tiles * N_EXPERTS).astype(jnp.int32), loc_digits)


ROW_ISSUE_UNROLL = 8


def _moe_body(bexp_ref, nact_ref, tok_hbm, h_hbm, w1_hbm, w3_hbm, w2_hbm, o_ref,
              tok_s, x_s, xb_s, acc_s, w1_s, w3_s, w2_s, st1_s, st3_s, st2_s,
              sem_idx, sem_row, sem_w, *, tm, tf):
    i = pl.program_id(0)
    n_active = nact_ref[0]
    chunks = _hidden_chunks(w1_s.shape[1], tf)
    rows_per_chunk = -(-tm // len(chunks))

    def idx_copy(step):
        return pltpu.make_async_copy(tok_hbm.at[step], tok_s.at[step % 2], sem_idx.at[step % 2])

    n_sub = h_hbm.shape[1]

    def row_copy(slot, r, tok):
        return pltpu.make_async_copy(h_hbm.at[tok], x_s.at[slot, pl.ds(r * n_sub, n_sub)], sem_row.at[slot])

    @pl.when(i == 0)
    def _():
        idx_copy(0).start()
        idx_copy(0).wait()

        def issue(r, c):
            row_copy(0, r, tok_s[0, r]).start()
            return c

        lax.fori_loop(0, tm, issue, 0, unroll=ROW_ISSUE_UNROLL)
        idx_copy(1).start()

    def weight_copies(expert, n):
        c0, cw = chunks[n]
        slot = n % 2
        return (pltpu.make_async_copy(w1_hbm.at[expert, :, pl.ds(c0, cw)], st1_s.at[slot, :, pl.ds(0, cw)],
                                      sem_w.at[slot]),
                pltpu.make_async_copy(w3_hbm.at[expert, :, pl.ds(c0, cw)], st3_s.at[slot, :, pl.ds(0, cw)],
                                      sem_w.at[slot]),
                pltpu.make_async_copy(w2_hbm.at[expert, pl.ds(c0, cw), :], st2_s.at[slot, pl.ds(0, cw), :],
                                      sem_w.at[slot]))

    def install_chunk(expert, n):
        c0, cw = chunks[n]
        slot = n % 2
        for cp in weight_copies(expert, n):
            cp.wait()
        w1_s[:, c0:c0 + cw] = st1_s[slot, :, 0:cw].astype(BF16)
        w3_s[:, c0:c0 + cw] = st3_s[slot, :, 0:cw].astype(BF16)
        w2_s[c0:c0 + cw, :] = st2_s[slot, 0:cw, :].astype(BF16)
        if n + 2 < len(chunks):
            for cp in weight_copies(expert, n + 2):
                cp.start()

    def start_install(expert):
        for n in range(min(2, len(chunks))):
            for cp in weight_copies(expert, n):
                cp.start()

    @pl.when(i == 0)
    def _():
        start_install(bexp_ref[0])
        for n in range(len(chunks)):
            install_chunk(bexp_ref[0], n)

    @pl.when(i <= n_active)
    def _():
        slot = i % 2
        pltpu.make_async_copy(x_s.at[slot], x_s.at[slot], sem_row.at[slot]).wait()
        for s in range(n_sub):
            xb_s[:, s * LANES:(s + 1) * LANES] = x_s[slot, pl.ds(s, tm, stride=n_sub), :].astype(BF16)

    next_expert = bexp_ref[jnp.minimum(i + 1, pl.num_programs(0) - 1)]
    expert_ends = (i + 1 < n_active) & (next_expert != bexp_ref[i])

    def multiply_block(install_next):
        nxt = (i + 1) % 2
        idx_copy(i + 1).wait()
        if install_next:
            start_install(next_expert)
        xb = xb_s[...]
        for n, (c0, cw) in enumerate(chunks):
            part = _swiglu_chunk(xb, w1_s, w3_s, w2_s, c0, cw)
            if n == 0:
                acc_s[...] = part
            else:
                acc_s[...] += part
            for r in range(n * rows_per_chunk, min((n + 1) * rows_per_chunk, tm)):
                row_copy(nxt, r, tok_s[nxt, r]).start()
            if install_next:
                install_chunk(next_expert, n)

        @pl.when(i + 2 <= n_active)
        def _():
            idx_copy(i + 2).start()

    @pl.when((i < n_active) & jnp.logical_not(expert_ends))
    def _():
        multiply_block(install_next=False)

    @pl.when((i < n_active) & expert_ends)
    def _():
        multiply_block(install_next=True)

    @pl.when(i >= n_active)
    def _():
        acc_s[...] = jnp.zeros(acc_s.shape, acc_s.dtype)

    o_ref[...] = acc_s[...].astype(o_ref.dtype)


def _moe_experts(h, slot_tok, blk_exp, n_active, w1, w3, w2, tm, tf):
    t, n_sub, _ = h.shape
    d = n_sub * LANES
    n_blk = slot_tok.shape[0]
    f = w1.shape[2]
    any_space = pl.BlockSpec(memory_space=pl.ANY)
    grid_spec = pltpu.PrefetchScalarGridSpec(
        num_scalar_prefetch=2,
        grid=(n_blk,),
        in_specs=[any_space, any_space, any_space, any_space, any_space],
        out_specs=pl.BlockSpec((tm, d), lambda i, bexp, nact: (i, 0)),
        scratch_shapes=[pltpu.SMEM((2, tm), jnp.int32),
                        pltpu.VMEM((2, tm * n_sub, LANES), F32),
                        pltpu.VMEM((tm, d), BF16),
                        pltpu.VMEM((tm, d), F32),
                        pltpu.VMEM((d, f), BF16), pltpu.VMEM((d, f), BF16), pltpu.VMEM((f, d), BF16),
                        pltpu.VMEM((2, d, tf), F32), pltpu.VMEM((2, d, tf), F32), pltpu.VMEM((2, tf, d), F32),
                        pltpu.SemaphoreType.DMA((2,)),
                        pltpu.SemaphoreType.DMA((2,)),
                        pltpu.SemaphoreType.DMA((2,))])
    return pl.pallas_call(
        functools.partial(_moe_body, tm=tm, tf=tf),
        grid_spec=grid_spec,
        out_shape=jax.ShapeDtypeStruct((n_blk * tm, d), BF16),
        compiler_params=_params("arbitrary"),
        name="moe_experts",
    )(blk_exp, n_active, slot_tok, h, w1, w3, w2)


COMBINE_SLOTS = 3


def _combine_body(base_ref, y_hbm, loc_ref, info_ref, x_ref, mod_ref, g_ref, o_ref, win_s, sem, *, wa):
    i = pl.program_id(0)
    n = pl.num_programs(0)

    def window_copy(step, e):
        slot = step % COMBINE_SLOTS
        base = pl.multiple_of(base_ref[step * N_EXPERTS + e], WINDOW_ALIGN)
        return pltpu.make_async_copy(y_hbm.at[pl.ds(base, wa)], win_s.at[slot, pl.ds(e * wa, wa)], sem.at[slot])

    def start_windows(step):
        for e in range(N_EXPERTS):
            window_copy(step, e).start()

    @pl.when(i == 0)
    def _():
        for step in range(COMBINE_SLOTS - 1):
            @pl.when(step < n)
            def _():
                start_windows(step)

    @pl.when(i + COMBINE_SLOTS - 1 < n)
    def _():
        start_windows(i + COMBINE_SLOTS - 1)

    for e in range(N_EXPERTS):
        window_copy(i, e).wait()

    info = info_ref[...]
    win = win_s[i % COMBINE_SLOTS]
    tt = info.shape[0]
    eye = (lax.broadcasted_iota(jnp.int32, (tt, tt), 0) == lax.broadcasted_iota(jnp.int32, (tt, tt), 1))
    digits = lax.dot_general(jnp.where(eye, 1.0, 0.0).astype(BF16), loc_ref[0].astype(BF16), _NT,
                             preferred_element_type=F32)
    loc0 = digits[:, 0:1] * LOC_RADIX + digits[:, 1:2]
    loc1 = digits[:, 2:3] * LOC_RADIX + digits[:, 3:4]
    row = lax.broadcasted_iota(jnp.int32, (tt, win.shape[0]), 1).astype(F32)
    pick0 = jnp.where(row == loc0, 1.0, 0.0).astype(BF16)
    pick1 = jnp.where(row == loc1, 1.0, 0.0).astype(BF16)
    f = (info[:, 2:3] * jnp.dot(pick0, win, preferred_element_type=F32)
         + info[:, 3:4] * jnp.dot(pick1, win, preferred_element_type=F32))
    xn = x_ref[...] + mod_ref[0, M_G2:M_G2 + 1] * f
    o_ref[...] = _rms(xn, g_ref[...])


def _combine_final(win_base, y, loc_digits, info, x, mod, mod_row, final_g, tt):
    t, d = x.shape
    wa = tt + WINDOW_ALIGN
    grid_spec = pltpu.PrefetchScalarGridSpec(
        num_scalar_prefetch=1,
        grid=(t // tt,),
        in_specs=[pl.BlockSpec(memory_space=pl.ANY),
                  pl.BlockSpec((1, SUBLANES, tt), lambda i, base: (i, 0, 0)),
                  pl.BlockSpec((tt, LANES), lambda i, base: (i, 0)),
                  pl.BlockSpec((tt, d), lambda i, base: (i, 0)),
                  pl.BlockSpec((1, 6, d), lambda i, base: (mod_row(i), 0, 0)),
                  pl.BlockSpec((1, d), lambda i, base: (0, 0))],
        out_specs=pl.BlockSpec((tt, d), lambda i, base: (i, 0)),
        scratch_shapes=[pltpu.VMEM((COMBINE_SLOTS, N_EXPERTS * wa, d), y.dtype),
                        pltpu.SemaphoreType.DMA((COMBINE_SLOTS,))])
    return pl.pallas_call(
        functools.partial(_combine_body, wa=wa),
        grid_spec=grid_spec,
        out_shape=jax.ShapeDtypeStruct((t, d), F32),
        compiler_params=_params("arbitrary"),
        name="moe_combine_final",
    )(win_base, y, loc_digits, info, x, mod, final_g.reshape(1, d))


class _Tiles(NamedTuple):
    rows: int
    transition_rows: int
    hidden: int
    expert_rows: int
    combine_rows: int


def _choose_tiles(seq):
    return _Tiles(rows=min(1024, seq), transition_rows=min(512, seq), hidden=512, expert_rows=512,
                  combine_rows=min(128, seq))


def kernel(x, c, ctx, c_ctx, w_mod, b_mod, norm1_g, norm2_g, w_in, conv_a_w, conv_b_w, conv_b_b, conv_ln_g, conv_ln_b, sgu_ln_g, sgu_ln_b, sgu_w, sgu_b, rpb, group_g, w_out, ffn_w1, ffn_w3, ffn_w2, router_w, router_b, moe_w1, moe_w3, moe_w2, final_g):
    bsz, seq, d = x.shape
    n_ctx = ctx.shape[1]
    depth = w_mod.shape[0]
    rows = seq // GRID_W
    assert d == D_MODEL and seq % (Q_BLOCK_ROWS * GRID_W) == 0 and n_ctx % CHUNK == 0 and bsz + 1 <= 8
    assert depth == 2, "layer 0 dense with context, layer 1 (last) MoE without context"

    tiles = _choose_tiles(seq)
    tm = tiles.rows
    lat_row = lambda bi: bi
    ctx_row = lambda bi: bsz

    c8 = jnp.concatenate([c, c_ctx[None], jnp.zeros((8 - bsz - 1, d), F32)], axis=0)
    mod_all = _modulation(c8, w_mod, b_mod).reshape(depth, 8, 6, d)

    conv_params = _conv_params(conv_a_w, conv_b_w, conv_b_b, conv_ln_g, conv_ln_b, sgu_ln_g, sgu_ln_b, group_g)
    sw_b = sgu_w.astype(BF16)
    sbias = jnp.repeat(jnp.swapaxes(sgu_b, 1, 2), HEAD_W, axis=2)
    bias_all = jax.vmap(lambda r: _window_bias(r, rows))(rpb)

    xl, xc = x, ctx
    p, q, kv = _norm_proj(xl, norm1_g[0], mod_all[0], lat_row, w_in[0].astype(BF16), tm)
    pc, qc, kvc = _norm_proj(xc, norm1_g[0], mod_all[0], ctx_row, w_in[0].astype(BF16), n_ctx)
    abc = _conv_mixer(p, conv_params, sw_b, sbias, 0, ts=tm)
    abc_c = _conv_mixer(pc, conv_params, sw_b, sbias, 0, ts=n_ctx)
    out = None
    for l in range(depth):
        last = l == depth - 1
        mod = mod_all[l]
        w_out_b = w_out[l].astype(BF16)
        gg_d = group_g[l, 3 * GROUP_W:]
        dn = _nbr_attention(q, kv, kvc, bias_all, l, gg_d)

        if not last:
            dn_c = _ctx_attention(qc, kvc, gg_d)
            ffn_w = tuple(w[l // 2].astype(BF16) for w in (ffn_w1, ffn_w3, ffn_w2))
            nxt = (mod_all[l + 1], norm1_g[l + 1], w_in[l + 1].astype(BF16))
            xl, abc, q, kv = _layer_transition(abc, dn, w_out_b, xl, mod, nxt[0], lat_row, norm2_g[l], ffn_w,
                                               nxt[1], nxt[2], conv_params, sw_b, sbias, l + 1,
                                               tiles.transition_rows, tiles.hidden)
            xc, abc_c, qc, kvc = _layer_transition(abc_c, dn_c, w_out_b, xc, mod, nxt[0], ctx_row, norm2_g[l], ffn_w,
                                                   nxt[1], nxt[2], conv_params, sw_b, sbias, l + 1, n_ctx,
                                                   tiles.hidden)
        else:
            rw = jnp.zeros((d, LANES), F32).at[:, :N_EXPERTS].set(router_w[l // 2]).astype(BF16)
            rb = jnp.full((1, LANES), NEG_INF, F32).at[0, :N_EXPERTS].set(router_b[l // 2])
            xl, h, info, info_t = _merge_router(abc, dn, w_out_b, xl, mod, lat_row, norm2_g[l], rw, rb, tm)
            t = bsz * seq
            info = info.reshape(t, LANES)
            tt = tiles.combine_rows
            expert_idx = jnp.stack([info_t[:, :, k, :].reshape(t) for k in range(2)]).astype(jnp.int32)
            slot_tok, blk_exp, n_active, win_base, loc_digits = _route_plan(expert_idx, tiles.expert_rows, tt)
            w1, w3, w2 = moe_w1[l // 2], moe_w3[l // 2], moe_w2[l // 2]
            y = _moe_experts(h.reshape(t, d // LANES, LANES), slot_tok, blk_exp, n_active, w1, w3, w2,
                             tiles.expert_rows, tiles.hidden)
            out = _combine_final(win_base, y, loc_digits, info, xl.reshape(t, d), mod, lambda i: i * tt // seq,
                                 final_g, tt).reshape(bsz, seq, d)
    return out
```

```python
import functools
from typing import NamedTuple

import numpy as np
import jax
import jax.numpy as jnp
from jax import lax
from jax.experimental import pallas as pl
from jax.experimental.pallas import tpu as pltpu

F32 = jnp.float32
BF16 = jnp.bfloat16

D_MODEL = 1024
GRID_W = 64
EPS = 1e-6
NEG_INF = -1e30
GROUP_W = 256
N_HEADS = 4
HEAD_W = GROUP_W // N_HEADS
CONV_A = 3
CONV_B = 31
CHUNK = 128
WIN_R = 8
WIN_C = 16
OFF_B = 3 * GROUP_W
OFF_C = OFF_B + 2 * GROUP_W
OFF_D = OFF_C + 2 * GROUP_W
OFF_KV = OFF_D + GROUP_W
IN_COLS = OFF_KV + 2 * GROUP_W
N_EXPERTS = 8
LANES = 128
SUBLANES = 8
HALO = 16
Q_BLOCK_ROWS = 8
KEY_ROWS_BEFORE = 4
KEY_ROWS_AFTER = 4
VMEM_LIMIT = 56 * 1024 * 1024

M_SH1, M_SC1, M_G1, M_SH2, M_SC2, M_G2 = range(6)


def _params(*dims):
    return pltpu.CompilerParams(dimension_semantics=dims, vmem_limit_bytes=VMEM_LIMIT)


def _rms(x, g):
    return x * lax.rsqrt(jnp.mean(x * x, axis=-1, keepdims=True) + EPS) * g


def _layer_norm(x, g, b):
    mu = jnp.mean(x, axis=-1, keepdims=True)
    xc = x - mu
    var = jnp.mean(xc * xc, axis=-1, keepdims=True)
    return xc * lax.rsqrt(var + EPS) * g + b


def _mod_body(c_ref, w_ref, b_ref, o_ref):
    s = c_ref[...]
    s = s * jax.nn.sigmoid(s)
    o_ref[0] = jnp.dot(s, w_ref[0], preferred_element_type=F32,
                       precision=lax.Precision.HIGHEST) + b_ref[0]


def _modulation(c8, w_mod, b_mod):
    n_layers, d, six_d = w_mod.shape
    return pl.pallas_call(
        _mod_body,
        grid=(n_layers, six_d // d),
        in_specs=[pl.BlockSpec((8, d), lambda l, j: (0, 0)),
                  pl.BlockSpec((1, d, d), lambda l, j: (l, 0, j)),
                  pl.BlockSpec((1, 1, d), lambda l, j: (l, 0, j))],
        out_specs=pl.BlockSpec((1, 8, d), lambda l, j: (l, 0, j)),
        out_shape=jax.ShapeDtypeStruct((n_layers, 8, six_d), F32),
        compiler_params=_params("arbitrary", "arbitrary"),
        name="modulation",
    )(c8, w_mod, b_mod.reshape(n_layers, 1, six_d))


def _modulated_norm(x, g, m, shift, scale):
    return _rms(x, g) * (1.0 + m[scale:scale + 1]) + m[shift:shift + 1]


def _input_projection(x, g_ref, m, w_ref, q_ref, kv_ref):
    hb = _modulated_norm(x, g_ref[...], m, M_SH1, M_SC1).astype(BF16)
    q = jnp.dot(hb, w_ref[:, OFF_D:OFF_KV], preferred_element_type=F32)
    q_ref[0] = (q * (HEAD_W ** -0.5)).astype(BF16)
    kv_ref[0] = jnp.dot(hb, w_ref[:, OFF_KV:], preferred_element_type=F32).astype(BF16)
    return jnp.dot(hb, w_ref[:, :OFF_D], preferred_element_type=F32)


def _norm_proj_body(x_ref, g_ref, mod_ref, w_ref, p_ref, q_ref, kv_ref):
    p_ref[0] = _input_projection(x_ref[0], g_ref, mod_ref[0], w_ref, q_ref, kv_ref)


def _norm_proj(x, g, mod, mod_row, w_in_bf16, tm):
    b, s, d = x.shape
    return pl.pallas_call(
        _norm_proj_body,
        grid=(b, s // tm),
        in_specs=[pl.BlockSpec((1, tm, d), lambda bi, i: (bi, i, 0)),
                  pl.BlockSpec((1, d), lambda bi, i: (0, 0)),
                  pl.BlockSpec((1, 6, d), lambda bi, i: (mod_row(bi), 0, 0)),
                  pl.BlockSpec((d, IN_COLS), lambda bi, i: (0, 0))],
        out_specs=[pl.BlockSpec((1, tm, OFF_D), lambda bi, i: (bi, i, 0)),
                   pl.BlockSpec((1, tm, GROUP_W), lambda bi, i: (bi, i, 0)),
                   pl.BlockSpec((1, tm, 2 * GROUP_W), lambda bi, i: (bi, i, 0))],
        out_shape=[jax.ShapeDtypeStruct((b, s, OFF_D), F32),
                   jax.ShapeDtypeStruct((b, s, GROUP_W), BF16),
                   jax.ShapeDtypeStruct((b, s, 2 * GROUP_W), BF16)],
        compiler_params=_params("parallel", "parallel"),
        name="norm_proj",
    )(x, g.reshape(1, d), mod, w_in_bf16)


def _conv_inputs(blk):
    za = blk[:, GROUP_W:2 * GROUP_W] * blk[:, 2 * GROUP_W:3 * GROUP_W]
    zb = blk[:, OFF_B:OFF_B + GROUP_W] * jax.nn.sigmoid(blk[:, OFF_B + GROUP_W:OFF_C])
    return za, zb


CP_WA = 0
CP_WB = CP_WA + CONV_A
CP_BB = CP_WB + CONV_B
CP_BLG, CP_BLB, CP_SLG, CP_SLB = (CP_BB + k for k in range(1, 5))
CP_GG = CP_SLB + 1
CP_ROWS = -(-(CP_GG + 3) // SUBLANES) * SUBLANES


def _conv_groups(cur, prev_blk, next_blk, has_prev, has_next, cp, sw_ref, sbias_ref, o_ref, za_s, zb_s, *, ts, rc):
    pa, pb = _conv_inputs(prev_blk)
    za_s[0:HALO] = pa * has_prev
    zb_s[0, 0:HALO] = pb * has_prev
    ca, cb = _conv_inputs(cur[...])
    za_s[HALO:HALO + ts] = ca
    zb_s[0, HALO:HALO + ts] = cb
    na, nb = _conv_inputs(next_blk)
    za_s[HALO + ts:] = na * has_next
    zb_s[0, HALO + ts:] = nb * has_next
    n_shifted = ts + 2 * HALO - SUBLANES
    for b in range(1, SUBLANES):
        zb_s[b, 0:n_shifted] = zb_s[0, b:b + n_shifted]

    row = lambda r: cp[r:r + 1]
    wa = cp[CP_WA:CP_WA + CONV_A]
    wb = cp[CP_WB:CP_WB + CONV_B]
    lane_head = lax.broadcasted_iota(jnp.int32, (1, GROUP_W), 1) // HEAD_W
    for r0 in range(0, ts, rc):
        acc = wa[0:1] * za_s[HALO + r0 - 1:HALO + r0 - 1 + rc]
        for j in range(1, CONV_A):
            acc = acc + wa[j:j + 1] * za_s[HALO + r0 - 1 + j:HALO + r0 - 1 + j + rc]
        ya = cur[r0:r0 + rc, 0:GROUP_W] * acc
        o_ref[0, r0:r0 + rc, 0:GROUP_W] = _rms(ya, row(CP_GG)).astype(o_ref.dtype)

        base = HALO + r0 - CONV_B // 2
        acc = None
        for j in range(CONV_B):
            b, a = (base + j) % SUBLANES, (base + j) // SUBLANES * SUBLANES
            term = wb[j:j + 1] * zb_s[b, a:a + rc]
            acc = term if acc is None else acc + term
        yb = _layer_norm(acc + row(CP_BB), row(CP_BLG), row(CP_BLB))
        yb = yb * jax.nn.sigmoid(yb)
        o_ref[0, r0:r0 + rc, GROUP_W:2 * GROUP_W] = _rms(yb, row(CP_GG + 1)).astype(o_ref.dtype)

    for r0 in range(0, ts, CHUNK):
        z = jax.nn.gelu(cur[r0:r0 + CHUNK, OFF_C:OFF_D])
        u = z[:, :GROUP_W]
        v = _layer_norm(z[:, GROUP_W:], row(CP_SLG), row(CP_SLB)).astype(BF16)
        mixed = sbias_ref[0]
        for h in range(N_HEADS):
            vh = v * (lane_head == h).astype(BF16)
            mixed = mixed + jnp.dot(sw_ref[0, h], vh, preferred_element_type=F32)
        yc = u * mixed
        o_ref[0, r0:r0 + CHUNK, 2 * GROUP_W:3 * GROUP_W] = _rms(yc, row(CP_GG + 2)).astype(o_ref.dtype)


def _mixer_body(prev_ref, cur_ref, next_ref, cp_ref, sw_ref, sbias_ref, o_ref, za_s, zb_s, *, ts, rc):
    i = pl.program_id(1)
    n = pl.num_programs(1)
    _conv_groups(cur_ref.at[0], prev_ref[0], next_ref[0], (i > 0).astype(F32), (i < n - 1).astype(F32),
                 cp_ref[0], sw_ref, sbias_ref, o_ref, za_s, zb_s, ts=ts, rc=rc)


def _conv_params(conv_a_w, conv_b_w, conv_b_b, conv_ln_g, conv_ln_b, sgu_ln_g, sgu_ln_b, group_g):
    n_layers = conv_a_w.shape[0]
    vec = lambda a: a[:, None, :]
    pad = jnp.zeros((n_layers, CP_ROWS - CP_GG - 3, GROUP_W), F32)
    return jnp.concatenate([conv_a_w, conv_b_w, vec(conv_b_b), vec(conv_ln_g), vec(conv_ln_b), vec(sgu_ln_g),
                            vec(sgu_ln_b), group_g[:, :3 * GROUP_W].reshape(n_layers, 3, GROUP_W), pad], axis=1)


def _conv_mixer(p, conv_params, sw_bf16, sbias, layer, ts):
    b, s, _ = p.shape
    rc = min(64, ts)
    hb = ts // HALO
    n_halo = s // HALO
    of_layer = lambda *shape: pl.BlockSpec((1,) + shape, lambda bi, i: (layer,) + (0,) * len(shape))
    return pl.pallas_call(
        functools.partial(_mixer_body, ts=ts, rc=rc),
        grid=(b, s // ts),
        in_specs=[pl.BlockSpec((1, HALO, OFF_D), lambda bi, i: (bi, jnp.maximum(i * hb - 1, 0), 0)),
                  pl.BlockSpec((1, ts, OFF_D), lambda bi, i: (bi, i, 0)),
                  pl.BlockSpec((1, HALO, OFF_D), lambda bi, i: (bi, jnp.minimum((i + 1) * hb, n_halo - 1), 0)),
                  of_layer(CP_ROWS, GROUP_W), of_layer(N_HEADS, CHUNK, CHUNK), of_layer(CHUNK, GROUP_W)],
        out_specs=pl.BlockSpec((1, ts, 3 * GROUP_W), lambda bi, i: (bi, i, 0)),
        out_shape=jax.ShapeDtypeStruct((b, s, 3 * GROUP_W), BF16),
        scratch_shapes=[pltpu.VMEM((ts + 2 * HALO, GROUP_W), F32),
                        pltpu.VMEM((SUBLANES, ts + 2 * HALO, GROUP_W), F32)],
        compiler_params=_params("parallel", "parallel"),
        name="conv_mixer",
    )(p, p, p, conv_params, sw_bf16, sbias)


def _head_masks():
    lane_head = lax.broadcasted_iota(jnp.int32, (1, GROUP_W), 1) // HEAD_W
    return [(lane_head == h).astype(BF16) for h in range(N_HEADS)]


_NT = (((1,), (1,)), ((), ()))


def _attend(q, masks, keys, values, biases, acc):
    lane_head = lax.broadcasted_iota(jnp.int32, (1, GROUP_W), 1) // HEAD_W
    for h in range(N_HEADS):
        qh = q * masks[h]
        scores = []
        for k, bias in zip(keys, biases):
            s = lax.dot_general(qh, k, _NT, preferred_element_type=F32)
            scores.append(s if bias is None else s + bias(h))
        m = scores[0].max(axis=-1, keepdims=True)
        for s in scores[1:]:
            m = jnp.maximum(m, s.max(axis=-1, keepdims=True))
        probs = [jnp.exp(s - m) for s in scores]
        denom = probs[0].sum(axis=-1, keepdims=True)
        for p in probs[1:]:
            denom = denom + p.sum(axis=-1, keepdims=True)
        o = jnp.dot(probs[0].astype(BF16), values[0], preferred_element_type=F32)
        for p, v in zip(probs[1:], values[1:]):
            o = o + jnp.dot(p.astype(BF16), v, preferred_element_type=F32)
        acc = acc + jnp.where(lane_head == h, o / denom, 0.0)
    return acc


def _nbr_attn_body(q_ref, kp_ref, kc_ref, kn_ref, kx_ref, bias_ref, gg_ref, o_ref,
                   k_s, v_s, *, starts):
    masks = _head_masks()
    n_half = kp_ref.shape[1]
    n_cur = kc_ref.shape[1]
    pieces = ((kp_ref, 0, n_half), (kc_ref, n_half, n_cur), (kn_ref, n_half + n_cur, n_half))
    for ref, off, n in pieces:
        k_s[off:off + n] = ref[0, :, 0:GROUP_W]
        v_s[off:off + n] = ref[0, :, GROUP_W:]

    j = pl.program_id(1)
    nb = pl.num_programs(1)
    n_win = bias_ref.shape[4]
    tq = q_ref.shape[1] // len(starts[0])
    for sb, (first, mid, last) in enumerate(zip(*starts)):
        st = jnp.where(j == 0, first, jnp.where(j == nb - 1, last, mid)) * GRID_W
        st = pl.multiple_of(st, GRID_W)
        r0 = sb * tq
        acc = _attend(
            q_ref[0, r0:r0 + tq, :], masks,
            keys=[k_s[pl.ds(st, n_win)], kx_ref[0, :, 0:GROUP_W]],
            values=[v_s[pl.ds(st, n_win)], kx_ref[0, :, GROUP_W:]],
            biases=[lambda h: bias_ref[0, 0, h, r0:r0 + tq, :], None],
            acc=jnp.zeros((tq, GROUP_W), F32))
        o_ref[0, r0:r0 + tq, :] = _rms(acc, gg_ref[...]).astype(o_ref.dtype)


SUB_Q_ROWS = 4
SUB_KEY_ROWS = 12


def _block_kinds(rows):
    nb = rows // Q_BLOCK_ROWS
    return (0, min(1, nb - 1), nb - 1)


def _sub_window_starts(rows):
    n_key_rows = KEY_ROWS_BEFORE + Q_BLOCK_ROWS + KEY_ROWS_AFTER
    starts = []
    for jv in _block_kinds(rows):
        per_sub = []
        for sb in range(Q_BLOCK_ROWS // SUB_Q_ROWS):
            r = Q_BLOCK_ROWS * jv + SUB_Q_ROWS * sb + np.arange(SUB_Q_ROWS)
            ks = np.clip(r - WIN_R // 2, 0, rows - WIN_R) - Q_BLOCK_ROWS * jv + KEY_ROWS_BEFORE
            start = int(min(ks.min(), n_key_rows - SUB_KEY_ROWS))
            assert start <= ks.min() and ks.max() + WIN_R <= start + SUB_KEY_ROWS
            per_sub.append(start)
        starts.append(tuple(per_sub))
    return tuple(starts)


def _window_bias(rpb_l, rows):
    n_key_rows = KEY_ROWS_BEFORE + Q_BLOCK_ROWS + KEY_ROWS_AFTER
    sub_starts = _sub_window_starts(rows)
    i = np.arange(Q_BLOCK_ROWS)[:, None, None, None]
    c = np.arange(GRID_W)[None, :, None, None]
    kr = np.arange(-KEY_ROWS_BEFORE, Q_BLOCK_ROWS + KEY_ROWS_AFTER)[None, None, :, None]
    kc = np.arange(GRID_W)[None, None, None, :]
    shape = (Q_BLOCK_ROWS, GRID_W, n_key_rows, GRID_W)
    edge = GRID_W - WIN_C
    cols = jnp.pad(rpb_l, ((0, 0), (0, 0), (edge, edge)), mode="edge")
    cols = jnp.pad(cols, ((0, 0), (0, 0), (0, 1)))
    skew = jnp.broadcast_to(cols[:, :, None, :], cols.shape[:2] + (GRID_W, 2 * GRID_W))
    skew = skew.reshape(cols.shape[:2] + (2 * GRID_W * GRID_W,))[:, :, :GRID_W * (2 * GRID_W - 1)]
    by_col = skew.reshape(cols.shape[:2] + (GRID_W, 2 * GRID_W - 1))[..., GRID_W - 1:]
    lo = KEY_ROWS_BEFORE
    hi = Q_BLOCK_ROWS + KEY_ROWS_AFTER - WIN_R
    by_col = jnp.pad(by_col, ((0, 0), (lo, hi), (0, 0), (0, 0)))
    tables = []
    for jv, starts in zip(_block_kinds(rows), sub_starts):
        r = Q_BLOCK_ROWS * jv + i
        ks = np.clip(r - WIN_R // 2, 0, rows - WIN_R)
        krow = Q_BLOCK_ROWS * jv + kr
        c_start = np.clip(c - WIN_C // 2, 0, GRID_W - WIN_C)
        valid = np.broadcast_to((krow >= ks) & (krow < ks + WIN_R) & (krow >= 0) & (krow < rows)
                                & (kc >= c_start) & (kc < c_start + WIN_C), shape)
        n_win = SUB_KEY_ROWS * GRID_W
        per_row, per_row_valid = [], []
        for ii in range(Q_BLOCK_ROWS):
            st = starts[ii // SUB_Q_ROWS]
            first = WIN_R - 1 - ii + st
            piece = by_col[:, first:first + SUB_KEY_ROWS].transpose(0, 2, 1, 3)
            per_row.append(piece.reshape(N_HEADS, GRID_W, n_win))
            per_row_valid.append(valid[ii, :, st:st + SUB_KEY_ROWS, :].reshape(GRID_W, n_win))
        values = jnp.concatenate(per_row, axis=1)
        tables.append(jnp.where(np.concatenate(per_row_valid)[None], values, NEG_INF))
    return jnp.stack(tables)


def _nbr_attention(q, kv, kv_ctx, bias, layer, gg):
    b, s, _ = q.shape
    n_ctx = kv_ctx.shape[1]
    tb = Q_BLOCK_ROWS * GRID_W
    half = KEY_ROWS_BEFORE * GRID_W
    nb = s // tb
    assert nb >= 2 and KEY_ROWS_BEFORE == KEY_ROWS_AFTER and tb == 2 * half
    n_loc = tb + 2 * half
    n_win = SUB_KEY_ROWS * GRID_W
    variant = lambda j: jnp.where(j == 0, 0, jnp.where(j == nb - 1, 2, 1))
    return pl.pallas_call(
        functools.partial(_nbr_attn_body, starts=_sub_window_starts(s // GRID_W)),
        grid=(b, nb),
        in_specs=[pl.BlockSpec((1, tb, GROUP_W), lambda bi, j: (bi, j, 0)),
                  pl.BlockSpec((1, half, 2 * GROUP_W), lambda bi, j: (bi, jnp.maximum(2 * j - 1, 0), 0)),
                  pl.BlockSpec((1, tb, 2 * GROUP_W), lambda bi, j: (bi, j, 0)),
                  pl.BlockSpec((1, half, 2 * GROUP_W), lambda bi, j: (bi, jnp.minimum(2 * j + 2, 2 * nb - 1), 0)),
                  pl.BlockSpec((1, n_ctx, 2 * GROUP_W), lambda bi, j: (bi, 0, 0)),
                  pl.BlockSpec((1, 1, N_HEADS, tb, n_win), lambda bi, j: (layer, variant(j), 0, 0, 0)),
                  pl.BlockSpec((1, GROUP_W), lambda bi, j: (0, 0))],
        out_specs=pl.BlockSpec((1, tb, GROUP_W), lambda bi, j: (bi, j, 0)),
        out_shape=jax.ShapeDtypeStruct((b, s, GROUP_W), BF16),
        scratch_shapes=[pltpu.VMEM((n_loc, GROUP_W), BF16),
                        pltpu.VMEM((n_loc, GROUP_W), BF16)],
        compiler_params=_params("parallel", "arbitrary"),
        name="nbr_attention",
    )(q, kv, kv, kv, kv_ctx, bias, gg.reshape(1, GROUP_W))


def _ctx_attn_body(q_ref, kx_ref, gg_ref, o_ref):
    tq = q_ref.shape[1]
    acc = _attend(q_ref[0], _head_masks(), keys=[kx_ref[0, :, 0:GROUP_W]], values=[kx_ref[0, :, GROUP_W:]],
                  biases=[None], acc=jnp.zeros((tq, GROUP_W), F32))
    o_ref[0] = _rms(acc, gg_ref[...]).astype(o_ref.dtype)


def _ctx_attention(q, kv_ctx, gg):
    b, n_ctx, _ = q.shape
    return pl.pallas_call(
        _ctx_attn_body,
        grid=(b,),
        in_specs=[pl.BlockSpec((1, n_ctx, GROUP_W), lambda bi: (bi, 0, 0)),
                  pl.BlockSpec((1, n_ctx, 2 * GROUP_W), lambda bi: (bi, 0, 0)),
                  pl.BlockSpec((1, GROUP_W), lambda bi: (0, 0))],
        out_specs=pl.BlockSpec((1, n_ctx, GROUP_W), lambda bi: (bi, 0, 0)),
        out_shape=jax.ShapeDtypeStruct((b, n_ctx, GROUP_W), BF16),
        compiler_params=_params("parallel"),
        name="ctx_attention",
    )(q, kv_ctx, gg.reshape(1, GROUP_W))


def _mixer_residual(abc_ref, dn_ref, w_ref, x_ref, m):
    y = (jnp.dot(abc_ref[0], w_ref[0:3 * GROUP_W], preferred_element_type=F32)
         + jnp.dot(dn_ref[0], w_ref[3 * GROUP_W:], preferred_element_type=F32))
    return x_ref[0] + m[M_G1:M_G1 + 1] * y


def _merge_router_body(abc_ref, dn_ref, w_ref, x_ref, mod_ref, g_ref, rw_ref, rb_ref,
                       xo_ref, h_ref, info_ref, ids_ref):
    m = mod_ref[0]
    xn = _mixer_residual(abc_ref, dn_ref, w_ref, x_ref, m)
    xo_ref[0] = xn
    h = _modulated_norm(xn, g_ref[...], m, M_SH2, M_SC2)
    n_sub = h.shape[1] // LANES
    for s in range(n_sub):
        h_ref[0, pl.ds(s, h.shape[0], stride=n_sub), :] = h[:, s * LANES:(s + 1) * LANES]
    logits = jnp.dot(h.astype(BF16), rw_ref[...], preferred_element_type=F32) + rb_ref[...]
    lane = lax.broadcasted_iota(jnp.int32, logits.shape, 1).astype(F32)
    m1 = logits.max(axis=-1, keepdims=True)
    i1 = jnp.where(logits == m1, lane, float(LANES)).min(axis=-1, keepdims=True)
    rest_logits = jnp.where(lane == i1, NEG_INF, logits)
    m2 = rest_logits.max(axis=-1, keepdims=True)
    i2 = jnp.where(rest_logits == m2, lane, float(LANES)).min(axis=-1, keepdims=True)
    e2 = jnp.exp(m2 - m1)
    den = 1.0 + e2
    info = jnp.where(lane == 0, i1, jnp.where(lane == 1, i2,
           jnp.where(lane == 2, 1.0 / den, jnp.where(lane == 3, e2 / den, 0.0))))
    info_ref[0] = info
    ids_ref[0, 0] = info.T[0:SUBLANES, :]


def _merge_router(abc, dn, w_out_bf16, x, mod, mod_row, g2, router_w, router_b, tm):
    b, s, d = x.shape
    const = lambda *shape: pl.BlockSpec(shape, lambda bi, i: (0,) * len(shape))
    return pl.pallas_call(
        _merge_router_body,
        grid=(b, s // tm),
        in_specs=[pl.BlockSpec((1, tm, 3 * GROUP_W), lambda bi, i: (bi, i, 0)),
                  pl.BlockSpec((1, tm, GROUP_W), lambda bi, i: (bi, i, 0)),
                  const(4 * GROUP_W, d),
                  pl.BlockSpec((1, tm, d), lambda bi, i: (bi, i, 0)),
                  pl.BlockSpec((1, 6, d), lambda bi, i: (mod_row(bi), 0, 0)),
                  const(1, d), const(d, LANES), const(1, LANES)],
        out_specs=[pl.BlockSpec((1, tm, d), lambda bi, i: (bi, i, 0)),
                   pl.BlockSpec((1, tm * d // LANES, LANES), lambda bi, i: (bi, i, 0)),
                   pl.BlockSpec((1, tm, LANES), lambda bi, i: (bi, i, 0)),
                   pl.BlockSpec((1, 1, SUBLANES, tm), lambda bi, i: (bi, i, 0, 0))],
        out_shape=[jax.ShapeDtypeStruct((b, s, d), F32),
                   jax.ShapeDtypeStruct((b, s * d // LANES, LANES), F32),
                   jax.ShapeDtypeStruct((b, s, LANES), F32),
                   jax.ShapeDtypeStruct((b, s // tm, SUBLANES, tm), F32)],
        compiler_params=_params("parallel", "parallel"),
        name="merge_router",
    )(abc, dn, w_out_bf16, x, mod, g2.reshape(1, d), router_w, router_b)


def _swiglu_chunk(xb, w1_ref, w3_ref, w2_ref, c0, cw):
    a = jnp.dot(xb, w1_ref[:, c0:c0 + cw], preferred_element_type=F32)
    g = jnp.dot(xb, w3_ref[:, c0:c0 + cw], preferred_element_type=F32)
    act = (a * jax.nn.sigmoid(a) * g).astype(BF16)
    return jnp.dot(act, w2_ref[c0:c0 + cw, :], preferred_element_type=F32)


def _hidden_chunks(f, tf):
    return [(c0, min(tf, f - c0)) for c0 in range(0, f, tf)]


def _transition_body(abc_ref, dn_ref, wo_ref, x_ref, mod_ref, g2_ref, w1_ref, w3_ref, w2_ref,
                     modn_ref, g1n_ref, win_ref, cp_ref, sw_ref, sbias_ref,
                     xo_ref, abcn_ref, q_ref, kv_ref, acc_ref, pcur_s, ptail_s, za_s, zb_s, *, tf, rc):
    i = pl.program_id(1)
    n = pl.num_programs(1) - 1
    tm = pcur_s.shape[0]

    @pl.when(i == 0)
    def _():
        pcur_s[...] = jnp.zeros(pcur_s.shape, pcur_s.dtype)
        ptail_s[...] = jnp.zeros(ptail_s.shape, ptail_s.dtype)

    m = mod_ref[0]
    xn = _mixer_residual(abc_ref, dn_ref, wo_ref, x_ref, m)
    h = _modulated_norm(xn, g2_ref[...], m, M_SH2, M_SC2).astype(BF16)
    for k, (c0, cw) in enumerate(_hidden_chunks(w1_ref.shape[1], tf)):
        part = _swiglu_chunk(h, w1_ref, w3_ref, w2_ref, c0, cw)
        if k == 0:
            acc_ref[...] = part
        else:
            acc_ref[...] += part
    x_next = xn + m[M_G2:M_G2 + 1] * acc_ref[...]
    xo_ref[0] = x_next
    p_new = _input_projection(x_next, g1n_ref, modn_ref[0], win_ref, q_ref, kv_ref)

    tile = i - 1
    _conv_groups(pcur_s, ptail_s[...], p_new[0:HALO], (tile > 0).astype(F32), (tile < n - 1).astype(F32),
                 cp_ref[0], sw_ref, sbias_ref, abcn_ref, za_s, zb_s, ts=tm, rc=rc)
    ptail_s[...] = pcur_s[tm - HALO:tm]
    pcur_s[...] = p_new


def _layer_transition(abc, dn, w_out_bf16, x, mod, mod_next, mod_row, g2, ffn_w, g1_next, w_in_next,
                      conv_params, sw_bf16, sbias, next_layer, tm, tf):
    b, s, d = x.shape
    w1, w3, w2 = ffn_w
    f = w1.shape[1]
    n = s // tm
    rc = min(64, tm)
    resident = lambda *shape: pl.BlockSpec(shape, lambda bi, i: (0,) * len(shape), pipeline_mode=pl.Buffered(1))
    of_layer = lambda *shape: pl.BlockSpec((1,) + shape, lambda bi, i: (next_layer,) + (0,) * len(shape),
                                           pipeline_mode=pl.Buffered(1))
    rows = lambda width: pl.BlockSpec((1, tm, width), lambda bi, i: (bi, jnp.minimum(i, n - 1), 0))
    mod_spec = pl.BlockSpec((1, 6, d), lambda bi, i: (mod_row(bi), 0, 0))
    return pl.pallas_call(
        functools.partial(_transition_body, tf=tf, rc=rc),
        grid=(b, n + 1),
        in_specs=[rows(3 * GROUP_W), rows(GROUP_W), resident(4 * GROUP_W, d), rows(d), mod_spec, resident(1, d),
                  resident(d, f), resident(d, f), resident(f, d),
                  mod_spec, resident(1, d), resident(d, IN_COLS),
                  of_layer(CP_ROWS, GROUP_W), of_layer(N_HEADS, CHUNK, CHUNK), of_layer(CHUNK, GROUP_W)],
        out_specs=[rows(d),
                   pl.BlockSpec((1, tm, 3 * GROUP_W), lambda bi, i: (bi, jnp.maximum(i - 1, 0), 0)),
                   rows(GROUP_W), rows(2 * GROUP_W)],
        out_shape=[jax.ShapeDtypeStruct((b, s, d), F32),
                   jax.ShapeDtypeStruct((b, s, 3 * GROUP_W), BF16),
                   jax.ShapeDtypeStruct((b, s, GROUP_W), BF16),
                   jax.ShapeDtypeStruct((b, s, 2 * GROUP_W), BF16)],
        scratch_shapes=[pltpu.VMEM((tm, d), F32),
                        pltpu.VMEM((tm, OFF_D), F32),
                        pltpu.VMEM((HALO, OFF_D), F32),
                        pltpu.VMEM((tm + 2 * HALO, GROUP_W), F32),
                        pltpu.VMEM((SUBLANES, tm + 2 * HALO, GROUP_W), F32)],
        compiler_params=_params("parallel", "arbitrary"),
        name="layer_transition",
    )(abc, dn, w_out_bf16, x, mod, g2.reshape(1, d), w1, w3, w2, mod_next, g1_next.reshape(1, d), w_in_next,
      conv_params, sw_bf16, sbias)


WINDOW_ALIGN = 16


LOC_RADIX = 128


def _route_plan(expert_idx, tm, tt):
    n_tok = expert_idx.shape[1]
    n_assign = 2 * n_tok
    experts = jnp.arange(N_EXPERTS, dtype=jnp.int32)[:, None]
    picks = [(expert_idx[k][None, :] == experts).astype(jnp.int32) for k in range(2)]
    chosen = picks[0] + picks[1]
    csum = jnp.cumsum(chosen, axis=1)
    counts = csum[:, -1]
    earlier = csum - chosen
    padded = (counts + tm - 1) // tm * tm
    pad_end = jnp.cumsum(padded)
    pad_start = pad_end - padded
    slots = [jnp.sum(pick * (earlier + pad_start[:, None]), axis=0) for pick in picks]
    total = n_assign + N_EXPERTS * tm
    n_blk = total // tm
    tok = jnp.arange(n_tok, dtype=jnp.int32)
    slot_tok = jnp.zeros((total,), jnp.int32).at[jnp.concatenate(slots)].set(jnp.concatenate([tok, tok]))
    n_active = pad_end[-1] // tm
    blk = jnp.arange(n_blk, dtype=jnp.int32)
    blk_exp = jnp.minimum(jnp.searchsorted(pad_end, blk * tm, side="right"), N_EXPERTS - 1).astype(jnp.int32)
    blk_exp = jnp.where(blk < n_active, blk_exp, blk_exp[jnp.maximum(n_active - 1, 0)])

    wa = tt + WINDOW_ALIGN
    assert N_EXPERTS * wa <= LOC_RADIX * LOC_RADIX
    n_tiles = n_tok // tt
    before = jnp.concatenate([jnp.zeros((N_EXPERTS, 1), jnp.int32), csum[:, tt - 1::tt][:, :-1]], axis=1)
    first = pad_start[:, None] + before
    win_base = jnp.minimum(first // WINDOW_ALIGN * WINDOW_ALIGN, total - wa)
    win_row0 = experts * wa - win_base
    digits = []
    for pick, slot in zip(picks, slots):
        loc = slot.reshape(n_tiles, tt) + jnp.sum(pick.reshape(N_EXPERTS, n_tiles, tt) * win_row0[:, :, None], axis=0)
        digits += [loc // LOC_RADIX, loc % LOC_RADIX]
    loc_digits = jnp.stack(digits + [jnp.zeros_like(digits[0])] * 4, axis=1).astype(F32)
    return (slot_tok.reshape(n_blk, tm), blk_exp, n_active.astype(jnp.int32).reshape(1),
            win_base.T.reshape(n_tiles * N_EXPERTS).astype(jnp.int32), loc_digits)


ROW_ISSUE_UNROLL = 8


def _moe_body(bexp_ref, nact_ref, tok_hbm, h_hbm, w1_hbm, w3_hbm, w2_hbm, o_ref,
              tok_s, x_s, xb_s, acc_s, w1_s, w3_s, w2_s, st1_s, st3_s, st2_s,
              sem_idx, sem_row, sem_w, *, tm, tf):
    i = pl.program_id(0)
    n_active = nact_ref[0]
    chunks = _hidden_chunks(w1_s.shape[1], tf)
    rows_per_chunk = -(-tm // len(chunks))

    def idx_copy(step):
        return pltpu.make_async_copy(tok_hbm.at[step], tok_s.at[step % 2], sem_idx.at[step % 2])

    n_sub = h_hbm.shape[1]

    def row_copy(slot, r, tok):
        return pltpu.make_async_copy(h_hbm.at[tok], x_s.at[slot, pl.ds(r * n_sub, n_sub)], sem_row.at[slot])

    @pl.when(i == 0)
    def _():
        idx_copy(0).start()
        idx_copy(0).wait()

        def issue(r, c):
            row_copy(0, r, tok_s[0, r]).start()
            return c

        lax.fori_loop(0, tm, issue, 0, unroll=ROW_ISSUE_UNROLL)
        idx_copy(1).start()

    def weight_copies(expert, n):
        c0, cw = chunks[n]
        slot = n % 2
        return (pltpu.make_async_copy(w1_hbm.at[expert, :, pl.ds(c0, cw)], st1_s.at[slot, :, pl.ds(0, cw)],
                                      sem_w.at[slot]),
                pltpu.make_async_copy(w3_hbm.at[expert, :, pl.ds(c0, cw)], st3_s.at[slot, :, pl.ds(0, cw)],
                                      sem_w.at[slot]),
                pltpu.make_async_copy(w2_hbm.at[expert, pl.ds(c0, cw), :], st2_s.at[slot, pl.ds(0, cw), :],
                                      sem_w.at[slot]))

    def install_chunk(expert, n):
        c0, cw = chunks[n]
        slot = n % 2
        for cp in weight_copies(expert, n):
            cp.wait()
        w1_s[:, c0:c0 + cw] = st1_s[slot, :, 0:cw].astype(BF16)
        w3_s[:, c0:c0 + cw] = st3_s[slot, :, 0:cw].astype(BF16)
        w2_s[c0:c0 + cw, :] = st2_s[slot, 0:cw, :].astype(BF16)
        if n + 2 < len(chunks):
            for cp in weight_copies(expert, n + 2):
                cp.start()

    def start_install(expert):
        for n in range(min(2, len(chunks))):
            for cp in weight_copies(expert, n):
                cp.start()

    @pl.when(i == 0)
    def _():
        start_install(bexp_ref[0])
        for n in range(len(chunks)):
            install_chunk(bexp_ref[0], n)

    @pl.when(i <= n_active)
    def _():
        slot = i % 2
        pltpu.make_async_copy(x_s.at[slot], x_s.at[slot], sem_row.at[slot]).wait()
        for s in range(n_sub):
            xb_s[:, s * LANES:(s + 1) * LANES] = x_s[slot, pl.ds(s, tm, stride=n_sub), :].astype(BF16)

    next_expert = bexp_ref[jnp.minimum(i + 1, pl.num_programs(0) - 1)]
    expert_ends = (i + 1 < n_active) & (next_expert != bexp_ref[i])

    def multiply_block(install_next):
        nxt = (i + 1) % 2
        idx_copy(i + 1).wait()
        if install_next:
            start_install(next_expert)
        xb = xb_s[...]
        for n, (c0, cw) in enumerate(chunks):
            part = _swiglu_chunk(xb, w1_s, w3_s, w2_s, c0, cw)
            if n == 0:
                acc_s[...] = part
            else:
                acc_s[...] += part
            for r in range(n * rows_per_chunk, min((n + 1) * rows_per_chunk, tm)):
                row_copy(nxt, r, tok_s[nxt, r]).start(priority=r % 2)
            if install_next:
                install_chunk(next_expert, n)

        @pl.when(i + 2 <= n_active)
        def _():
            idx_copy(i + 2).start()

    @pl.when((i < n_active) & jnp.logical_not(expert_ends))
    def _():
        multiply_block(install_next=False)

    @pl.when((i < n_active) & expert_ends)
    def _():
        multiply_block(install_next=True)

    @pl.when(i >= n_active)
    def _():
        acc_s[...] = jnp.zeros(acc_s.shape, acc_s.dtype)

    o_ref[...] = acc_s[...].astype(o_ref.dtype)


def _moe_experts(h, slot_tok, blk_exp, n_active, w1, w3, w2, tm, tf):
    t, n_sub, _ = h.shape
    d = n_sub * LANES
    n_blk = slot_tok.shape[0]
    f = w1.shape[2]
    any_space = pl.BlockSpec(memory_space=pl.ANY)
    grid_spec = pltpu.PrefetchScalarGridSpec(
        num_scalar_prefetch=2,
        grid=(n_blk,),
        in_specs=[any_space, any_space, any_space, any_space, any_space],
        out_specs=pl.BlockSpec((tm, d), lambda i, bexp, nact: (i, 0)),
        scratch_shapes=[pltpu.SMEM((2, tm), jnp.int32),
                        pltpu.VMEM((2, tm * n_sub, LANES), F32),
                        pltpu.VMEM((tm, d), BF16),
                        pltpu.VMEM((tm, d), F32),
                        pltpu.VMEM((d, f), BF16), pltpu.VMEM((d, f), BF16), pltpu.VMEM((f, d), BF16),
                        pltpu.VMEM((2, d, tf), F32), pltpu.VMEM((2, d, tf), F32), pltpu.VMEM((2, tf, d), F32),
                        pltpu.SemaphoreType.DMA((2,)),
                        pltpu.SemaphoreType.DMA((2,)),
                        pltpu.SemaphoreType.DMA((2,))])
    return pl.pallas_call(
        functools.partial(_moe_body, tm=tm, tf=tf),
        grid_spec=grid_spec,
        out_shape=jax.ShapeDtypeStruct((n_blk * tm, d), BF16),
        compiler_params=_params("arbitrary"),
        name="moe_experts",
    )(blk_exp, n_active, slot_tok, h, w1, w3, w2)


COMBINE_SLOTS = 3


def _combine_body(base_ref, y_hbm, loc_ref, info_ref, x_ref, mod_ref, g_ref, o_ref, win_s, sem, *, wa):
    i = pl.program_id(0)
    n = pl.num_programs(0)

    def window_copy(step, e):
        slot = step % COMBINE_SLOTS
        base = pl.multiple_of(base_ref[step * N_EXPERTS + e], WINDOW_ALIGN)
        return pltpu.make_async_copy(y_hbm.at[pl.ds(base, wa)], win_s.at[slot, pl.ds(e * wa, wa)], sem.at[slot])

    def start_windows(step):
        for e in range(N_EXPERTS):
            window_copy(step, e).start()

    @pl.when(i == 0)
    def _():
        for step in range(COMBINE_SLOTS - 1):
            @pl.when(step < n)
            def _():
                start_windows(step)

    @pl.when(i + COMBINE_SLOTS - 1 < n)
    def _():
        start_windows(i + COMBINE_SLOTS - 1)

    for e in range(N_EXPERTS):
        window_copy(i, e).wait()

    info = info_ref[...]
    win = win_s[i % COMBINE_SLOTS]
    tt = info.shape[0]
    eye = (lax.broadcasted_iota(jnp.int32, (tt, tt), 0) == lax.broadcasted_iota(jnp.int32, (tt, tt), 1))
    digits = lax.dot_general(jnp.where(eye, 1.0, 0.0).astype(BF16), loc_ref[0].astype(BF16), _NT,
                             preferred_element_type=F32)
    loc0 = digits[:, 0:1] * LOC_RADIX + digits[:, 1:2]
    loc1 = digits[:, 2:3] * LOC_RADIX + digits[:, 3:4]
    row = lax.broadcasted_iota(jnp.int32, (tt, win.shape[0]), 1).astype(F32)
    pick0 = jnp.where(row == loc0, 1.0, 0.0).astype(BF16)
    pick1 = jnp.where(row == loc1, 1.0, 0.0).astype(BF16)
    f = (info[:, 2:3] * jnp.dot(pick0, win, preferred_element_type=F32)
         + info[:, 3:4] * jnp.dot(pick1, win, preferred_element_type=F32))
    xn = x_ref[...] + mod_ref[0, M_G2:M_G2 + 1] * f
    o_ref[...] = _rms(xn, g_ref[...])


def _combine_final(win_base, y, loc_digits, info, x, mod, mod_row, final_g, tt):
    t, d = x.shape
    wa = tt + WINDOW_ALIGN
    grid_spec = pltpu.PrefetchScalarGridSpec(
        num_scalar_prefetch=1,
        grid=(t // tt,),
        in_specs=[pl.BlockSpec(memory_space=pl.ANY),
                  pl.BlockSpec((1, SUBLANES, tt), lambda i, base: (i, 0, 0)),
                  pl.BlockSpec((tt, LANES), lambda i, base: (i, 0)),
                  pl.BlockSpec((tt, d), lambda i, base: (i, 0)),
                  pl.BlockSpec((1, 6, d), lambda i, base: (mod_row(i), 0, 0)),
                  pl.BlockSpec((1, d), lambda i, base: (0, 0))],
        out_specs=pl.BlockSpec((tt, d), lambda i, base: (i, 0)),
        scratch_shapes=[pltpu.VMEM((COMBINE_SLOTS, N_EXPERTS * wa, d), y.dtype),
                        pltpu.SemaphoreType.DMA((COMBINE_SLOTS,))])
    return pl.pallas_call(
        functools.partial(_combine_body, wa=wa),
        grid_spec=grid_spec,
        out_shape=jax.ShapeDtypeStruct((t, d), F32),
        compiler_params=_params("arbitrary"),
        name="moe_combine_final",
    )(win_base, y, loc_digits, info, x, mod, final_g.reshape(1, d))


class _Tiles(NamedTuple):
    rows: int
    transition_rows: int
    hidden: int
    expert_rows: int
    combine_rows: int


def _choose_tiles(seq):
    return _Tiles(rows=min(1024, seq), transition_rows=min(512, seq), hidden=512, expert_rows=512,
                  combine_rows=min(128, seq))


def kernel(x, c, ctx, c_ctx, w_mod, b_mod, norm1_g, norm2_g, w_in, conv_a_w, conv_b_w, conv_b_b, conv_ln_g, conv_ln_b, sgu_ln_g, sgu_ln_b, sgu_w, sgu_b, rpb, group_g, w_out, ffn_w1, ffn_w3, ffn_w2, router_w, router_b, moe_w1, moe_w3, moe_w2, final_g):
    bsz, seq, d = x.shape
    n_ctx = ctx.shape[1]
    depth = w_mod.shape[0]
    rows = seq // GRID_W
    assert d == D_MODEL and seq % (Q_BLOCK_ROWS * GRID_W) == 0 and n_ctx % CHUNK == 0 and bsz + 1 <= 8
    assert depth == 2, "layer 0 dense with context, layer 1 (last) MoE without context"

    tiles = _choose_tiles(seq)
    tm = tiles.rows
    lat_row = lambda bi: bi
    ctx_row = lambda bi: bsz

    c8 = jnp.concatenate([c, c_ctx[None], jnp.zeros((8 - bsz - 1, d), F32)], axis=0)
    mod_all = _modulation(c8, w_mod, b_mod).reshape(depth, 8, 6, d)

    conv_params = _conv_params(conv_a_w, conv_b_w, conv_b_b, conv_ln_g, conv_ln_b, sgu_ln_g, sgu_ln_b, group_g)
    sw_b = sgu_w.astype(BF16)
    sbias = jnp.repeat(jnp.swapaxes(sgu_b, 1, 2), HEAD_W, axis=2)
    bias_all = jax.vmap(lambda r: _window_bias(r, rows))(rpb)

    xl, xc = x, ctx
    p, q, kv = _norm_proj(xl, norm1_g[0], mod_all[0], lat_row, w_in[0].astype(BF16), tm)
    pc, qc, kvc = _norm_proj(xc, norm1_g[0], mod_all[0], ctx_row, w_in[0].astype(BF16), n_ctx)
    abc = _conv_mixer(p, conv_params, sw_b, sbias, 0, ts=tm)
    abc_c = _conv_mixer(pc, conv_params, sw_b, sbias, 0, ts=n_ctx)
    out = None
    for l in range(depth):
        last = l == depth - 1
        mod = mod_all[l]
        w_out_b = w_out[l].astype(BF16)
        gg_d = group_g[l, 3 * GROUP_W:]
        dn = _nbr_attention(q, kv, kvc, bias_all, l, gg_d)

        if not last:
            dn_c = _ctx_attention(qc, kvc, gg_d)
            ffn_w = tuple(w[l // 2].astype(BF16) for w in (ffn_w1, ffn_w3, ffn_w2))
            nxt = (mod_all[l + 1], norm1_g[l + 1], w_in[l + 1].astype(BF16))
            xl, abc, q, kv = _layer_transition(abc, dn, w_out_b, xl, mod, nxt[0], lat_row, norm2_g[l], ffn_w,
                                               nxt[1], nxt[2], conv_params, sw_b, sbias, l + 1,
                                               tiles.transition_rows, tiles.hidden)
            xc, abc_c, qc, kvc = _layer_transition(abc_c, dn_c, w_out_b, xc, mod, nxt[0], ctx_row, norm2_g[l], ffn_w,
                                                   nxt[1], nxt[2], conv_params, sw_b, sbias, l + 1, n_ctx,
                                                   tiles.hidden)
        else:
            rw = jnp.zeros((d, LANES), F32).at[:, :N_EXPERTS].set(router_w[l // 2]).astype(BF16)
            rb = jnp.full((1, LANES), NEG_INF, F32).at[0, :N_EXPERTS].set(router_b[l // 2])
            xl, h, info, info_t = _merge_router(abc, dn, w_out_b, xl, mod, lat_row, norm2_g[l], rw, rb, tm)
            t = bsz * seq
            info = info.reshape(t, LANES)
            tt = tiles.combine_rows
            expert_idx = jnp.stack([info_t[:, :, k, :].reshape(t) for k in range(2)]).astype(jnp.int32)
            slot_tok, blk_exp, n_active, win_base, loc_digits = _route_plan(expert_idx, tiles.expert_rows, tt)
            w1, w3, w2 = moe_w1[l // 2], moe_w3[l // 2], moe_w2[l // 2]
            y = _moe_experts(h.reshape(t, d // LANES, LANES), slot_tok, blk_exp, n_active, w1, w3, w2,
                             tiles.expert_rows, tiles.hidden)
            out = _combine_final(win_base, y, loc_digits, info, xl.reshape(t, d), mod, lambda i: i * tt // seq,
                                 final_g, tt).reshape(bsz, seq, d)
    return out
```
